```python
import jax
import jax.numpy as jnp
from jax import lax
import numpy as np

D_MODEL = 2048
BATCH = 8
SEQ = 2048
DEPTH = 2

GRID_W = 64
CTX_LEN = 256
HA = 8
DH_A = 64
WIN_ROWS = 8
WIN_COLS = 16
HB = 4
DK_B = 128
DV_B = 256
DEC_RANK = 16
GATE_TAU = 16.0
GLA_CHUNK = 64
GC = 4
CG = 128
MIX_CHUNK = 128
N_EXPERTS = 16
EXPERT_FF = 2048
CAP_FACTOR = 2
ROPE_BASE = 10000.0
EPS = 1e-6

WA = HA * DH_A
KB = HB * DK_B
VB = HB * DV_B
WC = GC * CG
MIX_WIDTH = WA + VB + WC
IN_SPLITS = (WA, WA, WA, KB, VB, 2 * DEC_RANK, KB, VB, WC, WC)
N_IN = sum(IN_SPLITS)
CTX_SPLITS = (WA, WA, KB, VB, 2 * DEC_RANK)
CTX_KV_END = WA + sum(CTX_SPLITS)

kernel_name = 'hybrid_na_gla_gmlp_ecmoe_block'


def _split(t, sizes):
    cuts = [int(s) for s in np.cumsum(sizes)[:-1]]
    return jnp.split(t, cuts, axis=-1)


def _heads(t, n):
    return t.reshape(t.shape[:-1] + (n, t.shape[-1] // n))


def rmsnorm(x, g):
    xf = x.astype(jnp.float32)
    y = xf * lax.rsqrt(jnp.mean(xf * xf, axis=-1, keepdims=True) + EPS)
    return y.astype(x.dtype) * g


def layernorm(x, g, b):
    xf = x.astype(jnp.float32)
    mu = jnp.mean(xf, axis=-1, keepdims=True)
    var = jnp.mean(jnp.square(xf - mu), axis=-1, keepdims=True)
    return ((xf - mu) * lax.rsqrt(var + EPS)).astype(x.dtype) * g + b


def modulate(h, shift, scale):
    return h * (1 + scale) + shift


def axial_rope_tables(n_tok):
    t = jnp.arange(n_tok)
    half = DK_B // 2
    inv = ROPE_BASE ** (-jnp.arange(0, half, 2, dtype=jnp.float32) / half)

    def tab(pos):
        ang = pos.astype(jnp.float32)[:, None] * inv[None, :]
        return jnp.concatenate([ang, ang], axis=-1)

    ang = jnp.concatenate([tab(t // GRID_W), tab(t % GRID_W)], axis=-1)
    return jnp.cos(ang), jnp.sin(ang)


def apply_axial_rope(x, cos, sin):
    qd = DK_B // 4
    z = x.reshape(x.shape[:-1] + (2, 2, qd))
    rot = jnp.concatenate([-z[..., 1:, :], z[..., :1, :]], axis=-2).reshape(x.shape)
    cos = cos.astype(x.dtype)[None, :, None, :]
    sin = sin.astype(x.dtype)[None, :, None, :]
    return x * cos + rot * sin


def neighbourhood_attention(q, k, v, k_ctx, v_ctx, rpb, rows):
    bsz = q.shape[0]
    wr = min(WIN_ROWS, rows)
    r = np.arange(rows)
    row_idx = np.clip(r - wr // 2, 0, rows - wr)[:, None] + np.arange(wr)[None, :]
    cq = np.arange(GRID_W)
    col_start = np.clip(cq - WIN_COLS // 2, 0, GRID_W - WIN_COLS)
    col_mask = (cq[None, :] >= col_start[:, None]) & (cq[None, :] < col_start[:, None] + WIN_COLS)
    dr = row_idx - r[:, None] + WIN_ROWS - 1
    dc = np.clip(cq[None, :] - cq[:, None] + WIN_COLS - 1, 0, 2 * WIN_COLS - 2)
    bias = rpb[:, dr[:, None, :, None], dc[None, :, None, :]]
    qg = q.reshape(bsz, rows, GRID_W, HA, DH_A)
    kw = k.reshape(bsz, rows, GRID_W, HA, DH_A)[:, row_idx]
    vw = v.reshape(bsz, rows, GRID_W, HA, DH_A)[:, row_idx]
    s_loc = jnp.einsum('brqhd,brikhd->bhrqik', qg, kw).astype(jnp.float32) + bias[None].astype(jnp.float32)
    s_loc = jnp.where(col_mask[None, None, None, :, None, :], s_loc, -jnp.inf)
    s_loc = s_loc.reshape(bsz, HA, rows, GRID_W, wr * GRID_W)
    s_ctx = jnp.einsum('brqhd,blhd->bhrql', qg, k_ctx).astype(jnp.float32)
    p = jax.nn.softmax(jnp.concatenate([s_loc, s_ctx], axis=-1), axis=-1).astype(v.dtype)
    p_loc = p[..., :wr * GRID_W].reshape(bsz, HA, rows, GRID_W, wr, GRID_W)
    p_ctx = p[..., wr * GRID_W:]
    o = jnp.einsum('bhrqik,brikhd->brqhd', p_loc, vw) + jnp.einsum('bhrql,blhd->brqhd', p_ctx, v_ctx)
    return o.reshape(bsz, rows * GRID_W, WA)


def context_attention(q, k, v):
    s = jnp.einsum('blhd,bmhd->bhlm', q, k).astype(jnp.float32)
    p = jax.nn.softmax(s, axis=-1).astype(v.dtype)
    o = jnp.einsum('bhlm,bmhd->blhd', p, v)
    return o.reshape(o.shape[:2] + (WA,))


def log_decays(dec, w_dec, b_dec):
    def one(z):
        return (jax.nn.log_sigmoid(z.astype(jnp.float32)) / GATE_TAU).reshape(z.shape[:-1] + (HB, DK_B))
    z_f = dec[..., :DEC_RANK] @ w_dec[0] + b_dec[0]
    z_b = dec[..., DEC_RANK:] @ w_dec[1] + b_dec[1]
    return one(z_f), one(z_b)


def gla_scan(q, k, v, log_a, s0):
    bsz, n, h, _ = q.shape
    nc = n // GLA_CHUNK

    def chunks(a):
        a = a.astype(jnp.float32)
        return jnp.moveaxis(a.reshape((bsz, nc, GLA_CHUNK) + a.shape[2:]), 1, 0)

    tril = jnp.tril(jnp.ones((GLA_CHUNK, GLA_CHUNK), dtype=bool))[None, :, :, None, None]

    def step(s, inp):
        qc, kc, vc, lc = inp
        b = jnp.cumsum(lc, axis=1)
        o_inter = jnp.einsum('bihk,bhkv->bihv', qc * jnp.exp(b), s)
        dec = jnp.exp(jnp.where(tril, b[:, :, None] - b[:, None, :], -jnp.inf))
        att = jnp.einsum('bihk,bjhk,bijhk->bhij', qc, kc, dec)
        o_intra = jnp.einsum('bhij,bjhv->bihv', att, vc)
        b_last = b[:, -1]
        s = jnp.exp(b_last)[..., None] * s + jnp.einsum('bjhk,bjhv->bhkv', kc * jnp.exp(b_last[:, None] - b), vc)
        return s, o_inter + o_intra

    s_fin, o = lax.scan(step, s0, (chunks(q), chunks(k), chunks(v), chunks(log_a)))
    o = jnp.moveaxis(o, 0, 1).reshape(bsz, n, h, v.shape[-1])
    return o, s_fin


def gla_bidir(q, k, v, la_f, la_b, s0_f, s0_b):
    flip = lambda a: jnp.flip(a, axis=1)
    o_f, s_f = gla_scan(q, k, v, la_f, s0_f)
    o_b, s_b = gla_scan(flip(q), flip(k), flip(v), flip(la_b), s0_b)
    return o_f + flip(o_b), s_f, s_b


def gla_final_state(k, v, log_a):
    b = jnp.cumsum(log_a, axis=1)
    w = jnp.exp(b[:, -1:] - b)
    return jnp.einsum('bthk,bthv->bhkv', k.astype(jnp.float32) * w, v.astype(jnp.float32))


def gla_output(o, g, gain):
    o = rmsnorm(o.astype(g.dtype), gain.reshape(HB, DV_B))
    return o.reshape(g.shape) * jax.nn.silu(g)


def chunk_mlp(u, v, ln_g, ln_b, w_sp, b_sp):
    bsz, n, _ = u.shape
    u = jax.nn.gelu(u)
    v = jax.nn.gelu(v)
    v = layernorm(_heads(v, GC), ln_g.reshape(GC, CG), ln_b.reshape(GC, CG))
    v = v.reshape(bsz, n // MIX_CHUNK, MIX_CHUNK, GC, CG)
    s = jnp.einsum('gpq,bnqgc->bnpgc', w_sp, v) + b_sp.T[:, :, None]
    return u * s.reshape(bsz, n, WC)


def ec_moe(h, w_router, w_gate, w_up, w_down):
    bsz, n, d = h.shape
    cap = CAP_FACTOR * n // N_EXPERTS
    aff = jax.nn.softmax((h @ w_router).astype(jnp.float32), axis=-1)
    g, idx = lax.top_k(jnp.swapaxes(aff, 1, 2), cap)
    xs = jax.vmap(lambda hb, ib: hb[ib])(h, idx)
    a = jnp.einsum('becd,edf->becf', xs, w_gate)
    u = jnp.einsum('becd,edf->becf', xs, w_up)
    y = jnp.einsum('becf,efd->becd', jax.nn.silu(a) * u, w_down) * g[..., None].astype(h.dtype)
    return jax.vmap(lambda yb, ib: jnp.zeros((n, d), yb.dtype).at[ib.reshape(-1)].add(yb.reshape(-1, d)))(y, idx)


def setup_inputs(seed: int = 0) -> dict:
    key = jax.random.key(seed)
    ks = jax.random.split(key, 24)
    f32 = jnp.float32
    D = D_MODEL

    def nrm(k, shape, scale):
        return jax.random.normal(k, shape, f32) * scale

    return {
        'x': nrm(ks[0], (BATCH, SEQ, D), 1.0),
        'c': nrm(ks[1], (BATCH, D), 1.0),
        'ctx': nrm(ks[2], (BATCH, CTX_LEN, D), 1.0),
        'c_ctx': nrm(ks[3], (D,), 1.0),
        'w_ada': nrm(ks[4], (DEPTH, D, 6 * D), 0.5 * D ** -0.5),
        'b_ada': nrm(ks[5], (DEPTH, 6 * D), 0.02),
        'g_pre_mix': 1.0 + nrm(ks[6], (DEPTH, D), 0.1),
        'g_post_mix': 1.0 + nrm(ks[7], (DEPTH, D), 0.1),
        'g_pre_ffn': 1.0 + nrm(ks[8], (DEPTH, D), 0.1),
        'g_post_ffn': 1.0 + nrm(ks[9], (DEPTH, D), 0.1),
        'w_in': nrm(ks[10], (DEPTH, D, N_IN), D ** -0.5),
        'w_dec': nrm(ks[11], (DEPTH, 2, DEC_RANK, KB), DEC_RANK ** -0.5),
        'b_dec': nrm(ks[12], (DEPTH, 2, KB), 0.5),
        'rpb': nrm(ks[13], (DEPTH, HA, 2 * WIN_ROWS - 1, 2 * WIN_COLS - 1), 0.2),
        'g_gla': 1.0 + nrm(ks[14], (DEPTH, VB), 0.1),
        'ln_v_g': 1.0 + nrm(ks[15], (DEPTH, WC), 0.1),
        'ln_v_b': nrm(ks[16], (DEPTH, WC), 0.02),
        'w_sp': nrm(ks[17], (DEPTH, GC, MIX_CHUNK, MIX_CHUNK), MIX_CHUNK ** -0.5),
        'b_sp': 1.0 + nrm(ks[18], (DEPTH, GC, MIX_CHUNK), 0.1),
        'w_out': nrm(ks[19], (DEPTH, MIX_WIDTH, D), MIX_WIDTH ** -0.5),
        'w_router': nrm(ks[20], (DEPTH, D, N_EXPERTS), D ** -0.5),
        'w_gate': nrm(ks[21], (DEPTH, N_EXPERTS, D, EXPERT_FF), D ** -0.5),
        'w_up': nrm(ks[22], (DEPTH, N_EXPERTS, D, EXPERT_FF), D ** -0.5),
        'w_down': nrm(ks[23], (DEPTH, N_EXPERTS, EXPERT_FF, D), EXPERT_FF ** -0.5),
    }


def reference(x, c, ctx, c_ctx, w_ada, b_ada, g_pre_mix, g_post_mix, g_pre_ffn, g_post_ffn,
              w_in, w_dec, b_dec, rpb, g_gla, ln_v_g, ln_v_b, w_sp, b_sp, w_out,
              w_router, w_gate, w_up, w_down):
    bsz, n_tok, d = x.shape
    rows = n_tok // GRID_W
    cos, sin = axial_rope_tables(n_tok)
    s_lat = jax.nn.silu(c)
    s_ctx = jax.nn.silu(c_ctx)
    zero_state = jnp.zeros((bsz, HB, DK_B, DV_B), jnp.float32)
    q_scale_a = DH_A ** -0.5
    q_scale_b = DK_B ** -0.5
    xc = ctx
    for l in range(DEPTH):
        last = l == DEPTH - 1
        sh1, sc1, gt1, sh2, sc2, gt2 = jnp.split((s_lat @ w_ada[l] + b_ada[l])[:, None, :], 6, axis=-1)
        n_mod = 2 if last else 6
        modc = jnp.split(s_ctx @ w_ada[l][:, :n_mod * d] + b_ada[l][:n_mod * d], n_mod)

        h = modulate(rmsnorm(x, g_pre_mix[l]), sh1, sc1)
        qa, ka, va, kb, vb, dec, qb, gb, uc, vc = _split(h @ w_in[l], IN_SPLITS)
        hc = modulate(rmsnorm(xc, g_pre_mix[l]), modc[0], modc[1])
        if last:
            kac, vac, kbc, vbc, decc = _split(hc @ w_in[l][:, WA:CTX_KV_END], CTX_SPLITS)
        else:
            qac, kac, vac, kbc, vbc, decc, qbc, gbc, ucc, vcc = _split(hc @ w_in[l], IN_SPLITS)
        kac_h, vac_h = _heads(kac, HA), _heads(vac, HA)

        o_a = neighbourhood_attention(_heads(qa, HA) * q_scale_a, _heads(ka, HA), _heads(va, HA),
                                      kac_h, vac_h, rpb[l], rows)

        la_f, la_b = log_decays(dec, w_dec[l], b_dec[l])
        lac_f, lac_b = log_decays(decc, w_dec[l], b_dec[l])
        kbc_h, vbc_h = _heads(kbc, HB), _heads(vbc, HB)
        if last:
            st_f = gla_final_state(kbc_h, vbc_h, lac_f)
            st_b = gla_final_state(jnp.flip(kbc_h, 1), jnp.flip(vbc_h, 1), jnp.flip(lac_b, 1))
        else:
            o_bc, st_f, st_b = gla_bidir(_heads(qbc, HB) * q_scale_b, kbc_h, vbc_h, lac_f, lac_b,
                                         zero_state, zero_state)
        qb_h = apply_axial_rope(_heads(qb, HB), cos, sin) * q_scale_b
        kb_h = apply_axial_rope(_heads(kb, HB), cos, sin)
        o_b, _, _ = gla_bidir(qb_h, kb_h, _heads(vb, HB), la_f, la_b, st_f, st_b)

        o_c = chunk_mlp(uc, vc, ln_v_g[l], ln_v_b[l], w_sp[l], b_sp[l])

        y = jnp.concatenate([o_a, gla_output(o_b, gb, g_gla[l]), o_c], axis=-1) @ w_out[l]
        x = x + gt1 * rmsnorm(y, g_post_mix[l])
        if not last:
            o_ac = context_attention(_heads(qac, HA) * q_scale_a, kac_h, vac_h)
            o_cc = chunk_mlp(ucc, vcc, ln_v_g[l], ln_v_b[l], w_sp[l], b_sp[l])
            yc = jnp.concatenate([o_ac, gla_output(o_bc, gbc, g_gla[l]), o_cc], axis=-1) @ w_out[l]
            xc = xc + modc[2] * rmsnorm(yc, g_post_mix[l])

        h2 = modulate(rmsnorm(x, g_pre_ffn[l]), sh2, sc2)
        x = x + gt2 * rmsnorm(ec_moe(h2, w_router[l], w_gate[l], w_up[l], w_down[l]), g_post_ffn[l])
        if not last:
            hc2 = modulate(rmsnorm(xc, g_pre_ffn[l]), modc[3], modc[4])
            xc = xc + modc[5] * rmsnorm(ec_moe(hc2, w_router[l], w_gate[l], w_up[l], w_down[l]), g_post_ffn[l])
    return x
```

```python
import functools

import numpy as np
import jax
import jax.numpy as jnp
from jax import lax
from jax.experimental import pallas as pl
from jax.experimental.pallas import tpu as pltpu

F32 = jnp.float32
BF16 = jnp.bfloat16

D = 2048
T = 2048
L = 256
TT = T + L
BLK = 256
NBLK = TT // BLK
NLAT = T // BLK
DEPTH = 2
GRID_W = 64
ROWS = T // GRID_W
HA, DH_A = 8, 64
WIN_ROWS, WIN_COLS = 8, 16
HB, DK_B, DV_B = 4, 128, 256
DEC_RANK = 16
GATE_TAU = 16.0
GC, CG, MIX_CHUNK = 4, 128, 128
N_EXPERTS = 16
EXPERT_FF = 2048
CAP_FACTOR = 2
ROPE_BASE = 10000.0
EPS = 1e-6
WA, KB, VB, WC = HA * DH_A, HB * DK_B, HB * DV_B, GC * CG

NP = 2 * VB + 7 * 512 + 128
C_VB, C_GB, C_QA, C_KA, C_VA, C_KB, C_QB, C_UC, C_VC, C_DEC = (
    0, 1024, 2048, 2560, 3072, 3584, 4096, 4608, 5120, 5632)

NA_QROWS = 4
NA_KROWS = 12
NA_KEYS = NA_KROWS * GRID_W
NEG = -1e30

GLA_LEVELS = 8
FF_TILE = 256
CAP_LAT = CAP_FACTOR * T // N_EXPERTS
CAP_CTX = CAP_FACTOR * L // N_EXPERTS
VMEM_LIMIT = 56 * 1024 * 1024


def _cp(n_axes):
    return pltpu.CompilerParams(dimension_semantics=("arbitrary",) * n_axes,
                                vmem_limit_bytes=VMEM_LIMIT)


def _dot(a, b):
    return jnp.dot(a, b, preferred_element_type=F32)


def _dot_nt(a, b):
    return lax.dot_general(a, b, (((1,), (1,)), ((), ())), preferred_element_type=F32)


def _dot_tn(a, b):
    return lax.dot_general(a, b, (((0,), (0,)), ((), ())), preferred_element_type=F32)


def _rms(x):
    return x * lax.rsqrt(jnp.mean(x * x, axis=-1, keepdims=True) + EPS)


def _sigmoid(x):
    return 1.0 / (1.0 + jnp.exp(-x))


def _split_bf16(x):
    hi = x.astype(BF16)
    lo = (x - hi.astype(F32)).astype(BF16)
    return hi, lo


def _ada_kernel(c_ref, w_ref, b_ref, o_ref):
    cv = c_ref[...]
    s = cv * _sigmoid(cv)
    o_ref[...] = jnp.dot(s, w_ref[...], preferred_element_type=F32,
                         precision=lax.Precision.HIGHEST) + b_ref[...]


def _ada_call(c_all, w_ada, b_ada):
    tn = 1024
    n6 = w_ada.shape[-1]
    return pl.pallas_call(
        _ada_kernel,
        grid=(DEPTH, n6 // tn),
        in_specs=[pl.BlockSpec((16, D), lambda l, n: (0, 0)),
                  pl.BlockSpec((None, D, tn), lambda l, n: (l, 0, n)),
                  pl.BlockSpec((None, 1, tn), lambda l, n: (l, 0, n))],
        out_specs=pl.BlockSpec((None, 16, tn), lambda l, n: (l, 0, n)),
        out_shape=jax.ShapeDtypeStruct((DEPTH, 16, n6), F32),
        compiler_params=_cp(2), name="ada",
    )(c_all, w_ada, b_ada.reshape(DEPTH, 1, n6))


def _prenorm_kernel(x_ref, g_ref, mod_ref, h_ref):
    y = _rms(x_ref[...]) * g_ref[...]
    h_ref[...] = (y * (1.0 + mod_ref[1:2, :]) + mod_ref[0:1, :]).astype(BF16)


def _prenorm_call(xall, g, modtab):
    bsz = xall.shape[0]
    return pl.pallas_call(
        _prenorm_kernel,
        grid=(bsz, NBLK),
        in_specs=[pl.BlockSpec((None, BLK, D), lambda b, j: (b, j, 0)),
                  pl.BlockSpec((1, D), lambda b, j: (0, 0)),
                  pl.BlockSpec((None, None, 8, D), lambda b, j: (b, j // NLAT, 0, 0))],
        out_specs=pl.BlockSpec((None, BLK, D), lambda b, j: (b, j, 0)),
        out_shape=jax.ShapeDtypeStruct((bsz, TT, D), BF16),
        compiler_params=_cp(2), name="prenorm",
    )(xall, g.reshape(1, D), modtab)


def _proj_kernel(h_ref, w_ref, o_ref):
    o_ref[...] = _dot(h_ref[...], w_ref[...]).astype(BF16)


def _proj_call(h2d, w_pack):
    m = h2d.shape[0]
    tm = 1024 if m % 1024 == 0 else 768
    tn = NP // 5
    return pl.pallas_call(
        _proj_kernel,
        grid=(NP // tn, m // tm),
        in_specs=[pl.BlockSpec((tm, D), lambda n, i: (i, 0)),
                  pl.BlockSpec((D, tn), lambda n, i: (0, n))],
        out_specs=pl.BlockSpec((tm, tn), lambda n, i: (i, n)),
        out_shape=jax.ShapeDtypeStruct((m, NP), BF16),
        compiler_params=_cp(2), name="proj_in",
    )(h2d, w_pack)


def _softmax_pv(s_list, v_list):
    m = s_list[0].max(axis=-1, keepdims=True)
    for s in s_list[1:]:
        m = jnp.maximum(m, s.max(axis=-1, keepdims=True))
    acc = None
    den = None
    for s, v in zip(s_list, v_list):
        p = jnp.exp(s - m)
        d = p.sum(axis=-1, keepdims=True)
        o = _dot(p.astype(BF16), v)
        acc = o if acc is None else acc + o
        den = d if den is None else den + d
    return acc / den


def _na_kernel(q_ref, k_ref, v_ref, bias_ref, hm_ref, o_ref):
    j = pl.program_id(1)
    lane = lax.broadcasted_iota(jnp.int32, (BLK, 128), 1)
    low = lane < DH_A

    def run(local_start):
        for p in range(HA // 2):
            sl = slice(128 * p, 128 * p + 128)
            q2 = q_ref[:, sl]
            kc = k_ref[T:TT, sl]
            vc = v_ref[T:TT, sl]
            if local_start is not None:
                kl = k_ref[pl.ds(local_start, NA_KEYS), sl]
                vl = v_ref[pl.ds(local_start, NA_KEYS), sl]
            pair = []
            for hh in range(2):
                qm = (q2.astype(F32) * hm_ref[hh:hh + 1, :]).astype(BF16)
                s_ctx = _dot_nt(qm, kc)
                if local_start is not None:
                    s_loc = _dot_nt(qm, kl) + bias_ref[2 * p + hh]
                    pair.append(_softmax_pv([s_loc, s_ctx], [vl, vc]))
                else:
                    pair.append(_softmax_pv([s_ctx], [vc]))
            o_ref[:, sl] = jnp.where(low, pair[0], pair[1]).astype(BF16)

    @pl.when(j < NLAT)
    def _():
        krow = jnp.clip(j * NA_QROWS - WIN_ROWS // 2, 0, ROWS - NA_KROWS)
        run(pl.multiple_of(krow * GRID_W, GRID_W))

    @pl.when(j == NLAT)
    def _():
        run(None)


def _na_call(P, bias, hmask, n_blocks):
    bsz = P.shape[0]

    def bias_idx(b, j):
        return (jnp.where(j == 0, 0, jnp.where(j == NLAT - 1, 2, 1)), 0, 0, 0)

    return pl.pallas_call(
        _na_kernel,
        grid=(bsz, n_blocks),
        in_specs=[pl.BlockSpec((None, BLK, WA), lambda b, j: (b, j, C_QA // WA)),
                  pl.BlockSpec((None, TT, WA), lambda b, j: (b, 0, C_KA // WA)),
                  pl.BlockSpec((None, TT, WA), lambda b, j: (b, 0, C_VA // WA)),
                  pl.BlockSpec((None, HA, BLK, NA_KEYS), bias_idx),
                  pl.BlockSpec((2, 128), lambda b, j: (0, 0))],
        out_specs=pl.BlockSpec((None, BLK, WA), lambda b, j: (b, j, 0)),
        out_shape=jax.ShapeDtypeStruct((bsz, TT, WA), BF16),
        compiler_params=_cp(2), name="nbr_attn",
    )(P, P, P, bias, hmask)


def _na_bias(rpb_l):
    tabs = []
    qi = np.arange(NA_QROWS)[:, None, None, None]
    cq = np.arange(GRID_W)[None, :, None, None]
    kj = np.arange(NA_KROWS)[None, None, :, None]
    ck = np.arange(GRID_W)[None, None, None, :]
    for rb in (0, 1, NLAT - 1):
        r = rb * NA_QROWS + qi
        ks = int(np.clip(rb * NA_QROWS - WIN_ROWS // 2, 0, ROWS - NA_KROWS))
        kr = ks + kj
        rs = np.clip(r - WIN_ROWS // 2, 0, ROWS - WIN_ROWS)
        cs = np.clip(cq - WIN_COLS // 2, 0, GRID_W - WIN_COLS)
        valid = (kr >= rs) & (kr < rs + WIN_ROWS) & (ck >= cs) & (ck < cs + WIN_COLS)
        dr = np.clip(kr - r + WIN_ROWS - 1, 0, 2 * WIN_ROWS - 2)
        dc = np.clip(ck - cq + WIN_COLS - 1, 0, 2 * WIN_COLS - 2)
        shape = (NA_QROWS, GRID_W, NA_KROWS, GRID_W)
        dr = np.broadcast_to(dr, shape).reshape(BLK, NA_KEYS)
        dc = np.broadcast_to(dc, shape).reshape(BLK, NA_KEYS)
        valid = np.broadcast_to(valid, shape).reshape(BLK, NA_KEYS)
        tabs.append(jnp.where(valid[None], rpb_l[:, dr, dc], NEG))
    return jnp.stack(tabs).astype(F32)


def _gla_matrices():
    n = BLK
    i = np.arange(n)[:, None]
    t = np.arange(n)[None, :]
    out = np.zeros((2, (GLA_LEVELS + 2) * n, n), np.float32)
    for lvl in range(GLA_LEVELS):
        s = n >> (lvl + 1)
        blk0 = (i // (2 * s)) * (2 * s)
        m = blk0 + s - 1
        fwd = np.where(i > m, (t > m) & (t <= i), (t > i) & (t <= m))
        bwd = np.where(i <= m, (t >= i) & (t <= m), (t > m) & (t < i))
        out[0, lvl * n:(lvl + 1) * n] = fwd
        out[1, lvl * n:(lvl + 1) * n] = bwd
    out[0, GLA_LEVELS * n:(GLA_LEVELS + 1) * n] = t <= i
    out[1, GLA_LEVELS * n:(GLA_LEVELS + 1) * n] = t >= i
    out[0, (GLA_LEVELS + 1) * n:] = t > i
    out[1, (GLA_LEVELS + 1) * n:] = t < i
    return out


def _gla_kernel(kb_ref, qb_ref, vb_ref, dec_ref, wd_ref, bd_ref, cos_ref, sa_ref, sb_ref,
                m_ref, o_ref, st_ref):
    d = pl.program_id(2)
    step = pl.program_id(3)

    @pl.when(step == 0)
    def _():
        st_ref[...] = jnp.zeros_like(st_ref)

    z = _dot(dec_ref[...], wd_ref[...].astype(BF16)) + bd_ref[...]
    la = (jnp.minimum(z, 0.0) - jnp.log1p(jnp.exp(-jnp.abs(z)))) * (1.0 / GATE_TAU)
    la_hi, la_lo = _split_bf16(la)

    def seg_exp(lvl):
        mm = m_ref[lvl * BLK:(lvl + 1) * BLK, :]
        return jnp.exp(_dot(mm, la_hi) + _dot(mm, la_lo))

    row = lax.broadcasted_iota(jnp.int32, (BLK, DK_B), 0)
    ri = lax.broadcasted_iota(jnp.int32, (BLK, BLK), 0)
    ci = lax.broadcasted_iota(jnp.int32, (BLK, BLK), 1)
    xij = ri ^ ci
    cosv, sav, sbv = cos_ref[...], sa_ref[...], sb_ref[...]

    def rope(x):
        return x * cosv + pltpu.roll(x, DK_B - 32, 1) * sav + pltpu.roll(x, 32, 1) * sbv

    e_in = seg_exp(GLA_LEVELS)
    e_out = seg_exp(GLA_LEVELS + 1)
    e_lvls = [seg_exp(lvl) for lvl in range(GLA_LEVELS)]

    outs = []
    for hh in range(2):
        ks = slice(DK_B * hh, DK_B * (hh + 1))
        q = rope(qb_ref[:, ks].astype(F32)) * (DK_B ** -0.5)
        k = rope(kb_ref[:, ks].astype(F32))
        v = vb_ref[:, DV_B * hh:DV_B * (hh + 1)]
        att = jnp.where(ri == ci, jnp.sum(q * k, axis=-1, keepdims=True), 0.0)
        for lvl in range(GLA_LEVELS):
            s = BLK >> (lvl + 1)
            e = e_lvls[lvl][:, ks]
            is_q = ((row // s) & 1) != d
            qt = jnp.where(is_q, q * e, 0.0).astype(BF16)
            kt = jnp.where(is_q, 0.0, k * e).astype(BF16)
            a = _dot_nt(qt, kt)
            att = att + (a if lvl == 0 else jnp.where(xij < 2 * s, a, 0.0))
        e_q = e_in[:, ks]
        st = st_ref[hh]
        o = _dot_nt((q * e_q).astype(BF16), st.astype(BF16)) + _dot(att.astype(BF16), v)
        outs.append(o.astype(BF16))
        carry = jnp.where(d == 0, e_q[BLK - 1:BLK, :], e_q[0:1, :])
        kt = (k * e_out[:, ks]).astype(BF16)
        st_ref[hh] = st * carry + _dot_tn(v, kt)
    o_ref[...] = jnp.concatenate(outs, axis=-1)


def _gla_call(P, wd_pad, bd, cos_t, sa_t, sb_t, mstack):
    bsz = P.shape[0]

    def blk(dd, s):
        return jnp.where(s == 0, NLAT, jnp.where(dd == 0, s - 1, NLAT - s))

    return pl.pallas_call(
        _gla_kernel,
        grid=(bsz, 2, 2, NBLK),
        in_specs=[
            pl.BlockSpec((None, BLK, 2 * DK_B), lambda b, hp, dd, s: (b, blk(dd, s), C_KB // 256 + hp)),
            pl.BlockSpec((None, BLK, 2 * DK_B), lambda b, hp, dd, s: (b, blk(dd, s), C_QB // 256 + hp)),
            pl.BlockSpec((None, BLK, 2 * DV_B), lambda b, hp, dd, s: (b, blk(dd, s), C_VB // 512 + hp)),
            pl.BlockSpec((None, BLK, 128), lambda b, hp, dd, s: (b, blk(dd, s), C_DEC // 128)),
            pl.BlockSpec((None, 128, 2 * DK_B), lambda b, hp, dd, s: (dd, 0, hp)),
            pl.BlockSpec((None, 1, 2 * DK_B), lambda b, hp, dd, s: (dd, 0, hp)),
            pl.BlockSpec((BLK, DK_B), lambda b, hp, dd, s: (blk(dd, s), 0)),
            pl.BlockSpec((BLK, DK_B), lambda b, hp, dd, s: (blk(dd, s), 0)),
            pl.BlockSpec((BLK, DK_B), lambda b, hp, dd, s: (blk(dd, s), 0)),
            pl.BlockSpec((None, (GLA_LEVELS + 2) * BLK, BLK), lambda b, hp, dd, s: (dd, 0, 0)),
        ],
        out_specs=pl.BlockSpec((None, None, BLK, 2 * DV_B),
                               lambda b, hp, dd, s: (dd, b, blk(dd, s), hp)),
        out_shape=jax.ShapeDtypeStruct((2, bsz, TT, VB), BF16),
        scratch_shapes=[pltpu.VMEM((2, DV_B, DK_B), F32)],
        compiler_params=_cp(4), name="gla",
    )(P, P, P, P, wd_pad, bd, cos_t, sa_t, sb_t, mstack)


def _rope_tables():
    t = jnp.arange(T)
    half = DK_B // 2
    inv = ROPE_BASE ** (-jnp.arange(0, half, 2, dtype=F32) / half)

    def tab(pos):
        ang = pos.astype(F32)[:, None] * inv[None, :]
        return jnp.concatenate([ang, ang], axis=-1)

    ang = jnp.concatenate([tab(t // GRID_W), tab(t % GRID_W)], axis=-1)
    cos, sin = jnp.cos(ang), jnp.sin(ang)
    first = (np.arange(DK_B) % half) < (half // 2)
    sa = jnp.where(first[None, :], -sin, 0.0)
    sb = jnp.where(first[None, :], 0.0, sin)
    ident = jnp.ones((L, DK_B), F32)
    zero = jnp.zeros((L, DK_B), F32)
    return (jnp.concatenate([cos, ident]), jnp.concatenate([sa, zero]), jnp.concatenate([sb, zero]))


def _gelu(x):
    return 0.5 * x * (1.0 + jnp.tanh(0.7978845608028654 * (x + 0.044715 * (x * x * x))))


def _gmlp_kernel(u_ref, v_ref, g_ref, b_ref, w_ref, bs_ref, o_ref):
    for ch in range(BLK // MIX_CHUNK):
        rs = slice(ch * MIX_CHUNK, (ch + 1) * MIX_CHUNK)
        u = _gelu(u_ref[rs, :].astype(F32))
        v = _gelu(v_ref[rs, :].astype(F32))
        for g in range(GC):
            cs = slice(g * CG, (g + 1) * CG)
            vg = v[:, cs]
            mu = jnp.mean(vg, axis=-1, keepdims=True)
            var = jnp.mean(jnp.square(vg - mu), axis=-1, keepdims=True)
            vn = (vg - mu) * lax.rsqrt(var + EPS) * g_ref[:, cs] + b_ref[:, cs]
            s = _dot(w_ref[g].astype(BF16), vn.astype(BF16)) + bs_ref[g]
            o_ref[rs, cs] = (u[:, cs] * s).astype(BF16)


def _gmlp_call(P, ln_g, ln_b, w_sp, bs_rep, n_blocks):
    bsz = P.shape[0]
    return pl.pallas_call(
        _gmlp_kernel,
        grid=(bsz, n_blocks),
        in_specs=[pl.BlockSpec((None, BLK, WC), lambda b, j: (b, j, C_UC // WC)),
                  pl.BlockSpec((None, BLK, WC), lambda b, j: (b, j, C_VC // WC)),
                  pl.BlockSpec((1, WC), lambda b, j: (0, 0)),
                  pl.BlockSpec((1, WC), lambda b, j: (0, 0)),
                  pl.BlockSpec((GC, MIX_CHUNK, MIX_CHUNK), lambda b, j: (0, 0, 0)),
                  pl.BlockSpec((GC, MIX_CHUNK, 128), lambda b, j: (0, 0, 0))],
        out_specs=pl.BlockSpec((None, BLK, WC), lambda b, j: (b, j, 0)),
        out_shape=jax.ShapeDtypeStruct((bsz, TT, WC), BF16),
        compiler_params=_cp(2), name="gmlp",
    )(P, P, ln_g.reshape(1, WC), ln_b.reshape(1, WC), w_sp, bs_rep)


def _out_kernel(oa_ref, of_ref, ob_ref, gb_ref, oc_ref, w_ref, x_ref, gg_ref, gpost_ref, gpre_ref,
                mod_ref, wr_ref, xo_ref, h2_ref, lg_ref):
    o = of_ref[...].astype(F32) + ob_ref[...].astype(F32)
    gb = gb_ref[...].astype(F32)
    parts = []
    for h in range(HB):
        cs = slice(h * DV_B, (h + 1) * DV_B)
        gh = gb[:, cs]
        parts.append((_rms(o[:, cs]) * gg_ref[:, cs] * (gh * _sigmoid(gh))).astype(BF16))
    gl = jnp.concatenate(parts, axis=-1)
    y = (_dot(oa_ref[...], w_ref[0:WA, :]) + _dot(gl, w_ref[WA:WA + VB, :])
         + _dot(oc_ref[...], w_ref[WA + VB:, :]))
    x1 = x_ref[...] + mod_ref[2:3, :] * (_rms(y) * gpost_ref[...])
    xo_ref[...] = x1
    h2 = (_rms(x1) * gpre_ref[...]) * (1.0 + mod_ref[4:5, :]) + mod_ref[3:4, :]
    h2_ref[...] = h2.astype(BF16)
    hh, hl = _split_bf16(h2)
    wh, wl = _split_bf16(wr_ref[...])
    lg_ref[...] = _dot(hh, wh) + _dot(hl, wh) + _dot(hh, wl)


def _out_call(o_a, o_g, P, o_c, w_out_b, xall, g_gla, g_post, g_pre, modtab, wr_pad, n_blocks):
    bsz = P.shape[0]
    rows = n_blocks * BLK
    rowspec = lambda w: pl.BlockSpec((None, BLK, w), lambda b, j: (b, j, 0))
    vec = lambda w: pl.BlockSpec((1, w), lambda b, j: (0, 0))
    return pl.pallas_call(
        _out_kernel,
        grid=(bsz, n_blocks),
        in_specs=[rowspec(WA),
                  pl.BlockSpec((None, None, BLK, VB), lambda b, j: (0, b, j, 0)),
                  pl.BlockSpec((None, None, BLK, VB), lambda b, j: (1, b, j, 0)),
                  pl.BlockSpec((None, BLK, VB), lambda b, j: (b, j, C_GB // VB)),
                  rowspec(WC),
                  pl.BlockSpec((D, D), lambda b, j: (0, 0)),
                  rowspec(D),
                  vec(VB), vec(D), vec(D),
                  pl.BlockSpec((None, None, 8, D), lambda b, j: (b, j // NLAT, 0, 0)),
                  pl.BlockSpec((D, 128), lambda b, j: (0, 0))],
        out_specs=[rowspec(D), rowspec(D), rowspec(128)],
        out_shape=[jax.ShapeDtypeStruct((bsz, rows, D), F32),
                   jax.ShapeDtypeStruct((bsz, rows, D), BF16),
                   jax.ShapeDtypeStruct((bsz, rows, 128), F32)],
        compiler_params=_cp(2), name="proj_out",
    )(o_a, o_g, o_g, P, o_c, w_out_b, xall, g_gla.reshape(1, VB), g_post.reshape(1, D),
      g_pre.reshape(1, D), modtab, wr_pad)


def _route_set(lg, cap, upper):
    n = lg.shape[0]
    lt = lg.T[:N_EXPERTS, :]
    ex = jnp.exp(lt - lt.max(axis=0, keepdims=True))
    aff = ex / ex.sum(axis=0, keepdims=True)
    bits = pltpu.bitcast(aff, jnp.int32)
    capf = jnp.float32(cap)

    def body(i, prefix):
        cand = prefix | jnp.left_shift(jnp.int32(1), 30 - i)
        cnt = jnp.sum(jnp.where(bits >= cand, 1.0, 0.0), axis=1, keepdims=True)
        return jnp.where(cnt >= capf, cand, prefix)

    thr = lax.fori_loop(0, 31, body, jnp.zeros((N_EXPERTS, 1), jnp.int32))
    gt = jnp.where(bits > thr, 1.0, 0.0)
    eq = jnp.where(bits == thr, 1.0, 0.0)
    need = capf - gt.sum(axis=1, keepdims=True)
    rank_eq = _dot(eq.astype(BF16), upper)
    sel = gt + eq * jnp.where(rank_eq < need, 1.0, 0.0)
    slot = _dot(sel.astype(BF16), upper)
    slot = jnp.where(sel > 0.5, slot, -1.0)
    pad = jnp.full((128 - N_EXPERTS, n), -1.0, F32)
    slot_t = jnp.concatenate([slot, pad], axis=0).T
    return slot, aff, slot_t


def _route_kernel(lg_ref, u_ref, *out_refs, with_ctx):
    sl, af, st = _route_set(lg_ref[0:T, :], CAP_LAT, u_ref[...])
    out_refs[0][...] = sl
    out_refs[1][...] = af
    out_refs[2][0:T, :] = st
    if with_ctx:
        sl, af, st = _route_set(lg_ref[T:TT, :], CAP_CTX, u_ref[0:L, 0:L])
        out_refs[3][...] = sl
        out_refs[4][...] = af
        out_refs[2][T:TT, :] = st


def _route_call(lg, upper, with_ctx):
    bsz, rows, _ = lg.shape
    en = lambda n: pl.BlockSpec((None, N_EXPERTS, n), lambda b: (b, 0, 0))
    out_specs = [en(T), en(T), pl.BlockSpec((None, rows, 128), lambda b: (b, 0, 0))]
    out_shape = [jax.ShapeDtypeStruct((bsz, N_EXPERTS, T), F32),
                 jax.ShapeDtypeStruct((bsz, N_EXPERTS, T), F32),
                 jax.ShapeDtypeStruct((bsz, rows, 128), F32)]
    if with_ctx:
        out_specs += [en(L), en(L)]
        out_shape += [jax.ShapeDtypeStruct((bsz, N_EXPERTS, L), F32)] * 2
    return pl.pallas_call(
        functools.partial(_route_kernel, with_ctx=with_ctx),
        grid=(bsz,),
        in_specs=[pl.BlockSpec((None, rows, 128), lambda b: (b, 0, 0)),
                  pl.BlockSpec((T, T), lambda b: (0, 0))],
        out_specs=out_specs, out_shape=out_shape,
        compiler_params=_cp(1), name="route",
    )(lg, upper)


def _gather_kernel(slot_ref, aff_ref, h_ref, xs_ref, g_ref, *, cap):
    e = pl.program_id(1)
    n = h_ref.shape[0]
    srow = slot_ref[pl.ds(e, 1), :]
    arow = aff_ref[pl.ds(e, 1), :]
    sid = lax.broadcasted_iota(jnp.int32, (cap, n), 0).astype(F32)
    hit = sid == srow
    xs_ref[...] = _dot(jnp.where(hit, 1.0, 0.0).astype(BF16), h_ref[...]).astype(BF16)
    g = jnp.sum(jnp.where(hit, arow, 0.0), axis=1, keepdims=True)
    g_ref[...] = jnp.broadcast_to(g, (cap, 128))


def _gather_call(slot, aff, h2, n, cap, blk_idx):
    bsz = h2.shape[0]
    return pl.pallas_call(
        functools.partial(_gather_kernel, cap=cap),
        grid=(bsz, N_EXPERTS),
        in_specs=[pl.BlockSpec((None, N_EXPERTS, n), lambda b, e: (b, 0, 0)),
                  pl.BlockSpec((None, N_EXPERTS, n), lambda b, e: (b, 0, 0)),
                  pl.BlockSpec((None, n, D), lambda b, e: (b, blk_idx, 0))],
        out_specs=[pl.BlockSpec((None, cap, D), lambda b, e: (e, b, 0)),
                   pl.BlockSpec((None, cap, 128), lambda b, e: (e, b, 0))],
        out_shape=[jax.ShapeDtypeStruct((N_EXPERTS, bsz * cap, D), BF16),
                   jax.ShapeDtypeStruct((N_EXPERTS, bsz * cap, 128), F32)],
        compiler_params=_cp(2), name="gather",
    )(slot, aff, h2)


def _ffn_kernel(*refs, rl, rc):
    if rc:
        xl_ref, xc_ref, gl_ref, gc_ref, wg_ref, wu_ref, wd_ref, yl_ref, yc_ref, xcat, acc = refs
    else:
        xl_ref, gl_ref, wg_ref, wu_ref, wd_ref, yl_ref, xcat, acc = refs
    f = pl.program_id(2)
    nf = pl.num_programs(2)

    @pl.when(f == 0)
    def _():
        xcat[0:rl, :] = xl_ref[...]
        if rc:
            xcat[rl:rl + rc, :] = xc_ref[...]

    xv = xcat[...]
    a = _dot(xv, wg_ref[...].astype(BF16))
    u = _dot(xv, wu_ref[...].astype(BF16))
    hm = ((a * _sigmoid(a)) * u).astype(BF16)
    contrib = _dot(hm, wd_ref[...].astype(BF16))

    @pl.when(f == 0)
    def _():
        acc[...] = contrib

    @pl.when(f > 0)
    def _():
        acc[...] += contrib

    @pl.when(f == nf - 1)
    def _():
        for c in range(D // 128):
            cs = slice(128 * c, 128 * (c + 1))
            yl_ref[:, cs] = (acc[0:rl, cs] * gl_ref[...]).astype(BF16)
            if rc:
                yc_ref[:, cs] = (acc[rl:rl + rc, cs] * gc_ref[...]).astype(BF16)


def _ffn_call(layer, xs_l, g_l, xs_c, g_c, w_gate, w_up, w_down):
    rows_l = xs_l.shape[1]
    nh = 2
    rl = rows_l // nh
    rc = 0 if xs_c is None else xs_c.shape[1] // nh
    nf = EXPERT_FF // FF_TILE
    in_specs = [pl.BlockSpec((None, rl, D), lambda e, m, f: (e, m, 0))]
    args = [xs_l]
    if rc:
        in_specs.append(pl.BlockSpec((None, rc, D), lambda e, m, f: (e, m, 0)))
        args.append(xs_c)
    in_specs.append(pl.BlockSpec((None, rl, 128), lambda e, m, f: (e, m, 0)))
    args.append(g_l)
    if rc:
        in_specs.append(pl.BlockSpec((None, rc, 128), lambda e, m, f: (e, m, 0)))
        args.append(g_c)
    in_specs += [pl.BlockSpec((None, None, D, FF_TILE), lambda e, m, f: (layer, e, 0, f)),
                 pl.BlockSpec((None, None, D, FF_TILE), lambda e, m, f: (layer, e, 0, f)),
                 pl.BlockSpec((None, None, FF_TILE, D), lambda e, m, f: (layer, e, f, 0))]
    args += [w_gate, w_up, w_down]
    out_specs = [pl.BlockSpec((None, rl, D), lambda e, m, f: (e, m, 0))]
    out_shape = [jax.ShapeDtypeStruct((N_EXPERTS, rows_l, D), BF16)]
    if rc:
        out_specs.append(pl.BlockSpec((None, rc, D), lambda e, m, f: (e, m, 0)))
        out_shape.append(jax.ShapeDtypeStruct((N_EXPERTS, xs_c.shape[1], D), BF16))
    return pl.pallas_call(
        functools.partial(_ffn_kernel, rl=rl, rc=rc),
        grid=(N_EXPERTS, nh, nf),
        in_specs=in_specs, out_specs=out_specs, out_shape=out_shape,
        scratch_shapes=[pltpu.VMEM((rl + rc, D), BF16), pltpu.VMEM((rl + rc, D), F32)],
        compiler_params=_cp(3), name="expert_ffn",
    )(*args)


def _scatter_kernel(*refs, cap, ne, has_next, aliased):
    refs = list(refs)
    slot_ref, y_ref, x_ref, gpost_ref, mod_ref = refs[:5]
    refs = refs[5:]
    if has_next:
        gpre_ref, modn_ref = refs[:2]
        refs = refs[2:]
    if aliased:
        refs = refs[2 if has_next else 1:]
    xo_ref = refs[0]
    hn_ref = refs[1] if has_next else None
    acc = refs[-1]
    eg = pl.program_id(2)
    ng = pl.num_programs(2)
    rows = slot_ref.shape[0]
    kk = ne * cap
    er = lax.broadcasted_iota(jnp.int32, (128, kk), 0)
    ec = lax.broadcasted_iota(jnp.int32, (128, kk), 1)
    expand = jnp.where(er == eg * ne + ec // cap, 1.0, 0.0).astype(BF16)
    spread = _dot(slot_ref[...].astype(BF16), expand)
    want = (lax.broadcasted_iota(jnp.int32, (rows, kk), 1) % cap).astype(F32)
    pt = jnp.where(spread == want, 1.0, 0.0).astype(BF16)
    contrib = _dot(pt, y_ref[...].reshape(kk, D))

    @pl.when(eg == 0)
    def _():
        acc[...] = contrib

    @pl.when(eg > 0)
    def _():
        acc[...] += contrib

    @pl.when(eg == ng - 1)
    def _():
        x2 = x_ref[...] + mod_ref[5:6, :] * (_rms(acc[...]) * gpost_ref[...])
        xo_ref[...] = x2
        if has_next:
            hn = (_rms(x2) * gpre_ref[...]) * (1.0 + modn_ref[1:2, :]) + modn_ref[0:1, :]
            hn_ref[...] = hn.astype(BF16)


def _scatter_call(slot_t, y, x_mid, g_post, modtab, g_pre_next, modtab_next, *, rows, row0_blk, cap,
                  ne, seg, out_rows, prev=None):
    bsz = x_mid.shape[0]
    has_next = g_pre_next is not None
    n_tiles = (T // rows) if seg == 0 else 1
    ng = N_EXPERTS // ne
    vec = pl.BlockSpec((1, D), lambda b, r, g: (0, 0))
    modspec = pl.BlockSpec((None, None, 8, D), lambda b, r, g: (b, seg, 0, 0))
    rowmap = lambda b, r, g: (b, row0_blk + r, 0)
    in_specs = [pl.BlockSpec((None, rows, 128), rowmap),
                pl.BlockSpec((ne, cap, D), lambda b, r, g: (g, b, 0)),
                pl.BlockSpec((None, rows, D), rowmap),
                vec, modspec]
    args = [slot_t, y, x_mid, g_post.reshape(1, D), modtab]
    if has_next:
        in_specs += [vec, modspec]
        args += [g_pre_next.reshape(1, D), modtab_next]
    aliases = {}
    if prev is not None:
        for k, arr in enumerate(prev):
            aliases[len(args)] = k
            in_specs.append(pl.BlockSpec(memory_space=pl.ANY))
            args.append(arr)
    out_specs = [pl.BlockSpec((None, rows, D), rowmap)]
    out_shape = [jax.ShapeDtypeStruct((bsz, out_rows, D), F32)]
    if has_next:
        out_specs.append(pl.BlockSpec((None, rows, D), rowmap))
        out_shape.append(jax.ShapeDtypeStruct((bsz, out_rows, D), BF16))
    return pl.pallas_call(
        functools.partial(_scatter_kernel, cap=cap, ne=ne, has_next=has_next, aliased=prev is not None),
        grid=(bsz, n_tiles, ng),
        in_specs=in_specs, out_specs=out_specs, out_shape=out_shape,
        scratch_shapes=[pltpu.VMEM((rows, D), F32)],
        input_output_aliases=aliases,
        compiler_params=_cp(3), name="scatter_lat" if seg == 0 else "scatter_ctx",
    )(*args)


def _pack_w_in(w):
    qa, ka, va, kb, vb, dec, qb, gb, uc, vc = jnp.split(
        w, [int(s) for s in np.cumsum((WA, WA, WA, KB, VB, 2 * DEC_RANK, KB, VB, WC))], axis=-1)
    pad = jnp.zeros((D, 128 - 2 * DEC_RANK), w.dtype)
    return jnp.concatenate([vb, gb, qa, ka, va, kb, qb, uc, vc, dec, pad], axis=-1).astype(BF16)


def kernel(x, c, ctx, c_ctx, w_ada, b_ada, g_pre_mix, g_post_mix, g_pre_ffn, g_post_ffn, w_in, w_dec,
           b_dec, rpb, g_gla, ln_v_g, ln_v_b, w_sp, b_sp, w_out, w_router, w_gate, w_up, w_down):
    bsz = x.shape[0]
    xall = jnp.concatenate([x, ctx], axis=1)
    c_all = jnp.zeros((16, D), F32).at[:bsz].set(c).at[8].set(c_ctx)
    mods = _ada_call(c_all, w_ada, b_ada).reshape(DEPTH, 16, 6, D)
    lat = mods[:, :bsz]
    cx = jnp.broadcast_to(mods[:, 8:9], lat.shape)
    modtab = jnp.pad(jnp.stack([lat, cx], axis=2), ((0, 0), (0, 0), (0, 0), (0, 2), (0, 0)))

    cos_t, sa_t, sb_t = _rope_tables()
    mstack = jnp.asarray(_gla_matrices(), BF16)
    hmask = jnp.asarray(np.stack([(np.arange(128) < DH_A), (np.arange(128) >= DH_A)]) * DH_A ** -0.5, F32)
    upper = (lax.broadcasted_iota(jnp.int32, (T, T), 0) < lax.broadcasted_iota(jnp.int32, (T, T), 1)).astype(BF16)

    h = _prenorm_call(xall, g_pre_mix[0], modtab[0])
    for l in range(DEPTH):
        last = l == DEPTH - 1
        nb = NLAT if last else NBLK
        P = _proj_call(h.reshape(bsz * TT, D), _pack_w_in(w_in[l])).reshape(bsz, TT, NP)
        o_a = _na_call(P, _na_bias(rpb[l]), hmask, nb)
        wd_pad = jnp.zeros((2, 128, KB), F32)
        wd_pad = wd_pad.at[0, :DEC_RANK].set(w_dec[l, 0]).at[1, DEC_RANK:2 * DEC_RANK].set(w_dec[l, 1])
        o_g = _gla_call(P, wd_pad, b_dec[l].reshape(2, 1, KB), cos_t, sa_t, sb_t, mstack)
        bs_rep = jnp.broadcast_to(b_sp[l][:, :, None], (GC, MIX_CHUNK, 128))
        o_c = _gmlp_call(P, ln_v_g[l], ln_v_b[l], w_sp[l], bs_rep, nb)
        wr_pad = jnp.pad(w_router[l], ((0, 0), (0, 128 - N_EXPERTS)))
        x_mid, h2, lg = _out_call(o_a, o_g, P, o_c, w_out[l].astype(BF16), xall, g_gla[l], g_post_mix[l],
                                  g_pre_ffn[l], modtab[l], wr_pad, nb)
        routed = _route_call(lg, upper, not last)
        slot_l, aff_l, slot_t = routed[:3]
        xs_l, gt_l = _gather_call(slot_l, aff_l, h2, T, CAP_LAT, 0)
        if last:
            (y_l,) = _ffn_call(l, xs_l, gt_l, None, None, w_gate, w_up, w_down)
            (xall,) = _scatter_call(slot_t, y_l, x_mid, g_post_ffn[l], modtab[l], None, None, rows=512,
                                    row0_blk=0, cap=CAP_LAT, ne=4, seg=0, out_rows=T)
        else:
            slot_c, aff_c = routed[3:]
            xs_c, gt_c = _gather_call(slot_c, aff_c, h2, L, CAP_CTX, NLAT)
            y_l, y_c = _ffn_call(l, xs_l, gt_l, xs_c, gt_c, w_gate, w_up, w_down)
            part = _scatter_call(slot_t, y_l, x_mid, g_post_ffn[l], modtab[l], g_pre_mix[l + 1],
                                 modtab[l + 1], rows=512, row0_blk=0, cap=CAP_LAT, ne=4, seg=0, out_rows=TT)
            xall, h = _scatter_call(slot_t, y_c, x_mid, g_post_ffn[l], modtab[l], g_pre_mix[l + 1],
                                    modtab[l + 1], rows=L, row0_blk=NLAT, cap=CAP_CTX, ne=N_EXPERTS, seg=1,
                                    out_rows=TT, prev=part)
    return xall
```

```python
import functools

import numpy as np
import jax
import jax.numpy as jnp
from jax import lax
from jax.experimental import pallas as pl
from jax.experimental.pallas import tpu as pltpu

F32 = jnp.float32
BF16 = jnp.bfloat16

D = 2048
T = 2048
L = 256
TT = T + L
BLK = 256
NBLK = TT // BLK
NLAT = T // BLK
DEPTH = 2
GRID_W = 64
ROWS = T // GRID_W
HA, DH_A = 8, 64
WIN_ROWS, WIN_COLS = 8, 16
HB, DK_B, DV_B = 4, 128, 256
DEC_RANK = 16
GATE_TAU = 16.0
GC, CG, MIX_CHUNK = 4, 128, 128
N_EXPERTS = 16
EXPERT_FF = 2048
CAP_FACTOR = 2
ROPE_BASE = 10000.0
EPS = 1e-6
WA, KB, VB, WC = HA * DH_A, HB * DK_B, HB * DV_B, GC * CG

NP = 2 * VB + 7 * 512 + 128
C_VB, C_GB, C_QA, C_KA, C_VA, C_KB, C_QB, C_UC, C_VC, C_DEC = (
    0, 1024, 2048, 2560, 3072, 3584, 4096, 4608, 5120, 5632)

NA_QROWS = 4
NA_KROWS = 12
NA_KEYS = NA_KROWS * GRID_W
NEG = -1e30

GLA_LEVELS = 8
FF_TILE = 512
CAP_LAT = CAP_FACTOR * T // N_EXPERTS
CAP_CTX = CAP_FACTOR * L // N_EXPERTS
VMEM_LIMIT = 56 * 1024 * 1024


def _cp(n_axes):
    return pltpu.CompilerParams(dimension_semantics=("arbitrary",) * n_axes,
                                vmem_limit_bytes=VMEM_LIMIT)


def _dot(a, b):
    return jnp.dot(a, b, preferred_element_type=F32)


def _dot_nt(a, b):
    return lax.dot_general(a, b, (((1,), (1,)), ((), ())), preferred_element_type=F32)


def _dot_tn(a, b):
    return lax.dot_general(a, b, (((0,), (0,)), ((), ())), preferred_element_type=F32)


def _rms(x):
    return x * lax.rsqrt(jnp.mean(x * x, axis=-1, keepdims=True) + EPS)


def _sigmoid(x):
    return 1.0 / (1.0 + jnp.exp(-x))


def _split_bf16(x):
    hi = x.astype(BF16)
    lo = (x - hi.astype(F32)).astype(BF16)
    return hi, lo


def _ada_kernel(c_ref, w_ref, b_ref, o_ref):
    cv = c_ref[...]
    s = cv * _sigmoid(cv)
    o_ref[...] = jnp.dot(s, w_ref[...], preferred_element_type=F32,
                         precision=lax.Precision.HIGHEST) + b_ref[...]


def _ada_call(c_all, w_ada, b_ada):
    tn = 1024
    n6 = w_ada.shape[-1]
    return pl.pallas_call(
        _ada_kernel,
        grid=(DEPTH, n6 // tn),
        in_specs=[pl.BlockSpec((16, D), lambda l, n: (0, 0)),
                  pl.BlockSpec((None, D, tn), lambda l, n: (l, 0, n)),
                  pl.BlockSpec((None, 1, tn), lambda l, n: (l, 0, n))],
        out_specs=pl.BlockSpec((None, 16, tn), lambda l, n: (l, 0, n)),
        out_shape=jax.ShapeDtypeStruct((DEPTH, 16, n6), F32),
        compiler_params=_cp(2), name="ada",
    )(c_all, w_ada, b_ada.reshape(DEPTH, 1, n6))


def _prenorm_kernel(x_ref, g_ref, mod_ref, h_ref):
    y = _rms(x_ref[...]) * g_ref[...]
    h_ref[...] = (y * (1.0 + mod_ref[1:2, :]) + mod_ref[0:1, :]).astype(BF16)


def _prenorm_call(xall, g, modtab):
    bsz = xall.shape[0]
    return pl.pallas_call(
        _prenorm_kernel,
        grid=(bsz, NBLK),
        in_specs=[pl.BlockSpec((None, BLK, D), lambda b, j: (b, j, 0)),
                  pl.BlockSpec((1, D), lambda b, j: (0, 0)),
                  pl.BlockSpec((None, None, 8, D), lambda b, j: (b, j // NLAT, 0, 0))],
        out_specs=pl.BlockSpec((None, BLK, D), lambda b, j: (b, j, 0)),
        out_shape=jax.ShapeDtypeStruct((bsz, TT, D), BF16),
        compiler_params=_cp(2), name="prenorm",
    )(xall, g.reshape(1, D), modtab)


def _proj_kernel(h_ref, w_ref, o_ref):
    o_ref[...] = _dot(h_ref[...], w_ref[...]).astype(BF16)


def _proj_call(h2d, w_pack):
    m = h2d.shape[0]
    tm = 1024 if m % 1024 == 0 else 768
    tn = NP // 5
    return pl.pallas_call(
        _proj_kernel,
        grid=(NP // tn, m // tm),
        in_specs=[pl.BlockSpec((tm, D), lambda n, i: (i, 0)),
                  pl.BlockSpec((D, tn), lambda n, i: (0, n))],
        out_specs=pl.BlockSpec((tm, tn), lambda n, i: (i, n)),
        out_shape=jax.ShapeDtypeStruct((m, NP), BF16),
        compiler_params=_cp(2), name="proj_in",
    )(h2d, w_pack)


def _softmax_pv(s_list, v_list):
    m = s_list[0].max(axis=-1, keepdims=True)
    for s in s_list[1:]:
        m = jnp.maximum(m, s.max(axis=-1, keepdims=True))
    acc = None
    den = None
    for s, v in zip(s_list, v_list):
        p = jnp.exp(s - m)
        d = p.sum(axis=-1, keepdims=True)
        o = _dot(p.astype(BF16), v)
        acc = o if acc is None else acc + o
        den = d if den is None else den + d
    return acc / den


def _na_kernel(q_ref, k_ref, v_ref, bias_ref, hm_ref, o_ref):
    j = pl.program_id(1)
    lane = lax.broadcasted_iota(jnp.int32, (BLK, 128), 1)
    low = lane < DH_A

    def run(local_start):
        for p in range(HA // 2):
            sl = slice(128 * p, 128 * p + 128)
            q2 = q_ref[:, sl]
            kc = k_ref[T:TT, sl]
            vc = v_ref[T:TT, sl]
            if local_start is not None:
                kl = k_ref[pl.ds(local_start, NA_KEYS), sl]
                vl = v_ref[pl.ds(local_start, NA_KEYS), sl]
            pair = []
            for hh in range(2):
                qm = (q2.astype(F32) * hm_ref[hh:hh + 1, :]).astype(BF16)
                s_ctx = _dot_nt(qm, kc)
                if local_start is not None:
                    s_loc = _dot_nt(qm, kl) + bias_ref[2 * p + hh]
                    pair.append(_softmax_pv([s_loc, s_ctx], [vl, vc]))
                else:
                    pair.append(_softmax_pv([s_ctx], [vc]))
            o_ref[:, sl] = jnp.where(low, pair[0], pair[1]).astype(BF16)

    @pl.when(j < NLAT)
    def _():
        krow = jnp.clip(j * NA_QROWS - WIN_ROWS // 2, 0, ROWS - NA_KROWS)
        run(pl.multiple_of(krow * GRID_W, GRID_W))

    @pl.when(j == NLAT)
    def _():
        run(None)


def _na_call(P, bias, hmask, n_blocks):
    bsz = P.shape[0]

    def bias_idx(b, j):
        return (jnp.where(j == 0, 0, jnp.where(j == NLAT - 1, 2, 1)), 0, 0, 0)

    return pl.pallas_call(
        _na_kernel,
        grid=(bsz, n_blocks),
        in_specs=[pl.BlockSpec((None, BLK, WA), lambda b, j: (b, j, C_QA // WA)),
                  pl.BlockSpec((None, TT, WA), lambda b, j: (b, 0, C_KA // WA)),
                  pl.BlockSpec((None, TT, WA), lambda b, j: (b, 0, C_VA // WA)),
                  pl.BlockSpec((None, HA, BLK, NA_KEYS), bias_idx),
                  pl.BlockSpec((2, 128), lambda b, j: (0, 0))],
        out_specs=pl.BlockSpec((None, BLK, WA), lambda b, j: (b, j, 0)),
        out_shape=jax.ShapeDtypeStruct((bsz, TT, WA), BF16),
        compiler_params=_cp(2), name="nbr_attn",
    )(P, P, P, bias, hmask)


def _na_bias(rpb_l):
    n_dr, n_dc = 2 * WIN_ROWS - 1, 2 * WIN_COLS - 1
    cq = np.arange(GRID_W)[:, None]
    ck = np.arange(GRID_W)[None, :]
    cs = np.clip(cq - WIN_COLS // 2, 0, GRID_W - WIN_COLS)
    col_ok = (ck >= cs) & (ck < cs + WIN_COLS)
    dc = np.clip(ck - cq + WIN_COLS - 1, 0, n_dc - 1)
    col_sel = (dc[:, :, None] == np.arange(n_dc)).astype(np.float32)
    qi = np.arange(NA_QROWS)[:, None]
    kj = np.arange(NA_KROWS)[None, :]
    row_sel, row_ok = [], []
    for rb in (0, 1, NLAT - 1):
        r = rb * NA_QROWS + qi
        kr = int(np.clip(rb * NA_QROWS - WIN_ROWS // 2, 0, ROWS - NA_KROWS)) + kj
        rs = np.clip(r - WIN_ROWS // 2, 0, ROWS - WIN_ROWS)
        ok = (kr >= rs) & (kr < rs + WIN_ROWS)
        dr = kr - r + WIN_ROWS - 1
        row_sel.append(((dr[:, :, None] == np.arange(n_dr)) & ok[:, :, None]).astype(np.float32))
        row_ok.append(ok)
    row_sel, row_ok = np.stack(row_sel), np.stack(row_ok)
    colx = jnp.einsum("qke,hde->hdqk", col_sel, rpb_l, precision=lax.Precision.HIGHEST)
    tab = jnp.einsum("tikd,hdqc->thiqkc", row_sel, colx, precision=lax.Precision.HIGHEST)
    valid = row_ok[:, None, :, None, :, None] & col_ok[None, None, None, :, None, :]
    tab = jnp.where(valid, tab, NEG)
    return tab.reshape(3, HA, BLK, NA_KEYS).astype(F32)


def _gla_matrices():
    n = BLK
    i = np.arange(n)[:, None]
    t = np.arange(n)[None, :]
    out = np.zeros((2, (GLA_LEVELS + 2) * n, n), np.float32)
    for lvl in range(GLA_LEVELS):
        s = n >> (lvl + 1)
        blk0 = (i // (2 * s)) * (2 * s)
        m = blk0 + s - 1
        fwd = np.where(i > m, (t > m) & (t <= i), (t > i) & (t <= m))
        bwd = np.where(i <= m, (t >= i) & (t <= m), (t > m) & (t < i))
        out[0, lvl * n:(lvl + 1) * n] = fwd
        out[1, lvl * n:(lvl + 1) * n] = bwd
    out[0, GLA_LEVELS * n:(GLA_LEVELS + 1) * n] = t <= i
    out[1, GLA_LEVELS * n:(GLA_LEVELS + 1) * n] = t >= i
    out[0, (GLA_LEVELS + 1) * n:] = t > i
    out[1, (GLA_LEVELS + 1) * n:] = t < i
    return out


def _gla_kernel(kb_ref, qb_ref, vb_ref, dec_ref, wd_ref, bd_ref, cos_ref, sa_ref, sb_ref,
                m_ref, o_ref, st_ref):
    d = pl.program_id(2)
    step = pl.program_id(3)

    @pl.when(step == 0)
    def _():
        st_ref[...] = jnp.zeros_like(st_ref)

    z = _dot(dec_ref[...], wd_ref[...].astype(BF16)) + bd_ref[...]
    la = (jnp.minimum(z, 0.0) - jnp.log1p(jnp.exp(-jnp.abs(z)))) * (1.0 / GATE_TAU)
    la_b = la.astype(BF16)

    def seg_exp(lvl):
        return jnp.exp(_dot(m_ref[lvl * BLK:(lvl + 1) * BLK, :], la_b))

    row = lax.broadcasted_iota(jnp.int32, (BLK, DK_B), 0)
    ri = lax.broadcasted_iota(jnp.int32, (BLK, BLK), 0)
    ci = lax.broadcasted_iota(jnp.int32, (BLK, BLK), 1)
    xij = ri ^ ci
    cosv, sav, sbv = cos_ref[...], sa_ref[...], sb_ref[...]

    def rope(x):
        return x * cosv + pltpu.roll(x, DK_B - 32, 1) * sav + pltpu.roll(x, 32, 1) * sbv

    e_in = seg_exp(GLA_LEVELS)
    e_out = seg_exp(GLA_LEVELS + 1)
    e_lvls = [seg_exp(lvl) for lvl in range(GLA_LEVELS)]

    outs = []
    for hh in range(2):
        ks = slice(DK_B * hh, DK_B * (hh + 1))
        q = rope(qb_ref[:, ks].astype(F32)) * (DK_B ** -0.5)
        k = rope(kb_ref[:, ks].astype(F32))
        v = vb_ref[:, DV_B * hh:DV_B * (hh + 1)]
        att = jnp.where(ri == ci, jnp.sum(q * k, axis=-1, keepdims=True), 0.0)
        for lvl in range(GLA_LEVELS):
            s = BLK >> (lvl + 1)
            e = e_lvls[lvl][:, ks]
            is_q = ((row // s) & 1) != d
            qt = jnp.where(is_q, q * e, 0.0).astype(BF16)
            kt = jnp.where(is_q, 0.0, k * e).astype(BF16)
            a = _dot_nt(qt, kt)
            att = att + (a if lvl == 0 else jnp.where(xij < 2 * s, a, 0.0))
        e_q = e_in[:, ks]
        st = st_ref[hh]
        o = _dot_nt((q * e_q).astype(BF16), st.astype(BF16)) + _dot(att.astype(BF16), v)
        outs.append(o.astype(BF16))
        carry = jnp.where(d == 0, e_q[BLK - 1:BLK, :], e_q[0:1, :])
        kt = (k * e_out[:, ks]).astype(BF16)
        st_ref[hh] = st * carry + _dot_tn(v, kt)
    o_ref[...] = jnp.concatenate(outs, axis=-1)


def _gla_call(P, wd_pad, bd, cos_t, sa_t, sb_t, mstack):
    bsz = P.shape[0]

    def blk(dd, s):
        return jnp.where(s == 0, NLAT, jnp.where(dd == 0, s - 1, NLAT - s))

    return pl.pallas_call(
        _gla_kernel,
        grid=(bsz, 2, 2, NBLK),
        in_specs=[
            pl.BlockSpec((None, BLK, 2 * DK_B), lambda b, hp, dd, s: (b, blk(dd, s), C_KB // 256 + hp)),
            pl.BlockSpec((None, BLK, 2 * DK_B), lambda b, hp, dd, s: (b, blk(dd, s), C_QB // 256 + hp)),
            pl.BlockSpec((None, BLK, 2 * DV_B), lambda b, hp, dd, s: (b, blk(dd, s), C_VB // 512 + hp)),
            pl.BlockSpec((None, BLK, 128), lambda b, hp, dd, s: (b, blk(dd, s), C_DEC // 128)),
            pl.BlockSpec((None, 128, 2 * DK_B), lambda b, hp, dd, s: (dd, 0, hp)),
            pl.BlockSpec((None, 1, 2 * DK_B), lambda b, hp, dd, s: (dd, 0, hp)),
            pl.BlockSpec((BLK, DK_B), lambda b, hp, dd, s: (blk(dd, s), 0)),
            pl.BlockSpec((BLK, DK_B), lambda b, hp, dd, s: (blk(dd, s), 0)),
            pl.BlockSpec((BLK, DK_B), lambda b, hp, dd, s: (blk(dd, s), 0)),
            pl.BlockSpec((None, (GLA_LEVELS + 2) * BLK, BLK), lambda b, hp, dd, s: (dd, 0, 0)),
        ],
        out_specs=pl.BlockSpec((None, None, BLK, 2 * DV_B),
                               lambda b, hp, dd, s: (dd, b, blk(dd, s), hp)),
        out_shape=jax.ShapeDtypeStruct((2, bsz, TT, VB), BF16),
        scratch_shapes=[pltpu.VMEM((2, DV_B, DK_B), F32)],
        compiler_params=_cp(4), name="gla",
    )(P, P, P, P, wd_pad, bd, cos_t, sa_t, sb_t, mstack)


def _rope_tables():
    t = jnp.arange(T)
    half = DK_B // 2
    inv = ROPE_BASE ** (-jnp.arange(0, half, 2, dtype=F32) / half)

    def tab(pos):
        ang = pos.astype(F32)[:, None] * inv[None, :]
        return jnp.concatenate([ang, ang], axis=-1)

    ang = jnp.concatenate([tab(t // GRID_W), tab(t % GRID_W)], axis=-1)
    cos, sin = jnp.cos(ang), jnp.sin(ang)
    first = (np.arange(DK_B) % half) < (half // 2)
    sa = jnp.where(first[None, :], -sin, 0.0)
    sb = jnp.where(first[None, :], 0.0, sin)
    ident = jnp.ones((L, DK_B), F32)
    zero = jnp.zeros((L, DK_B), F32)
    return (jnp.concatenate([cos, ident]), jnp.concatenate([sa, zero]), jnp.concatenate([sb, zero]))


def _gelu(x):
    return 0.5 * x * (1.0 + jnp.tanh(0.7978845608028654 * (x + 0.044715 * (x * x * x))))


def _gmlp_kernel(u_ref, v_ref, g_ref, b_ref, w_ref, bs_ref, o_ref):
    for ch in range(BLK // MIX_CHUNK):
        rs = slice(ch * MIX_CHUNK, (ch + 1) * MIX_CHUNK)
        u = _gelu(u_ref[rs, :].astype(F32))
        v = _gelu(v_ref[rs, :].astype(F32))
        for g in range(GC):
            cs = slice(g * CG, (g + 1) * CG)
            vg = v[:, cs]
            mu = jnp.mean(vg, axis=-1, keepdims=True)
            var = jnp.mean(jnp.square(vg - mu), axis=-1, keepdims=True)
            vn = (vg - mu) * lax.rsqrt(var + EPS) * g_ref[:, cs] + b_ref[:, cs]
            s = _dot(w_ref[g].astype(BF16), vn.astype(BF16)) + bs_ref[g]
            o_ref[rs, cs] = (u[:, cs] * s).astype(BF16)


def _gmlp_call(P, ln_g, ln_b, w_sp, bs_rep, n_blocks):
    bsz = P.shape[0]
    return pl.pallas_call(
        _gmlp_kernel,
        grid=(bsz, n_blocks),
        in_specs=[pl.BlockSpec((None, BLK, WC), lambda b, j: (b, j, C_UC // WC)),
                  pl.BlockSpec((None, BLK, WC), lambda b, j: (b, j, C_VC // WC)),
                  pl.BlockSpec((1, WC), lambda b, j: (0, 0)),
                  pl.BlockSpec((1, WC), lambda b, j: (0, 0)),
                  pl.BlockSpec((GC, MIX_CHUNK, MIX_CHUNK), lambda b, j: (0, 0, 0)),
                  pl.BlockSpec((GC, MIX_CHUNK, 128), lambda b, j: (0, 0, 0))],
        out_specs=pl.BlockSpec((None, BLK, WC), lambda b, j: (b, j, 0)),
        out_shape=jax.ShapeDtypeStruct((bsz, TT, WC), BF16),
        compiler_params=_cp(2), name="gmlp",
    )(P, P, ln_g.reshape(1, WC), ln_b.reshape(1, WC), w_sp, bs_rep)


def _out_kernel(oa_ref, of_ref, ob_ref, gb_ref, oc_ref, w_ref, x_ref, gg_ref, gpost_ref, gpre_ref,
                mod_ref, wr_ref, xo_ref, h2_ref, lg_ref):
    o = of_ref[...].astype(F32) + ob_ref[...].astype(F32)
    gb = gb_ref[...].astype(F32)
    parts = []
    for h in range(HB):
        cs = slice(h * DV_B, (h + 1) * DV_B)
        gh = gb[:, cs]
        parts.append((_rms(o[:, cs]) * gg_ref[:, cs] * (gh * _sigmoid(gh))).astype(BF16))
    gl = jnp.concatenate(parts, axis=-1)
    y = (_dot(oa_ref[...], w_ref[0:WA, :]) + _dot(gl, w_ref[WA:WA + VB, :])
         + _dot(oc_ref[...], w_ref[WA + VB:, :]))
    x1 = x_ref[...] + mod_ref[2:3, :] * (_rms(y) * gpost_ref[...])
    xo_ref[...] = x1
    h2 = (_rms(x1) * gpre_ref[...]) * (1.0 + mod_ref[4:5, :]) + mod_ref[3:4, :]
    h2_ref[...] = h2.astype(BF16)
    hh, hl = _split_bf16(h2)
    wh, wl = _split_bf16(wr_ref[...])
    lg_ref[...] = _dot(hh, wh) + _dot(hl, wh) + _dot(hh, wl)


def _out_call(o_a, o_g, P, o_c, w_out_b, xall, g_gla, g_post, g_pre, modtab, wr_pad, n_blocks):
    bsz = P.shape[0]
    rows = n_blocks * BLK
    rowspec = lambda w: pl.BlockSpec((None, BLK, w), lambda b, j: (b, j, 0))
    vec = lambda w: pl.BlockSpec((1, w), lambda b, j: (0, 0))
    return pl.pallas_call(
        _out_kernel,
        grid=(bsz, n_blocks),
        in_specs=[rowspec(WA),
                  pl.BlockSpec((None, None, BLK, VB), lambda b, j: (0, b, j, 0)),
                  pl.BlockSpec((None, None, BLK, VB), lambda b, j: (1, b, j, 0)),
                  pl.BlockSpec((None, BLK, VB), lambda b, j: (b, j, C_GB // VB)),
                  rowspec(WC),
                  pl.BlockSpec((D, D), lambda b, j: (0, 0)),
                  rowspec(D),
                  vec(VB), vec(D), vec(D),
                  pl.BlockSpec((None, None, 8, D), lambda b, j: (b, j // NLAT, 0, 0)),
                  pl.BlockSpec((D, 128), lambda b, j: (0, 0))],
        out_specs=[rowspec(D), rowspec(D), rowspec(128)],
        out_shape=[jax.ShapeDtypeStruct((bsz, rows, D), F32),
                   jax.ShapeDtypeStruct((bsz, rows, D), BF16),
                   jax.ShapeDtypeStruct((bsz, rows, 128), F32)],
        compiler_params=_cp(2), name="proj_out",
    )(o_a, o_g, o_g, P, o_c, w_out_b, xall, g_gla.reshape(1, VB), g_post.reshape(1, D),
      g_pre.reshape(1, D), modtab, wr_pad)


def _route_set(lg, cap, upper):
    n = lg.shape[0]
    lt = lg.T[:N_EXPERTS, :]
    ex = jnp.exp(lt - lt.max(axis=0, keepdims=True))
    aff = ex / ex.sum(axis=0, keepdims=True)
    bits = pltpu.bitcast(aff, jnp.int32)
    capf = jnp.float32(cap)

    def body(i, prefix):
        cand = prefix | jnp.left_shift(jnp.int32(1), 30 - i)
        cnt = jnp.sum(jnp.where(bits >= cand, 1.0, 0.0), axis=1, keepdims=True)
        return jnp.where(cnt >= capf, cand, prefix)

    thr = lax.fori_loop(0, 31, body, jnp.zeros((N_EXPERTS, 1), jnp.int32))
    gt = jnp.where(bits > thr, 1.0, 0.0)
    eq = jnp.where(bits == thr, 1.0, 0.0)
    need = capf - gt.sum(axis=1, keepdims=True)
    rank_eq = _dot(eq.astype(BF16), upper)
    sel = gt + eq * jnp.where(rank_eq < need, 1.0, 0.0)
    slot = _dot(sel.astype(BF16), upper)
    slot = jnp.where(sel > 0.5, slot, -1.0)
    pad = jnp.full((128 - N_EXPERTS, n), -1.0, F32)
    slot_t = jnp.concatenate([slot, pad], axis=0).T
    return slot, aff, slot_t


def _route_kernel(lg_ref, u_ref, *out_refs, with_ctx):
    sl, af, st = _route_set(lg_ref[0:T, :], CAP_LAT, u_ref[...])
    out_refs[0][...] = sl
    out_refs[1][...] = af
    out_refs[2][0:T, :] = st
    if with_ctx:
        sl, af, st = _route_set(lg_ref[T:TT, :], CAP_CTX, u_ref[0:L, 0:L])
        out_refs[3][...] = sl
        out_refs[4][...] = af
        out_refs[2][T:TT, :] = st


def _route_call(lg, upper, with_ctx):
    bsz, rows, _ = lg.shape
    en = lambda n: pl.BlockSpec((None, N_EXPERTS, n), lambda b: (b, 0, 0))
    out_specs = [en(T), en(T), pl.BlockSpec((None, rows, 128), lambda b: (b, 0, 0))]
    out_shape = [jax.ShapeDtypeStruct((bsz, N_EXPERTS, T), F32),
                 jax.ShapeDtypeStruct((bsz, N_EXPERTS, T), F32),
                 jax.ShapeDtypeStruct((bsz, rows, 128), F32)]
    if with_ctx:
        out_specs += [en(L), en(L)]
        out_shape += [jax.ShapeDtypeStruct((bsz, N_EXPERTS, L), F32)] * 2
    return pl.pallas_call(
        functools.partial(_route_kernel, with_ctx=with_ctx),
        grid=(bsz,),
        in_specs=[pl.BlockSpec((None, rows, 128), lambda b: (b, 0, 0)),
                  pl.BlockSpec((T, T), lambda b: (0, 0))],
        out_specs=out_specs, out_shape=out_shape,
        compiler_params=_cp(1), name="route",
    )(lg, upper)


def _gather_kernel(slot_ref, aff_ref, h_ref, xs_ref, g_ref, *, cap):
    e = pl.program_id(1)
    n = h_ref.shape[0]
    srow = slot_ref[pl.ds(e, 1), :]
    arow = aff_ref[pl.ds(e, 1), :]
    sid = lax.broadcasted_iota(jnp.int32, (cap, n), 0).astype(F32)
    hit = sid == srow
    xs_ref[...] = _dot(jnp.where(hit, 1.0, 0.0).astype(BF16), h_ref[...]).astype(BF16)
    g = jnp.sum(jnp.where(hit, arow, 0.0), axis=1, keepdims=True)
    g_ref[...] = jnp.broadcast_to(g, (cap, 128))


def _gather_call(slot, aff, h2, n, cap, blk_idx):
    bsz = h2.shape[0]
    return pl.pallas_call(
        functools.partial(_gather_kernel, cap=cap),
        grid=(bsz, N_EXPERTS),
        in_specs=[pl.BlockSpec((None, N_EXPERTS, n), lambda b, e: (b, 0, 0)),
                  pl.BlockSpec((None, N_EXPERTS, n), lambda b, e: (b, 0, 0)),
                  pl.BlockSpec((None, n, D), lambda b, e: (b, blk_idx, 0))],
        out_specs=[pl.BlockSpec((None, cap, D), lambda b, e: (e, b, 0)),
                   pl.BlockSpec((None, cap, 128), lambda b, e: (e, b, 0))],
        out_shape=[jax.ShapeDtypeStruct((N_EXPERTS, bsz * cap, D), BF16),
                   jax.ShapeDtypeStruct((N_EXPERTS, bsz * cap, 128), F32)],
        compiler_params=_cp(2), name="gather",
    )(slot, aff, h2)


def _ffn_kernel(*refs, rl, rc, nf):
    if rc:
        xl_ref, xc_ref, gl_ref, gc_ref, wg_ref, wu_ref, wd_ref, yl_ref, yc_ref, xcat, hid = refs
    else:
        xl_ref, gl_ref, wg_ref, wu_ref, wd_ref, yl_ref, xcat, hid = refs
    s = pl.program_id(2)

    @pl.when(s == 0)
    def _():
        xcat[0:rl, :] = xl_ref[...]
        if rc:
            xcat[rl:rl + rc, :] = xc_ref[...]

    @pl.when(s < nf)
    def _():
        xv = xcat[...]
        a = _dot(xv, wg_ref[...].astype(BF16))
        u = _dot(xv, wu_ref[...].astype(BF16))
        hid[s] = ((a * _sigmoid(a)) * u).astype(BF16)

    @pl.when(s >= nf)
    def _():
        hm = jnp.concatenate([hid[k] for k in range(nf)], axis=1)
        y = _dot(hm, wd_ref[...].astype(BF16))
        for c in range(FF_TILE // 128):
            cs = slice(128 * c, 128 * (c + 1))
            yl_ref[:, cs] = (y[0:rl, cs] * gl_ref[...]).astype(BF16)
            if rc:
                yc_ref[:, cs] = (y[rl:rl + rc, cs] * gc_ref[...]).astype(BF16)


def _ffn_call(layer, xs_l, g_l, xs_c, g_c, w_gate, w_up, w_down):
    rows_l = xs_l.shape[1]
    nh = 2
    rl = rows_l // nh
    rc = 0 if xs_c is None else xs_c.shape[1] // nh
    nf = EXPERT_FF // FF_TILE
    nd = D // FF_TILE
    up = lambda e, m, s: (layer, e, 0, jnp.minimum(s, nf - 1))
    down = lambda s: jnp.maximum(s - nf, 0)
    in_specs = [pl.BlockSpec((None, rl, D), lambda e, m, s: (e, m, 0))]
    args = [xs_l]
    if rc:
        in_specs.append(pl.BlockSpec((None, rc, D), lambda e, m, s: (e, m, 0)))
        args.append(xs_c)
    in_specs.append(pl.BlockSpec((None, rl, 128), lambda e, m, s: (e, m, 0)))
    args.append(g_l)
    if rc:
        in_specs.append(pl.BlockSpec((None, rc, 128), lambda e, m, s: (e, m, 0)))
        args.append(g_c)
    in_specs += [pl.BlockSpec((None, None, D, FF_TILE), up),
                 pl.BlockSpec((None, None, D, FF_TILE), up),
                 pl.BlockSpec((None, None, EXPERT_FF, FF_TILE), lambda e, m, s: (layer, e, 0, down(s)))]
    args += [w_gate, w_up, w_down]
    out_specs = [pl.BlockSpec((None, rl, FF_TILE), lambda e, m, s: (e, m, down(s)))]
    out_shape = [jax.ShapeDtypeStruct((N_EXPERTS, rows_l, D), BF16)]
    if rc:
        out_specs.append(pl.BlockSpec((None, rc, FF_TILE), lambda e, m, s: (e, m, down(s))))
        out_shape.append(jax.ShapeDtypeStruct((N_EXPERTS, xs_c.shape[1], D), BF16))
    return pl.pallas_call(
        functools.partial(_ffn_kernel, rl=rl, rc=rc, nf=nf),
        grid=(N_EXPERTS, nh, nf + nd),
        in_specs=in_specs, out_specs=out_specs, out_shape=out_shape,
        scratch_shapes=[pltpu.VMEM((rl + rc, D), BF16), pltpu.VMEM((nf, rl + rc, FF_TILE), BF16)],
        compiler_params=_cp(3), name="expert_ffn",
    )(*args)


def _scatter_kernel(*refs, cap, ne, has_next, aliased):
    refs = list(refs)
    slot_ref, y_ref, x_ref, gpost_ref, mod_ref = refs[:5]
    refs = refs[5:]
    if has_next:
        gpre_ref, modn_ref = refs[:2]
        refs = refs[2:]
    if aliased:
        refs = refs[2 if has_next else 1:]
    xo_ref = refs[0]
    hn_ref = refs[1] if has_next else None
    acc = refs[-1]
    eg = pl.program_id(2)
    ng = pl.num_programs(2)
    rows = slot_ref.shape[0]
    kk = ne * cap
    er = lax.broadcasted_iota(jnp.int32, (128, kk), 0)
    ec = lax.broadcasted_iota(jnp.int32, (128, kk), 1)
    expand = jnp.where(er == eg * ne + ec // cap, 1.0, 0.0).astype(BF16)
    spread = _dot(slot_ref[...].astype(BF16), expand)
    want = (lax.broadcasted_iota(jnp.int32, (rows, kk), 1) % cap).astype(F32)
    pt = jnp.where(spread == want, 1.0, 0.0).astype(BF16)

    @pl.when(eg == 0)
    def _():
        acc[...] = jnp.zeros_like(acc)

    acc[...] += _dot(pt, y_ref[...].reshape(kk, D))

    @pl.when(eg == ng - 1)
    def _():
        x2 = x_ref[...] + mod_ref[5:6, :] * (_rms(acc[...]) * gpost_ref[...])
        xo_ref[...] = x2
        if has_next:
            hn = (_rms(x2) * gpre_ref[...]) * (1.0 + modn_ref[1:2, :]) + modn_ref[0:1, :]
            hn_ref[...] = hn.astype(BF16)


def _scatter_call(slot_t, y, x_mid, g_post, modtab, g_pre_next, modtab_next, *, rows, row0_blk, cap,
                  ne, seg, out_rows, prev=None):
    bsz = x_mid.shape[0]
    has_next = g_pre_next is not None
    n_tiles = (T // rows) if seg == 0 else 1
    ng = N_EXPERTS // ne
    vec = pl.BlockSpec((1, D), lambda b, r, g: (0, 0))
    modspec = pl.BlockSpec((None, None, 8, D), lambda b, r, g: (b, seg, 0, 0))
    rowmap = lambda b, r, g: (b, row0_blk + r, 0)
    in_specs = [pl.BlockSpec((None, rows, 128), rowmap),
                pl.BlockSpec((ne, cap, D), lambda b, r, g: (g, b, 0)),
                pl.BlockSpec((None, rows, D), rowmap),
                vec, modspec]
    args = [slot_t, y, x_mid, g_post.reshape(1, D), modtab]
    if has_next:
        in_specs += [vec, modspec]
        args += [g_pre_next.reshape(1, D), modtab_next]
    aliases = {}
    if prev is not None:
        for k, arr in enumerate(prev):
            aliases[len(args)] = k
            in_specs.append(pl.BlockSpec(memory_space=pl.ANY))
            args.append(arr)
    out_specs = [pl.BlockSpec((None, rows, D), rowmap)]
    out_shape = [jax.ShapeDtypeStruct((bsz, out_rows, D), F32)]
    if has_next:
        out_specs.append(pl.BlockSpec((None, rows, D), rowmap))
        out_shape.append(jax.ShapeDtypeStruct((bsz, out_rows, D), BF16))
    return pl.pallas_call(
        functools.partial(_scatter_kernel, cap=cap, ne=ne, has_next=has_next, aliased=prev is not None),
        grid=(bsz, n_tiles, ng),
        in_specs=in_specs, out_specs=out_specs, out_shape=out_shape,
        scratch_shapes=[pltpu.VMEM((rows, D), F32)],
        input_output_aliases=aliases,
        compiler_params=_cp(3), name="scatter_lat" if seg == 0 else "scatter_ctx",
    )(*args)


def _pack_w_in(w):
    qa, ka, va, kb, vb, dec, qb, gb, uc, vc = jnp.split(
        w, [int(s) for s in np.cumsum((WA, WA, WA, KB, VB, 2 * DEC_RANK, KB, VB, WC))], axis=-1)
    pad = jnp.zeros((D, 128 - 2 * DEC_RANK), w.dtype)
    return jnp.concatenate([vb, gb, qa, ka, va, kb, qb, uc, vc, dec, pad], axis=-1).astype(BF16)


def kernel(x, c, ctx, c_ctx, w_ada, b_ada, g_pre_mix, g_post_mix, g_pre_ffn, g_post_ffn, w_in, w_dec,
           b_dec, rpb, g_gla, ln_v_g, ln_v_b, w_sp, b_sp, w_out, w_router, w_gate, w_up, w_down):
    bsz = x.shape[0]
    xall = jnp.concatenate([x, ctx], axis=1)
    c_all = jnp.zeros((16, D), F32).at[:bsz].set(c).at[8].set(c_ctx)
    mods = _ada_call(c_all, w_ada, b_ada).reshape(DEPTH, 16, 6, D)
    lat = mods[:, :bsz]
    cx = jnp.broadcast_to(mods[:, 8:9], lat.shape)
    modtab = jnp.pad(jnp.stack([lat, cx], axis=2), ((0, 0), (0, 0), (0, 0), (0, 2), (0, 0)))

    cos_t, sa_t, sb_t = _rope_tables()
    mstack = jnp.asarray(_gla_matrices(), BF16)
    hmask = jnp.asarray(np.stack([(np.arange(128) < DH_A), (np.arange(128) >= DH_A)]) * DH_A ** -0.5, F32)
    upper = (lax.broadcasted_iota(jnp.int32, (T, T), 0) < lax.broadcasted_iota(jnp.int32, (T, T), 1)).astype(BF16)

    h = _prenorm_call(xall, g_pre_mix[0], modtab[0])
    for l in range(DEPTH):
        last = l == DEPTH - 1
        nb = NLAT if last else NBLK
        P = _proj_call(h.reshape(bsz * TT, D), _pack_w_in(w_in[l])).reshape(bsz, TT, NP)
        o_a = _na_call(P, _na_bias(rpb[l]), hmask, nb)
        wd_pad = jnp.zeros((2, 128, KB), F32)
        wd_pad = wd_pad.at[0, :DEC_RANK].set(w_dec[l, 0]).at[1, DEC_RANK:2 * DEC_RANK].set(w_dec[l, 1])
        o_g = _gla_call(P, wd_pad, b_dec[l].reshape(2, 1, KB), cos_t, sa_t, sb_t, mstack)
        bs_rep = jnp.broadcast_to(b_sp[l][:, :, None], (GC, MIX_CHUNK, 128))
        o_c = _gmlp_call(P, ln_v_g[l], ln_v_b[l], w_sp[l], bs_rep, nb)
        wr_pad = jnp.pad(w_router[l], ((0, 0), (0, 128 - N_EXPERTS)))
        x_mid, h2, lg = _out_call(o_a, o_g, P, o_c, w_out[l].astype(BF16), xall, g_gla[l], g_post_mix[l],
                                  g_pre_ffn[l], modtab[l], wr_pad, nb)
        routed = _route_call(lg, upper, not last)
        slot_l, aff_l, slot_t = routed[:3]
        xs_l, gt_l = _gather_call(slot_l, aff_l, h2, T, CAP_LAT, 0)
        if last:
            (y_l,) = _ffn_call(l, xs_l, gt_l, None, None, w_gate, w_up, w_down)
            (xall,) = _scatter_call(slot_t, y_l, x_mid, g_post_ffn[l], modtab[l], None, None, rows=512,
                                    row0_blk=0, cap=CAP_LAT, ne=4, seg=0, out_rows=T)
        else:
            slot_c, aff_c = routed[3:]
            xs_c, gt_c = _gather_call(slot_c, aff_c, h2, L, CAP_CTX, NLAT)
            y_l, y_c = _ffn_call(l, xs_l, gt_l, xs_c, gt_c, w_gate, w_up, w_down)
            part = _scatter_call(slot_t, y_l, x_mid, g_post_ffn[l], modtab[l], g_pre_mix[l + 1],
                                 modtab[l + 1], rows=512, row0_blk=0, cap=CAP_LAT, ne=4, seg=0, out_rows=TT)
            xall, h = _scatter_call(slot_t, y_c, x_mid, g_post_ffn[l], modtab[l], g_pre_mix[l + 1],
                                    modtab[l + 1], rows=L, row0_blk=NLAT, cap=CAP_CTX, ne=N_EXPERTS, seg=1,
                                    out_rows=TT, prev=part)
    return xall
```

```python
import functools

import numpy as np
import jax
import jax.numpy as jnp
from jax import lax
from jax.experimental import pallas as pl
from jax.experimental.pallas import tpu as pltpu

F32 = jnp.float32
BF16 = jnp.bfloat16

D = 2048
T = 2048
L = 256
TT = T + L
BLK = 256
NBLK = TT // BLK
NLAT = T // BLK
DEPTH = 2
GRID_W = 64
ROWS = T // GRID_W
HA, DH_A = 8, 64
WIN_ROWS, WIN_COLS = 8, 16
HB, DK_B, DV_B = 4, 128, 256
DEC_RANK = 16
GATE_TAU = 16.0
GC, CG, MIX_CHUNK = 4, 128, 128
N_EXPERTS = 16
EXPERT_FF = 2048
CAP_FACTOR = 2
ROPE_BASE = 10000.0
EPS = 1e-6
LOG2E = 1.4426950408889634
WA, KB, VB, WC = HA * DH_A, HB * DK_B, HB * DV_B, GC * CG

NP = 2 * VB + 7 * 512 + 128
C_VB, C_GB, C_QA, C_KA, C_VA, C_KB, C_QB, C_UC, C_VC, C_DEC = (
    0, 1024, 2048, 2560, 3072, 3584, 4096, 4608, 5120, 5632)

NA_QROWS = 4
NA_KROWS = 12
NA_KEYS = NA_KROWS * GRID_W
NEG = -1e30

GLA_LEVELS = 8
FF_TILE = 256
CAP_LAT = CAP_FACTOR * T // N_EXPERTS
CAP_CTX = CAP_FACTOR * L // N_EXPERTS
VMEM_LIMIT = 56 * 1024 * 1024


def _cp(n_axes):
    return pltpu.CompilerParams(dimension_semantics=("arbitrary",) * n_axes,
                                vmem_limit_bytes=VMEM_LIMIT)


def _dot(a, b):
    return jnp.dot(a, b, preferred_element_type=F32)


def _dot_nt(a, b):
    return lax.dot_general(a, b, (((1,), (1,)), ((), ())), preferred_element_type=F32)


def _dot_tn(a, b):
    return lax.dot_general(a, b, (((0,), (0,)), ((), ())), preferred_element_type=F32)


def _rms(x):
    return x * lax.rsqrt(jnp.mean(x * x, axis=-1, keepdims=True) + EPS)


def _sigmoid(x):
    return 1.0 / (1.0 + jnp.exp(-x))


def _split_bf16(x):
    hi = x.astype(BF16)
    lo = (x - hi.astype(F32)).astype(BF16)
    return hi, lo


def _ada_kernel(c_ref, w_ref, b_ref, o_ref):
    cv = c_ref[...]
    s = cv * _sigmoid(cv)
    o_ref[...] = jnp.dot(s, w_ref[...], preferred_element_type=F32,
                         precision=lax.Precision.HIGHEST) + b_ref[...]


def _ada_call(c_all, w_ada, b_ada):
    tn = 1024
    n6 = w_ada.shape[-1]
    return pl.pallas_call(
        _ada_kernel,
        grid=(DEPTH, n6 // tn),
        in_specs=[pl.BlockSpec((16, D), lambda l, n: (0, 0)),
                  pl.BlockSpec((None, D, tn), lambda l, n: (l, 0, n)),
                  pl.BlockSpec((None, 1, tn), lambda l, n: (l, 0, n))],
        out_specs=pl.BlockSpec((None, 16, tn), lambda l, n: (l, 0, n)),
        out_shape=jax.ShapeDtypeStruct((DEPTH, 16, n6), F32),
        compiler_params=_cp(2), name="ada",
    )(c_all, w_ada, b_ada.reshape(DEPTH, 1, n6))


def _prenorm_kernel(x_ref, g_ref, mod_ref, h_ref):
    y = _rms(x_ref[...]) * g_ref[...]
    h_ref[...] = (y * (1.0 + mod_ref[1:2, :]) + mod_ref[0:1, :]).astype(BF16)


def _prenorm_call(xall, g, modtab):
    bsz = xall.shape[0]
    return pl.pallas_call(
        _prenorm_kernel,
        grid=(bsz, NBLK),
        in_specs=[pl.BlockSpec((None, BLK, D), lambda b, j: (b, j, 0)),
                  pl.BlockSpec((1, D), lambda b, j: (0, 0)),
                  pl.BlockSpec((None, None, 8, D), lambda b, j: (b, j // NLAT, 0, 0))],
        out_specs=pl.BlockSpec((None, BLK, D), lambda b, j: (b, j, 0)),
        out_shape=jax.ShapeDtypeStruct((bsz, TT, D), BF16),
        compiler_params=_cp(2), name="prenorm",
    )(xall, g.reshape(1, D), modtab)


def _proj_kernel(h_ref, w_ref, o_ref):
    o_ref[...] = _dot(h_ref[...], w_ref[...]).astype(BF16)


def _proj_call(h2d, w_pack):
    m = h2d.shape[0]
    tm = 1024 if m % 1024 == 0 else 768
    tn = NP // 5
    return pl.pallas_call(
        _proj_kernel,
        grid=(NP // tn, m // tm),
        in_specs=[pl.BlockSpec((tm, D), lambda n, i: (i, 0)),
                  pl.BlockSpec((D, tn), lambda n, i: (0, n))],
        out_specs=pl.BlockSpec((tm, tn), lambda n, i: (i, n)),
        out_shape=jax.ShapeDtypeStruct((m, NP), BF16),
        compiler_params=_cp(2), name="proj_in",
    )(h2d, w_pack)


def _softmax_pv(s_list, v_list):
    m = s_list[0].max(axis=-1, keepdims=True)
    for s in s_list[1:]:
        m = jnp.maximum(m, s.max(axis=-1, keepdims=True))
    acc = None
    den = None
    for s, v in zip(s_list, v_list):
        p = jnp.exp(s - m)
        d = p.sum(axis=-1, keepdims=True)
        o = _dot(p.astype(BF16), v)
        acc = o if acc is None else acc + o
        den = d if den is None else den + d
    return acc / den


def _na_kernel(q_ref, k_ref, v_ref, bias_ref, hm_ref, o_ref):
    j = pl.program_id(1)
    lane = lax.broadcasted_iota(jnp.int32, (BLK, 128), 1)
    low = lane < DH_A

    def run(local_start):
        for p in range(HA // 2):
            sl = slice(128 * p, 128 * p + 128)
            q2 = q_ref[:, sl]
            kc = k_ref[T:TT, sl]
            vc = v_ref[T:TT, sl]
            if local_start is not None:
                kl = k_ref[pl.ds(local_start, NA_KEYS), sl]
                vl = v_ref[pl.ds(local_start, NA_KEYS), sl]
            pair = []
            for hh in range(2):
                qm = (q2.astype(F32) * hm_ref[hh:hh + 1, :]).astype(BF16)
                s_ctx = _dot_nt(qm, kc)
                if local_start is not None:
                    s_loc = _dot_nt(qm, kl) + bias_ref[2 * p + hh]
                    pair.append(_softmax_pv([s_loc, s_ctx], [vl, vc]))
                else:
                    pair.append(_softmax_pv([s_ctx], [vc]))
            o_ref[:, sl] = jnp.where(low, pair[0], pair[1]).astype(BF16)

    @pl.when(j < NLAT)
    def _():
        krow = jnp.clip(j * NA_QROWS - WIN_ROWS // 2, 0, ROWS - NA_KROWS)
        run(pl.multiple_of(krow * GRID_W, GRID_W))

    @pl.when(j == NLAT)
    def _():
        run(None)


def _na_call(P, bias, hmask, n_blocks):
    bsz = P.shape[0]

    def bias_idx(b, j):
        return (jnp.where(j == 0, 0, jnp.where(j == NLAT - 1, 2, 1)), 0, 0, 0)

    return pl.pallas_call(
        _na_kernel,
        grid=(bsz, n_blocks),
        in_specs=[pl.BlockSpec((None, BLK, WA), lambda b, j: (b, j, C_QA // WA)),
                  pl.BlockSpec((None, TT, WA), lambda b, j: (b, 0, C_KA // WA)),
                  pl.BlockSpec((None, TT, WA), lambda b, j: (b, 0, C_VA // WA)),
                  pl.BlockSpec((None, HA, BLK, NA_KEYS), bias_idx),
                  pl.BlockSpec((2, 128), lambda b, j: (0, 0))],
        out_specs=pl.BlockSpec((None, BLK, WA), lambda b, j: (b, j, 0)),
        out_shape=jax.ShapeDtypeStruct((bsz, n_blocks * BLK, WA), BF16),
        compiler_params=_cp(2), name="nbr_attn",
    )(P, P, P, bias, hmask)


def _na_bias(rpb_l):
    n_dr, n_dc = 2 * WIN_ROWS - 1, 2 * WIN_COLS - 1
    cq = np.arange(GRID_W)[:, None]
    ck = np.arange(GRID_W)[None, :]
    cs = np.clip(cq - WIN_COLS // 2, 0, GRID_W - WIN_COLS)
    col_ok = (ck >= cs) & (ck < cs + WIN_COLS)
    dc = np.clip(ck - cq + WIN_COLS - 1, 0, n_dc - 1)
    col_sel = (dc[:, :, None] == np.arange(n_dc)).astype(np.float32)
    qi = np.arange(NA_QROWS)[:, None]
    kj = np.arange(NA_KROWS)[None, :]
    row_sel, row_ok = [], []
    for rb in (0, 1, NLAT - 1):
        r = rb * NA_QROWS + qi
        kr = int(np.clip(rb * NA_QROWS - WIN_ROWS // 2, 0, ROWS - NA_KROWS)) + kj
        rs = np.clip(r - WIN_ROWS // 2, 0, ROWS - WIN_ROWS)
        ok = (kr >= rs) & (kr < rs + WIN_ROWS)
        dr = kr - r + WIN_ROWS - 1
        row_sel.append(((dr[:, :, None] == np.arange(n_dr)) & ok[:, :, None]).astype(np.float32))
        row_ok.append(ok)
    row_sel, row_ok = np.stack(row_sel), np.stack(row_ok)
    colx = jnp.einsum("qke,hde->hdqk", col_sel, rpb_l, precision=lax.Precision.HIGHEST)
    tab = jnp.einsum("tikd,hdqc->thiqkc", row_sel, colx, precision=lax.Precision.HIGHEST)
    valid = row_ok[:, None, :, None, :, None] & col_ok[None, None, None, :, None, :]
    tab = jnp.where(valid, tab, NEG)
    return tab.reshape(3, HA, BLK, NA_KEYS).astype(F32)


def _gla_matrices():
    n = BLK
    i = np.arange(n)[:, None]
    t = np.arange(n)[None, :]
    out = np.zeros((2, (GLA_LEVELS + 2) * n, n), np.float32)
    for lvl in range(GLA_LEVELS):
        s = n >> (lvl + 1)
        blk0 = (i // (2 * s)) * (2 * s)
        m = blk0 + s - 1
        fwd = np.where(i > m, (t > m) & (t <= i), (t > i) & (t <= m))
        bwd = np.where(i <= m, (t >= i) & (t <= m), (t > m) & (t < i))
        out[0, lvl * n:(lvl + 1) * n] = fwd
        out[1, lvl * n:(lvl + 1) * n] = bwd
    out[0, GLA_LEVELS * n:(GLA_LEVELS + 1) * n] = t <= i
    out[1, GLA_LEVELS * n:(GLA_LEVELS + 1) * n] = t >= i
    out[0, (GLA_LEVELS + 1) * n:] = t > i
    out[1, (GLA_LEVELS + 1) * n:] = t < i
    return out


def _gla_level_ids():
    h = BLK // 2
    i = np.arange(h)[:, None]
    j = np.arange(h)[None, :]
    x = np.maximum(i ^ j, 1)
    lvl = GLA_LEVELS - 1 - np.floor(np.log2(x)).astype(np.int32)
    fwd = np.where(i == j, -1, np.where(i > j, lvl, -2))
    bwd = np.where(i == j, -1, np.where(i < j, lvl, -2))
    return np.stack([fwd, bwd]).astype(np.int32)


def _gla_body(rev, kb_ref, qb_ref, vb_ref, dec_ref, wd_ref, bd_ref, cos_ref, sa_ref, sb_ref,
              m_ref, lev_ref, o_ref, st_ref):
    half = BLK // 2
    z = _dot(dec_ref[...], wd_ref[...].astype(BF16)) + bd_ref[...]
    la = (jnp.minimum(z, 0.0) - jnp.log1p(jnp.exp(-jnp.abs(z)))) * (LOG2E / GATE_TAU)
    la_b = la.astype(BF16)

    def seg_exp(lvl):
        return jnp.exp2(_dot(m_ref[lvl * BLK:(lvl + 1) * BLK, :], la_b))

    row = lax.broadcasted_iota(jnp.int32, (BLK, DK_B), 0)
    lev = lev_ref[...]
    on_diag = lev == -1
    at_level = {lvl: lev == lvl for lvl in range(1, GLA_LEVELS)}
    cosv, sav, sbv = cos_ref[...], sa_ref[...], sb_ref[...]

    def rope(x):
        return x * cosv + pltpu.roll(x, DK_B - 32, 1) * sav + pltpu.roll(x, 32, 1) * sbv

    e_in = seg_exp(GLA_LEVELS)
    e_out = seg_exp(GLA_LEVELS + 1)
    e_lvls = [seg_exp(lvl) for lvl in range(GLA_LEVELS)]
    q_rows = slice(0, half) if rev else slice(half, BLK)
    k_rows = slice(half, BLK) if rev else slice(0, half)

    outs = []
    for hh in range(2):
        ks = slice(DK_B * hh, DK_B * (hh + 1))
        q = rope(qb_ref[:, ks].astype(F32)) * (DK_B ** -0.5)
        k = rope(kb_ref[:, ks].astype(F32))
        v = vb_ref[:, DV_B * hh:DV_B * (hh + 1)]
        self_w = jnp.sum(q * k, axis=-1, keepdims=True)
        e = e_lvls[0][:, ks]
        cross = _dot_nt((q[q_rows] * e[q_rows]).astype(BF16), (k[k_rows] * e[k_rows]).astype(BF16))
        diag = [jnp.where(on_diag, self_w[c * half:(c + 1) * half], 0.0) for c in range(2)]
        for lvl in range(1, GLA_LEVELS):
            s = BLK >> (lvl + 1)
            e = e_lvls[lvl][:, ks]
            if s >= 8:
                parts = []
                for i in range(BLK // s):
                    rs = slice(i * s, (i + 1) * s)
                    src = q if ((i % 2 == 1) != rev) else k
                    parts.append(src[rs] * e[rs])
                x = jnp.concatenate(parts, axis=0)
            else:
                second = ((row // s) & 1) == 1
                x = (jnp.where(second, k, q) if rev else jnp.where(second, q, k)) * e
            xb = x.astype(BF16)
            for c in range(2):
                xc = xb[c * half:(c + 1) * half]
                diag[c] = jnp.where(at_level[lvl], _dot_nt(xc, xc), diag[c])
        zero = jnp.zeros((half, half), F32)
        if rev:
            att = jnp.concatenate([jnp.concatenate([diag[0], cross], axis=1),
                                   jnp.concatenate([zero, diag[1]], axis=1)], axis=0)
        else:
            att = jnp.concatenate([jnp.concatenate([diag[0], zero], axis=1),
                                   jnp.concatenate([cross, diag[1]], axis=1)], axis=0)
        e_q = e_in[:, ks]
        st = st_ref[hh]
        o = _dot_nt((q * e_q).astype(BF16), st.astype(BF16)) + _dot(att.astype(BF16), v)
        outs.append(o.astype(BF16))
        carry = e_q[0:1, :] if rev else e_q[BLK - 1:BLK, :]
        kt = (k * e_out[:, ks]).astype(BF16)
        st_ref[hh] = st * carry + _dot_tn(v, kt)
    o_ref[...] = jnp.concatenate(outs, axis=-1)


def _gla_kernel(*refs):
    st_ref = refs[-1]
    d = pl.program_id(2)

    @pl.when(pl.program_id(3) == 0)
    def _():
        st_ref[...] = jnp.zeros_like(st_ref)

    @pl.when(d == 0)
    def _():
        _gla_body(False, *refs)

    @pl.when(d == 1)
    def _():
        _gla_body(True, *refs)


def _gla_call(P, wd_pad, bd, cos_t, sa_t, sb_t, mstack, lev_ids):
    bsz = P.shape[0]

    def blk(dd, s):
        return jnp.where(s == 0, NLAT, jnp.where(dd == 0, s - 1, NLAT - s))

    return pl.pallas_call(
        _gla_kernel,
        grid=(bsz, 2, 2, NBLK),
        in_specs=[
            pl.BlockSpec((None, BLK, 2 * DK_B), lambda b, hp, dd, s: (b, blk(dd, s), C_KB // 256 + hp)),
            pl.BlockSpec((None, BLK, 2 * DK_B), lambda b, hp, dd, s: (b, blk(dd, s), C_QB // 256 + hp)),
            pl.BlockSpec((None, BLK, 2 * DV_B), lambda b, hp, dd, s: (b, blk(dd, s), C_VB // 512 + hp)),
            pl.BlockSpec((None, BLK, 128), lambda b, hp, dd, s: (b, blk(dd, s), C_DEC // 128)),
            pl.BlockSpec((None, 128, 2 * DK_B), lambda b, hp, dd, s: (dd, 0, hp)),
            pl.BlockSpec((None, 1, 2 * DK_B), lambda b, hp, dd, s: (dd, 0, hp)),
            pl.BlockSpec((BLK, DK_B), lambda b, hp, dd, s: (blk(dd, s), 0)),
            pl.BlockSpec((BLK, DK_B), lambda b, hp, dd, s: (blk(dd, s), 0)),
            pl.BlockSpec((BLK, DK_B), lambda b, hp, dd, s: (blk(dd, s), 0)),
            pl.BlockSpec((None, (GLA_LEVELS + 2) * BLK, BLK), lambda b, hp, dd, s: (dd, 0, 0)),
            pl.BlockSpec((None, BLK // 2, BLK // 2), lambda b, hp, dd, s: (dd, 0, 0)),
        ],
        out_specs=pl.BlockSpec((None, None, BLK, 2 * DV_B),
                               lambda b, hp, dd, s: (dd, b, blk(dd, s), hp)),
        out_shape=jax.ShapeDtypeStruct((2, bsz, TT, VB), BF16),
        scratch_shapes=[pltpu.VMEM((2, DV_B, DK_B), F32)],
        compiler_params=_cp(4), name="gla",
    )(P, P, P, P, wd_pad, bd, cos_t, sa_t, sb_t, mstack, lev_ids)


def _rope_tables():
    t = jnp.arange(T)
    half = DK_B // 2
    inv = ROPE_BASE ** (-jnp.arange(0, half, 2, dtype=F32) / half)

    def tab(pos):
        ang = pos.astype(F32)[:, None] * inv[None, :]
        return jnp.concatenate([ang, ang], axis=-1)

    ang = jnp.concatenate([tab(t // GRID_W), tab(t % GRID_W)], axis=-1)
    cos, sin = jnp.cos(ang), jnp.sin(ang)
    first = (np.arange(DK_B) % half) < (half // 2)
    sa = jnp.where(first[None, :], -sin, 0.0)
    sb = jnp.where(first[None, :], 0.0, sin)
    ident = jnp.ones((L, DK_B), F32)
    zero = jnp.zeros((L, DK_B), F32)
    return (jnp.concatenate([cos, ident]), jnp.concatenate([sa, zero]), jnp.concatenate([sb, zero]))


def _gelu(x):
    return 0.5 * x * (1.0 + jnp.tanh(0.7978845608028654 * (x + 0.044715 * (x * x * x))))


def _gmlp_kernel(u_ref, v_ref, g_ref, b_ref, w_ref, bs_ref, o_ref):
    for ch in range(BLK // MIX_CHUNK):
        rs = slice(ch * MIX_CHUNK, (ch + 1) * MIX_CHUNK)
        u = _gelu(u_ref[rs, :].astype(F32))
        v = _gelu(v_ref[rs, :].astype(F32))
        for g in range(GC):
            cs = slice(g * CG, (g + 1) * CG)
            vg = v[:, cs]
            mu = jnp.mean(vg, axis=-1, keepdims=True)
            var = jnp.mean(jnp.square(vg - mu), axis=-1, keepdims=True)
            vn = (vg - mu) * lax.rsqrt(var + EPS) * g_ref[:, cs] + b_ref[:, cs]
            s = _dot(w_ref[g].astype(BF16), vn.astype(BF16)) + bs_ref[g]
            o_ref[rs, cs] = (u[:, cs] * s).astype(BF16)


def _gmlp_call(P, ln_g, ln_b, w_sp, bs_rep, n_blocks):
    bsz = P.shape[0]
    return pl.pallas_call(
        _gmlp_kernel,
        grid=(bsz, n_blocks),
        in_specs=[pl.BlockSpec((None, BLK, WC), lambda b, j: (b, j, C_UC // WC)),
                  pl.BlockSpec((None, BLK, WC), lambda b, j: (b, j, C_VC // WC)),
                  pl.BlockSpec((1, WC), lambda b, j: (0, 0)),
                  pl.BlockSpec((1, WC), lambda b, j: (0, 0)),
                  pl.BlockSpec((GC, MIX_CHUNK, MIX_CHUNK), lambda b, j: (0, 0, 0)),
                  pl.BlockSpec((GC, MIX_CHUNK, 128), lambda b, j: (0, 0, 0))],
        out_specs=pl.BlockSpec((None, BLK, WC), lambda b, j: (b, j, 0)),
        out_shape=jax.ShapeDtypeStruct((bsz, n_blocks * BLK, WC), BF16),
        compiler_params=_cp(2), name="gmlp",
    )(P, P, ln_g.reshape(1, WC), ln_b.reshape(1, WC), w_sp, bs_rep)


def _out_kernel(oa_ref, of_ref, ob_ref, gb_ref, oc_ref, w_ref, x_ref, gg_ref, gpost_ref, gpre_ref,
                mod_ref, wr_ref, xo_ref, h2_ref, lg_ref):
    o = of_ref[...].astype(F32) + ob_ref[...].astype(F32)
    gb = gb_ref[...].astype(F32)
    parts = []
    for h in range(HB):
        cs = slice(h * DV_B, (h + 1) * DV_B)
        gh = gb[:, cs]
        parts.append((_rms(o[:, cs]) * gg_ref[:, cs] * (gh * _sigmoid(gh))).astype(BF16))
    gl = jnp.concatenate(parts, axis=-1)
    y = (_dot(oa_ref[...], w_ref[0:WA, :]) + _dot(gl, w_ref[WA:WA + VB, :])
         + _dot(oc_ref[...], w_ref[WA + VB:, :]))
    x1 = x_ref[...] + mod_ref[2:3, :] * (_rms(y) * gpost_ref[...])
    xo_ref[...] = x1
    h2 = (_rms(x1) * gpre_ref[...]) * (1.0 + mod_ref[4:5, :]) + mod_ref[3:4, :]
    h2_ref[...] = h2.astype(BF16)
    hh, hl = _split_bf16(h2)
    wh, wl = _split_bf16(wr_ref[...])
    lg_ref[...] = _dot(hh, wh) + _dot(hl, wh) + _dot(hh, wl)


def _out_call(o_a, o_g, P, o_c, w_out_b, xall, g_gla, g_post, g_pre, modtab, wr_pad, n_blocks):
    bsz = P.shape[0]
    rows = n_blocks * BLK
    rowspec = lambda w: pl.BlockSpec((None, BLK, w), lambda b, j: (b, j, 0))
    vec = lambda w: pl.BlockSpec((1, w), lambda b, j: (0, 0))
    return pl.pallas_call(
        _out_kernel,
        grid=(bsz, n_blocks),
        in_specs=[rowspec(WA),
                  pl.BlockSpec((None, None, BLK, VB), lambda b, j: (0, b, j, 0)),
                  pl.BlockSpec((None, None, BLK, VB), lambda b, j: (1, b, j, 0)),
                  pl.BlockSpec((None, BLK, VB), lambda b, j: (b, j, C_GB // VB)),
                  rowspec(WC),
                  pl.BlockSpec((D, D), lambda b, j: (0, 0)),
                  rowspec(D),
                  vec(VB), vec(D), vec(D),
                  pl.BlockSpec((None, None, 8, D), lambda b, j: (b, j // NLAT, 0, 0)),
                  pl.BlockSpec((D, 128), lambda b, j: (0, 0))],
        out_specs=[rowspec(D), rowspec(D), rowspec(128)],
        out_shape=[jax.ShapeDtypeStruct((bsz, rows, D), F32),
                   jax.ShapeDtypeStruct((bsz, rows, D), BF16),
                   jax.ShapeDtypeStruct((bsz, rows, 128), F32)],
        compiler_params=_cp(2), name="proj_out",
    )(o_a, o_g, o_g, P, o_c, w_out_b, xall, g_gla.reshape(1, VB), g_post.reshape(1, D),
      g_pre.reshape(1, D), modtab, wr_pad)


def _route_set(lg, cap, upper):
    n = lg.shape[0]
    lt = lg.T[:N_EXPERTS, :]
    ex = jnp.exp(lt - lt.max(axis=0, keepdims=True))
    aff = ex / ex.sum(axis=0, keepdims=True)
    bits = pltpu.bitcast(aff, jnp.int32)
    capf = jnp.float32(cap)

    def body(i, prefix):
        cand = prefix | jnp.left_shift(jnp.int32(1), 30 - i)
        cnt = jnp.sum(jnp.where(bits >= cand, 1.0, 0.0), axis=1, keepdims=True)
        return jnp.where(cnt >= capf, cand, prefix)

    thr = lax.fori_loop(0, 31, body, jnp.zeros((N_EXPERTS, 1), jnp.int32))
    gt = jnp.where(bits > thr, 1.0, 0.0)
    eq = jnp.where(bits == thr, 1.0, 0.0)
    need = capf - gt.sum(axis=1, keepdims=True)
    rank_eq = _dot(eq.astype(BF16), upper)
    sel = gt + eq * jnp.where(rank_eq < need, 1.0, 0.0)
    slot = _dot(sel.astype(BF16), upper)
    slot = jnp.where(sel > 0.5, slot, -1.0)
    pad = jnp.full((128 - N_EXPERTS, n), -1.0, F32)
    slot_t = jnp.concatenate([slot, pad], axis=0).T
    return slot, aff, slot_t


def _route_kernel(lg_ref, u_ref, *out_refs, with_ctx):
    sl, af, st = _route_set(lg_ref[0:T, :], CAP_LAT, u_ref[...])
    out_refs[0][...] = sl
    out_refs[1][...] = af
    out_refs[2][0:T, :] = st
    if with_ctx:
        sl, af, st = _route_set(lg_ref[T:TT, :], CAP_CTX, u_ref[0:L, 0:L])
        out_refs[3][...] = sl
        out_refs[4][...] = af
        out_refs[2][T:TT, :] = st


def _route_call(lg, upper, with_ctx):
    bsz, rows, _ = lg.shape
    en = lambda n: pl.BlockSpec((None, N_EXPERTS, n), lambda b: (b, 0, 0))
    out_specs = [en(T), en(T), pl.BlockSpec((None, rows, 128), lambda b: (b, 0, 0))]
    out_shape = [jax.ShapeDtypeStruct((bsz, N_EXPERTS, T), F32),
                 jax.ShapeDtypeStruct((bsz, N_EXPERTS, T), F32),
                 jax.ShapeDtypeStruct((bsz, rows, 128), F32)]
    if with_ctx:
        out_specs += [en(L), en(L)]
        out_shape += [jax.ShapeDtypeStruct((bsz, N_EXPERTS, L), F32)] * 2
    return pl.pallas_call(
        functools.partial(_route_kernel, with_ctx=with_ctx),
        grid=(bsz,),
        in_specs=[pl.BlockSpec((None, rows, 128), lambda b: (b, 0, 0)),
                  pl.BlockSpec((T, T), lambda b: (0, 0))],
        out_specs=out_specs, out_shape=out_shape,
        compiler_params=_cp(1), name="route",
    )(lg, upper)


def _gather_kernel(slot_ref, aff_ref, h_ref, xs_ref, g_ref, *, cap):
    e = pl.program_id(1)
    n = h_ref.shape[0]
    srow = slot_ref[pl.ds(e, 1), :]
    arow = aff_ref[pl.ds(e, 1), :]
    sid = lax.broadcasted_iota(jnp.int32, (cap, n), 0).astype(F32)
    hit = sid == srow
    xs_ref[...] = _dot(jnp.where(hit, 1.0, 0.0).astype(BF16), h_ref[...]).astype(BF16)
    g = jnp.sum(jnp.where(hit, arow, 0.0), axis=1, keepdims=True)
    g_ref[...] = jnp.broadcast_to(g, (cap, 128))


def _gather_call(slot, aff, h2, n, cap, blk_idx):
    bsz = h2.shape[0]
    return pl.pallas_call(
        functools.partial(_gather_kernel, cap=cap),
        grid=(bsz, N_EXPERTS),
        in_specs=[pl.BlockSpec((None, N_EXPERTS, n), lambda b, e: (b, 0, 0)),
                  pl.BlockSpec((None, N_EXPERTS, n), lambda b, e: (b, 0, 0)),
                  pl.BlockSpec((None, n, D), lambda b, e: (b, blk_idx, 0))],
        out_specs=[pl.BlockSpec((None, cap, D), lambda b, e: (e, b, 0)),
                   pl.BlockSpec((None, cap, 128), lambda b, e: (e, b, 0))],
        out_shape=[jax.ShapeDtypeStruct((N_EXPERTS, bsz * cap, D), BF16),
                   jax.ShapeDtypeStruct((N_EXPERTS, bsz * cap, 128), F32)],
        compiler_params=_cp(2), name="gather",
    )(slot, aff, h2)


def _ffn_kernel(*refs, rl, rc, nf):
    if rc:
        xl_ref, xc_ref, gl_ref, gc_ref, wg_ref, wu_ref, wd_ref, yl_ref, yc_ref, hid = refs
        groups = ((xl_ref, gl_ref, yl_ref, 0, rl), (xc_ref, gc_ref, yc_ref, rl, rc))
    else:
        xl_ref, gl_ref, wg_ref, wu_ref, wd_ref, yl_ref, hid = refs
        groups = ((xl_ref, gl_ref, yl_ref, 0, rl),)
    s = pl.program_id(1)

    @pl.when(s < nf)
    def _():
        wg = wg_ref[...].astype(BF16)
        wu = wu_ref[...].astype(BF16)
        for x_ref, _, _, r0, nr in groups:
            xv = x_ref[...]
            a = _dot(xv, wg)
            u = _dot(xv, wu)
            hid[s, r0:r0 + nr, :] = ((a * _sigmoid(a)) * u).astype(BF16)

    @pl.when(s >= nf)
    def _():
        wd = wd_ref[...].astype(BF16)
        for _, g_ref, y_ref, r0, nr in groups:
            hm = jnp.concatenate([hid[k, r0:r0 + nr, :] for k in range(nf)], axis=1)
            y = _dot(hm, wd)
            for c in range(FF_TILE // 128):
                cs = slice(128 * c, 128 * (c + 1))
                y_ref[:, cs] = (y[:, cs] * g_ref[...]).astype(BF16)


def _ffn_call(layer, xs_l, g_l, xs_c, g_c, w_gate, w_up, w_down):
    rl = xs_l.shape[1]
    rc = 0 if xs_c is None else xs_c.shape[1]
    nf = EXPERT_FF // FF_TILE
    nd = D // FF_TILE
    up = lambda e, s: (layer, e, 0, jnp.minimum(s, nf - 1))
    down = lambda s: jnp.maximum(s - nf, 0)
    in_specs = [pl.BlockSpec((None, rl, D), lambda e, s: (e, 0, 0))]
    args = [xs_l]
    if rc:
        in_specs.append(pl.BlockSpec((None, rc, D), lambda e, s: (e, 0, 0)))
        args.append(xs_c)
    in_specs.append(pl.BlockSpec((None, rl, 128), lambda e, s: (e, 0, 0)))
    args.append(g_l)
    if rc:
        in_specs.append(pl.BlockSpec((None, rc, 128), lambda e, s: (e, 0, 0)))
        args.append(g_c)
    in_specs += [pl.BlockSpec((None, None, D, FF_TILE), up),
                 pl.BlockSpec((None, None, D, FF_TILE), up),
                 pl.BlockSpec((None, None, EXPERT_FF, FF_TILE), lambda e, s: (layer, e, 0, down(s)))]
    args += [w_gate, w_up, w_down]
    out_specs = [pl.BlockSpec((None, rl, FF_TILE), lambda e, s: (e, 0, down(s)))]
    out_shape = [jax.ShapeDtypeStruct((N_EXPERTS, rl, D), BF16)]
    if rc:
        out_specs.append(pl.BlockSpec((None, rc, FF_TILE), lambda e, s: (e, 0, down(s))))
        out_shape.append(jax.ShapeDtypeStruct((N_EXPERTS, rc, D), BF16))
    return pl.pallas_call(
        functools.partial(_ffn_kernel, rl=rl, rc=rc, nf=nf),
        grid=(N_EXPERTS, nf + nd),
        in_specs=in_specs, out_specs=out_specs, out_shape=out_shape,
        scratch_shapes=[pltpu.VMEM((nf, rl + rc, FF_TILE), BF16)],
        compiler_params=_cp(2), name="expert_ffn",
    )(*args)


def _scatter_kernel(*refs, ne, with_ctx, has_next):
    refs = list(refs)
    slot_ref, yl_ref = refs[:2]
    refs = refs[2:]
    yc_ref = refs.pop(0) if with_ctx else None
    x_ref, gpost_ref, mod_ref = refs[:3]
    refs = refs[3:]
    if has_next:
        gpre_ref, modn_ref = refs[:2]
        refs = refs[2:]
    xo_ref = refs[0]
    hn_ref = refs[1] if has_next else None
    acc = refs[-1]
    eg = pl.program_id(1)
    r = pl.program_id(2)
    ng = pl.num_programs(1)

    def add(cap, y_ref):
        kk = ne * cap
        er = lax.broadcasted_iota(jnp.int32, (128, kk), 0)
        ec = lax.broadcasted_iota(jnp.int32, (128, kk), 1)
        expand = jnp.where(er == eg * ne + ec // cap, 1.0, 0.0).astype(BF16)
        spread = _dot(slot_ref[...].astype(BF16), expand)
        want = (lax.broadcasted_iota(jnp.int32, (BLK, kk), 1) % cap).astype(F32)
        pt = jnp.where(spread == want, 1.0, 0.0).astype(BF16)
        acc[r] += _dot(pt, y_ref[...].reshape(kk, D))

    @pl.when(eg == 0)
    def _():
        acc[r] = jnp.zeros((BLK, D), F32)

    @pl.when(r < NLAT)
    def _():
        add(CAP_LAT, yl_ref)

    if with_ctx:
        @pl.when(r == NLAT)
        def _():
            add(CAP_CTX, yc_ref)

    @pl.when(eg == ng - 1)
    def _():
        x2 = x_ref[...] + mod_ref[5:6, :] * (_rms(acc[r]) * gpost_ref[...])
        xo_ref[...] = x2
        if has_next:
            hn = (_rms(x2) * gpre_ref[...]) * (1.0 + modn_ref[1:2, :]) + modn_ref[0:1, :]
            hn_ref[...] = hn.astype(BF16)


def _scatter_call(slot_t, y_l, y_c, x_mid, g_post, modtab, g_pre_next, modtab_next):
    bsz = x_mid.shape[0]
    with_ctx = y_c is not None
    has_next = g_pre_next is not None
    n_tiles = NBLK if with_ctx else NLAT
    ne = 4
    ng = N_EXPERTS // ne
    vec = pl.BlockSpec((1, D), lambda b, g, r: (0, 0))
    modspec = pl.BlockSpec((None, None, 8, D), lambda b, g, r: (b, r // NLAT, 0, 0))
    late = lambda b, g, r: (b, jnp.where(g == ng - 1, r, 0), 0)
    in_specs = [pl.BlockSpec((None, BLK, 128), lambda b, g, r: (b, r, 0)),
                pl.BlockSpec((ne, CAP_LAT, D), lambda b, g, r: (g, b, 0))]
    args = [slot_t, y_l]
    if with_ctx:
        in_specs.append(pl.BlockSpec((ne, CAP_CTX, D), lambda b, g, r: (g, b, 0)))
        args.append(y_c)
    in_specs += [pl.BlockSpec((None, BLK, D), late), vec, modspec]
    args += [x_mid, g_post.reshape(1, D), modtab]
    if has_next:
        in_specs += [vec, modspec]
        args += [g_pre_next.reshape(1, D), modtab_next]
    out_specs = [pl.BlockSpec((None, BLK, D), late)]
    out_shape = [jax.ShapeDtypeStruct((bsz, n_tiles * BLK, D), F32)]
    if has_next:
        out_specs.append(pl.BlockSpec((None, BLK, D), late))
        out_shape.append(jax.ShapeDtypeStruct((bsz, n_tiles * BLK, D), BF16))
    return pl.pallas_call(
        functools.partial(_scatter_kernel, ne=ne, with_ctx=with_ctx, has_next=has_next),
        grid=(bsz, ng, n_tiles),
        in_specs=in_specs, out_specs=out_specs, out_shape=out_shape,
        scratch_shapes=[pltpu.VMEM((n_tiles, BLK, D), F32)],
        compiler_params=_cp(3), name="scatter",
    )(*args)


def _pack_w_in(w):
    qa, ka, va, kb, vb, dec, qb, gb, uc, vc = jnp.split(
        w, [int(s) for s in np.cumsum((WA, WA, WA, KB, VB, 2 * DEC_RANK, KB, VB, WC))], axis=-1)
    pad = jnp.zeros((D, 128 - 2 * DEC_RANK), w.dtype)
    return jnp.concatenate([vb, gb, qa, ka, va, kb, qb, uc, vc, dec, pad], axis=-1).astype(BF16)


def kernel(x, c, ctx, c_ctx, w_ada, b_ada, g_pre_mix, g_post_mix, g_pre_ffn, g_post_ffn, w_in, w_dec,
           b_dec, rpb, g_gla, ln_v_g, ln_v_b, w_sp, b_sp, w_out, w_router, w_gate, w_up, w_down):
    bsz = x.shape[0]
    xall = jnp.concatenate([x, ctx], axis=1)
    c_all = jnp.zeros((16, D), F32).at[:bsz].set(c).at[8].set(c_ctx)
    mods = _ada_call(c_all, w_ada, b_ada).reshape(DEPTH, 16, 6, D)
    lat = mods[:, :bsz]
    cx = jnp.broadcast_to(mods[:, 8:9], lat.shape)
    modtab = jnp.pad(jnp.stack([lat, cx], axis=2), ((0, 0), (0, 0), (0, 0), (0, 2), (0, 0)))

    cos_t, sa_t, sb_t = _rope_tables()
    mstack = jnp.asarray(_gla_matrices(), BF16)
    lev_ids = jnp.asarray(_gla_level_ids())
    hmask = jnp.asarray(np.stack([(np.arange(128) < DH_A), (np.arange(128) >= DH_A)]) * DH_A ** -0.5, F32)
    upper = (lax.broadcasted_iota(jnp.int32, (T, T), 0) < lax.broadcasted_iota(jnp.int32, (T, T), 1)).astype(BF16)

    h = _prenorm_call(xall, g_pre_mix[0], modtab[0])
    for l in range(DEPTH):
        last = l == DEPTH - 1
        nb = NLAT if last else NBLK
        P = _proj_call(h.reshape(bsz * TT, D), _pack_w_in(w_in[l])).reshape(bsz, TT, NP)
        o_a = _na_call(P, _na_bias(rpb[l]), hmask, nb)
        wd_pad = jnp.zeros((2, 128, KB), F32)
        wd_pad = wd_pad.at[0, :DEC_RANK].set(w_dec[l, 0]).at[1, DEC_RANK:2 * DEC_RANK].set(w_dec[l, 1])
        o_g = _gla_call(P, wd_pad, b_dec[l].reshape(2, 1, KB), cos_t, sa_t, sb_t, mstack, lev_ids)
        bs_rep = jnp.broadcast_to(b_sp[l][:, :, None], (GC, MIX_CHUNK, 128))
        o_c = _gmlp_call(P, ln_v_g[l], ln_v_b[l], w_sp[l], bs_rep, nb)
        wr_pad = jnp.pad(w_router[l], ((0, 0), (0, 128 - N_EXPERTS)))
        x_mid, h2, lg = _out_call(o_a, o_g, P, o_c, w_out[l].astype(BF16), xall, g_gla[l], g_post_mix[l],
                                  g_pre_ffn[l], modtab[l], wr_pad, nb)
        routed = _route_call(lg, upper, not last)
        slot_l, aff_l, slot_t = routed[:3]
        xs_l, gt_l = _gather_call(slot_l, aff_l, h2, T, CAP_LAT, 0)
        if last:
            (y_l,) = _ffn_call(l, xs_l, gt_l, None, None, w_gate, w_up, w_down)
            (xall,) = _scatter_call(slot_t, y_l, None, x_mid, g_post_ffn[l], modtab[l], None, None)
        else:
            slot_c, aff_c = routed[3:]
            xs_c, gt_c = _gather_call(slot_c, aff_c, h2, L, CAP_CTX, NLAT)
            y_l, y_c = _ffn_call(l, xs_l, gt_l, xs_c, gt_c, w_gate, w_up, w_down)
            xall, h = _scatter_call(slot_t, y_l, y_c, x_mid, g_post_ffn[l], modtab[l], g_pre_mix[l + 1],
                                    modtab[l + 1])
    return xall
```

```python
import functools

import numpy as np
import jax
import jax.numpy as jnp
from jax import lax
from jax.experimental import pallas as pl
from jax.experimental.pallas import tpu as pltpu

F32 = jnp.float32
BF16 = jnp.bfloat16

D = 2048
T = 2048
L = 256
TT = T + L
BLK = 256
NBLK = TT // BLK
NLAT = T // BLK
DEPTH = 2
GRID_W = 64
ROWS = T // GRID_W
HA, DH_A = 8, 64
WIN_ROWS, WIN_COLS = 8, 16
HB, DK_B, DV_B = 4, 128, 256
DEC_RANK = 16
GATE_TAU = 16.0
GC, CG, MIX_CHUNK = 4, 128, 128
N_EXPERTS = 16
EXPERT_FF = 2048
CAP_FACTOR = 2
ROPE_BASE = 10000.0
EPS = 1e-6
LOG2E = 1.4426950408889634
WA, KB, VB, WC = HA * DH_A, HB * DK_B, HB * DV_B, GC * CG

NP = 2 * VB + 7 * 512 + 128
C_VB, C_GB, C_QA, C_KA, C_VA, C_KB, C_QB, C_UC, C_VC, C_DEC = (
    0, 1024, 2048, 2560, 3072, 3584, 4096, 4608, 5120, 5632)

NA_QROWS = 4
NA_KROWS = 12
NA_KEYS = NA_KROWS * GRID_W
NEG = -1e30

GLA_LEVELS = 8
FF_TILE = 256
CAP_LAT = CAP_FACTOR * T // N_EXPERTS
CAP_CTX = CAP_FACTOR * L // N_EXPERTS
VMEM_LIMIT = 56 * 1024 * 1024


def _cp(n_axes):
    return pltpu.CompilerParams(dimension_semantics=("arbitrary",) * n_axes,
                                vmem_limit_bytes=VMEM_LIMIT)


def _dot(a, b):
    return jnp.dot(a, b, preferred_element_type=F32)


def _dot_nt(a, b):
    return lax.dot_general(a, b, (((1,), (1,)), ((), ())), preferred_element_type=F32)


def _dot_tn(a, b):
    return lax.dot_general(a, b, (((0,), (0,)), ((), ())), preferred_element_type=F32)


def _rms(x):
    return x * lax.rsqrt(jnp.mean(x * x, axis=-1, keepdims=True) + EPS)


def _sigmoid(x):
    return 1.0 / (1.0 + jnp.exp(-x))


def _split_bf16(x):
    hi = x.astype(BF16)
    lo = (x - hi.astype(F32)).astype(BF16)
    return hi, lo


def _ada_kernel(c_ref, w_ref, b_ref, o_ref):
    cv = c_ref[...]
    s = cv * _sigmoid(cv)
    o_ref[...] = jnp.dot(s, w_ref[...], preferred_element_type=F32,
                         precision=lax.Precision.HIGHEST) + b_ref[...]


def _ada_call(c_all, w_ada, b_ada):
    tn = 1024
    n6 = w_ada.shape[-1]
    return pl.pallas_call(
        _ada_kernel,
        grid=(DEPTH, n6 // tn),
        in_specs=[pl.BlockSpec((16, D), lambda l, n: (0, 0)),
                  pl.BlockSpec((None, D, tn), lambda l, n: (l, 0, n)),
                  pl.BlockSpec((None, 1, tn), lambda l, n: (l, 0, n))],
        out_specs=pl.BlockSpec((None, 16, tn), lambda l, n: (l, 0, n)),
        out_shape=jax.ShapeDtypeStruct((DEPTH, 16, n6), F32),
        compiler_params=_cp(2), name="ada",
    )(c_all, w_ada, b_ada.reshape(DEPTH, 1, n6))


def _stream_block(x_ref, c_ref):
    return jnp.where(pl.program_id(1) < NLAT, x_ref[...], c_ref[...])


def _stream_specs():
    return [pl.BlockSpec((None, BLK, D), lambda b, j: (b, jnp.minimum(j, NLAT - 1), 0)),
            pl.BlockSpec((None, BLK, D), lambda b, j: (b, 0, 0))]


def _prenorm_kernel(x_ref, c_ref, g_ref, mod_ref, h_ref):
    y = _rms(_stream_block(x_ref, c_ref)) * g_ref[...]
    h_ref[...] = (y * (1.0 + mod_ref[1:2, :]) + mod_ref[0:1, :]).astype(BF16)


def _prenorm_call(x, ctx, g, modtab):
    bsz = x.shape[0]
    return pl.pallas_call(
        _prenorm_kernel,
        grid=(bsz, NBLK),
        in_specs=_stream_specs() + [
            pl.BlockSpec((1, D), lambda b, j: (0, 0)),
            pl.BlockSpec((None, None, 8, D), lambda b, j: (b, j // NLAT, 0, 0))],
        out_specs=pl.BlockSpec((None, BLK, D), lambda b, j: (b, j, 0)),
        out_shape=jax.ShapeDtypeStruct((bsz, TT, D), BF16),
        compiler_params=_cp(2), name="prenorm",
    )(x, ctx, g.reshape(1, D), modtab)


def _proj_kernel(h_ref, w_ref, o_ref):
    o_ref[...] = _dot(h_ref[...], w_ref[...]).astype(BF16)


def _proj_call(h2d, w_pack):
    m = h2d.shape[0]
    tm = 1024 if m % 1024 == 0 else 768
    tn = NP // 5
    return pl.pallas_call(
        _proj_kernel,
        grid=(NP // tn, m // tm),
        in_specs=[pl.BlockSpec((tm, D), lambda n, i: (i, 0)),
                  pl.BlockSpec((D, tn), lambda n, i: (0, n))],
        out_specs=pl.BlockSpec((tm, tn), lambda n, i: (i, n)),
        out_shape=jax.ShapeDtypeStruct((m, NP), BF16),
        compiler_params=_cp(2), name="proj_in",
    )(h2d, w_pack)


def _softmax_pv(s_list, v_list):
    m = s_list[0].max(axis=-1, keepdims=True)
    for s in s_list[1:]:
        m = jnp.maximum(m, s.max(axis=-1, keepdims=True))
    acc = None
    for s, v in zip(s_list, v_list):
        o = _dot(jnp.exp(s - m).astype(BF16), v)
        acc = o if acc is None else acc + o
    return acc[:, :128] / acc[:, 128:]


def _na_kernel(q_ref, k_ref, v_ref, bias_ref, hm_ref, o_ref):
    j = pl.program_id(1)
    lane = lax.broadcasted_iota(jnp.int32, (BLK, 128), 1)
    low = lane < DH_A

    def run(local_start):
        for p in range(HA // 2):
            sl = slice(128 * p, 128 * p + 128)
            q2 = q_ref[:, sl]
            kc = k_ref[T:TT, sl]
            vc = jnp.concatenate([v_ref[T:TT, sl], jnp.ones((L, 128), BF16)], axis=1)
            if local_start is not None:
                kl = k_ref[pl.ds(local_start, NA_KEYS), sl]
                vl = jnp.concatenate([v_ref[pl.ds(local_start, NA_KEYS), sl],
                                      jnp.ones((NA_KEYS, 128), BF16)], axis=1)
            pair = []
            for hh in range(2):
                qm = (q2.astype(F32) * hm_ref[hh:hh + 1, :]).astype(BF16)
                s_ctx = _dot_nt(qm, kc)
                if local_start is not None:
                    s_loc = _dot_nt(qm, kl) + bias_ref[2 * p + hh]
                    pair.append(_softmax_pv([s_loc, s_ctx], [vl, vc]))
                else:
                    pair.append(_softmax_pv([s_ctx], [vc]))
            o_ref[:, sl] = jnp.where(low, pair[0], pair[1]).astype(BF16)

    @pl.when(j < NLAT)
    def _():
        krow = jnp.clip(j * NA_QROWS - WIN_ROWS // 2, 0, ROWS - NA_KROWS)
        run(pl.multiple_of(krow * GRID_W, GRID_W))

    @pl.when(j == NLAT)
    def _():
        run(None)


def _na_call(P, bias, hmask, n_blocks):
    bsz = P.shape[0]

    def bias_idx(b, j):
        return (jnp.where(j == 0, 0, jnp.where(j == NLAT - 1, 2, 1)), 0, 0, 0)

    return pl.pallas_call(
        _na_kernel,
        grid=(bsz, n_blocks),
        in_specs=[pl.BlockSpec((None, BLK, WA), lambda b, j: (b, j, C_QA // WA)),
                  pl.BlockSpec((None, TT, WA), lambda b, j: (b, 0, C_KA // WA)),
                  pl.BlockSpec((None, TT, WA), lambda b, j: (b, 0, C_VA // WA)),
                  pl.BlockSpec((None, HA, BLK, NA_KEYS), bias_idx),
                  pl.BlockSpec((2, 128), lambda b, j: (0, 0))],
        out_specs=pl.BlockSpec((None, BLK, WA), lambda b, j: (b, j, 0)),
        out_shape=jax.ShapeDtypeStruct((bsz, n_blocks * BLK, WA), BF16),
        compiler_params=_cp(2), name="nbr_attn",
    )(P, P, P, bias, hmask)


def _na_bias(rpb_l):
    n_dr, n_dc = 2 * WIN_ROWS - 1, 2 * WIN_COLS - 1
    cq = np.arange(GRID_W)[:, None]
    ck = np.arange(GRID_W)[None, :]
    cs = np.clip(cq - WIN_COLS // 2, 0, GRID_W - WIN_COLS)
    col_ok = (ck >= cs) & (ck < cs + WIN_COLS)
    dc = np.clip(ck - cq + WIN_COLS - 1, 0, n_dc - 1)
    col_sel = (dc[:, :, None] == np.arange(n_dc)).astype(np.float32)
    qi = np.arange(NA_QROWS)[:, None]
    kj = np.arange(NA_KROWS)[None, :]
    row_sel, row_ok = [], []
    for rb in (0, 1, NLAT - 1):
        r = rb * NA_QROWS + qi
        kr = int(np.clip(rb * NA_QROWS - WIN_ROWS // 2, 0, ROWS - NA_KROWS)) + kj
        rs = np.clip(r - WIN_ROWS // 2, 0, ROWS - WIN_ROWS)
        ok = (kr >= rs) & (kr < rs + WIN_ROWS)
        dr = kr - r + WIN_ROWS - 1
        row_sel.append(((dr[:, :, None] == np.arange(n_dr)) & ok[:, :, None]).astype(np.float32))
        row_ok.append(ok)
    row_sel, row_ok = np.stack(row_sel), np.stack(row_ok)
    colx = jnp.einsum("qke,hde->hdqk", col_sel, rpb_l, precision=lax.Precision.HIGHEST)
    tab = jnp.einsum("tikd,hdqc->thiqkc", row_sel, colx, precision=lax.Precision.HIGHEST)
    valid = row_ok[:, None, :, None, :, None] & col_ok[None, None, None, :, None, :]
    tab = jnp.where(valid, tab, NEG)
    return tab.reshape(3, HA, BLK, NA_KEYS).astype(F32)


def _gla_matrices():
    n = BLK
    i = np.arange(n)[:, None]
    t = np.arange(n)[None, :]
    out = np.zeros((2, (GLA_LEVELS + 2) * n, n), np.float32)
    for lvl in range(GLA_LEVELS):
        s = n >> (lvl + 1)
        blk0 = (i // (2 * s)) * (2 * s)
        m = blk0 + s - 1
        fwd = np.where(i > m, (t > m) & (t <= i), (t > i) & (t <= m))
        bwd = np.where(i <= m, (t >= i) & (t <= m), (t > m) & (t < i))
        out[0, lvl * n:(lvl + 1) * n] = fwd
        out[1, lvl * n:(lvl + 1) * n] = bwd
    out[0, GLA_LEVELS * n:(GLA_LEVELS + 1) * n] = t <= i
    out[1, GLA_LEVELS * n:(GLA_LEVELS + 1) * n] = t >= i
    out[0, (GLA_LEVELS + 1) * n:] = t > i
    out[1, (GLA_LEVELS + 1) * n:] = t < i
    return out


def _gla_level_ids():
    h = BLK // 2
    i = np.arange(h)[:, None]
    j = np.arange(h)[None, :]
    x = np.maximum(i ^ j, 1)
    lvl = GLA_LEVELS - 1 - np.floor(np.log2(x)).astype(np.int32)
    fwd = np.where(i == j, -1, np.where(i > j, lvl, -2))
    bwd = np.where(i == j, -1, np.where(i < j, lvl, -2))
    return np.stack([fwd, bwd]).astype(np.int32)


def _gla_body(rev, kb_ref, qb_ref, vb_ref, dec_ref, wd_ref, bd_ref, cos_ref, sa_ref, sb_ref,
              m_ref, lev_ref, o_ref, st_ref):
    half = BLK // 2
    z = _dot(dec_ref[...], wd_ref[...].astype(BF16)) + bd_ref[...]
    la = (jnp.minimum(z, 0.0) - jnp.log1p(jnp.exp(-jnp.abs(z)))) * (LOG2E / GATE_TAU)
    la_b = la.astype(BF16)

    def seg_exp(lvl):
        return jnp.exp2(_dot(m_ref[lvl * BLK:(lvl + 1) * BLK, :], la_b))

    row = lax.broadcasted_iota(jnp.int32, (BLK, DK_B), 0)
    lev = lev_ref[...]
    on_diag = lev == -1
    at_level = {lvl: lev == lvl for lvl in range(1, GLA_LEVELS)}
    cosv, sav, sbv = cos_ref[...], sa_ref[...], sb_ref[...]

    def rope(x):
        return x * cosv + pltpu.roll(x, DK_B - 32, 1) * sav + pltpu.roll(x, 32, 1) * sbv

    e_in = seg_exp(GLA_LEVELS)
    e_out = seg_exp(GLA_LEVELS + 1)
    e_lvls = [seg_exp(lvl) for lvl in range(GLA_LEVELS)]
    q_rows = slice(0, half) if rev else slice(half, BLK)
    k_rows = slice(half, BLK) if rev else slice(0, half)

    outs = []
    for hh in range(2):
        ks = slice(DK_B * hh, DK_B * (hh + 1))
        q = rope(qb_ref[:, ks].astype(F32)) * (DK_B ** -0.5)
        k = rope(kb_ref[:, ks].astype(F32))
        v = vb_ref[:, DV_B * hh:DV_B * (hh + 1)]
        self_w = jnp.sum(q * k, axis=-1, keepdims=True)
        e = e_lvls[0][:, ks]
        cross = _dot_nt((q[q_rows] * e[q_rows]).astype(BF16), (k[k_rows] * e[k_rows]).astype(BF16))
        diag = [jnp.where(on_diag, self_w[c * half:(c + 1) * half], 0.0) for c in range(2)]
        for lvl in range(1, GLA_LEVELS):
            s = BLK >> (lvl + 1)
            e = e_lvls[lvl][:, ks]
            if s >= 8:
                parts = []
                for i in range(BLK // s):
                    rs = slice(i * s, (i + 1) * s)
                    src = q if ((i % 2 == 1) != rev) else k
                    parts.append(src[rs] * e[rs])
                x = jnp.concatenate(parts, axis=0)
            else:
                second = ((row // s) & 1) == 1
                x = (jnp.where(second, k, q) if rev else jnp.where(second, q, k)) * e
            xb = x.astype(BF16)
            for c in range(2):
                xc = xb[c * half:(c + 1) * half]
                diag[c] = jnp.where(at_level[lvl], _dot_nt(xc, xc), diag[c])
        zero = jnp.zeros((half, half), F32)
        if rev:
            att = jnp.concatenate([jnp.concatenate([diag[0], cross], axis=1),
                                   jnp.concatenate([zero, diag[1]], axis=1)], axis=0)
        else:
            att = jnp.concatenate([jnp.concatenate([diag[0], zero], axis=1),
                                   jnp.concatenate([cross, diag[1]], axis=1)], axis=0)
        e_q = e_in[:, ks]
        st = st_ref[hh]
        o = _dot_nt((q * e_q).astype(BF16), st.astype(BF16)) + _dot(att.astype(BF16), v)
        outs.append(o.astype(BF16))
        carry = e_q[0:1, :] if rev else e_q[BLK - 1:BLK, :]
        kt = (k * e_out[:, ks]).astype(BF16)
        st_ref[hh] = st * carry + _dot_tn(v, kt)
    o_ref[...] = jnp.concatenate(outs, axis=-1)


def _gla_kernel(*refs):
    st_ref = refs[-1]
    d = pl.program_id(2)

    @pl.when(pl.program_id(3) == 0)
    def _():
        st_ref[...] = jnp.zeros_like(st_ref)

    @pl.when(d == 0)
    def _():
        _gla_body(False, *refs)

    @pl.when(d == 1)
    def _():
        _gla_body(True, *refs)


def _gla_call(P, wd_pad, bd, cos_t, sa_t, sb_t, mstack, lev_ids):
    bsz = P.shape[0]

    def blk(dd, s):
        return jnp.where(s == 0, NLAT, jnp.where(dd == 0, s - 1, NLAT - s))

    return pl.pallas_call(
        _gla_kernel,
        grid=(bsz, 2, 2, NBLK),
        in_specs=[
            pl.BlockSpec((None, BLK, 2 * DK_B), lambda b, hp, dd, s: (b, blk(dd, s), C_KB // 256 + hp)),
            pl.BlockSpec((None, BLK, 2 * DK_B), lambda b, hp, dd, s: (b, blk(dd, s), C_QB // 256 + hp)),
            pl.BlockSpec((None, BLK, 2 * DV_B), lambda b, hp, dd, s: (b, blk(dd, s), C_VB // 512 + hp)),
            pl.BlockSpec((None, BLK, 128), lambda b, hp, dd, s: (b, blk(dd, s), C_DEC // 128)),
            pl.BlockSpec((None, 128, 2 * DK_B), lambda b, hp, dd, s: (dd, 0, hp)),
            pl.BlockSpec((None, 1, 2 * DK_B), lambda b, hp, dd, s: (dd, 0, hp)),
            pl.BlockSpec((BLK, DK_B), lambda b, hp, dd, s: (blk(dd, s), 0)),
            pl.BlockSpec((BLK, DK_B), lambda b, hp, dd, s: (blk(dd, s), 0)),
            pl.BlockSpec((BLK, DK_B), lambda b, hp, dd, s: (blk(dd, s), 0)),
            pl.BlockSpec((None, (GLA_LEVELS + 2) * BLK, BLK), lambda b, hp, dd, s: (dd, 0, 0)),
            pl.BlockSpec((None, BLK // 2, BLK // 2), lambda b, hp, dd, s: (dd, 0, 0)),
        ],
        out_specs=pl.BlockSpec((None, None, BLK, 2 * DV_B),
                               lambda b, hp, dd, s: (dd, b, blk(dd, s), hp)),
        out_shape=jax.ShapeDtypeStruct((2, bsz, TT, VB), BF16),
        scratch_shapes=[pltpu.VMEM((2, DV_B, DK_B), F32)],
        compiler_params=_cp(4), name="gla",
    )(P, P, P, P, wd_pad, bd, cos_t, sa_t, sb_t, mstack, lev_ids)


def _rope_tables():
    t = jnp.arange(T)
    half = DK_B // 2
    inv = ROPE_BASE ** (-jnp.arange(0, half, 2, dtype=F32) / half)

    def tab(pos):
        ang = pos.astype(F32)[:, None] * inv[None, :]
        return jnp.concatenate([ang, ang], axis=-1)

    ang = jnp.concatenate([tab(t // GRID_W), tab(t % GRID_W)], axis=-1)
    cos, sin = jnp.cos(ang), jnp.sin(ang)
    first = (np.arange(DK_B) % half) < (half // 2)
    sa = jnp.where(first[None, :], -sin, 0.0)
    sb = jnp.where(first[None, :], 0.0, sin)
    ident = jnp.ones((L, DK_B), F32)
    zero = jnp.zeros((L, DK_B), F32)
    return (jnp.concatenate([cos, ident]), jnp.concatenate([sa, zero]), jnp.concatenate([sb, zero]))


def _gelu(x):
    return 0.5 * x * (1.0 + jnp.tanh(0.7978845608028654 * (x + 0.044715 * (x * x * x))))


def _gmlp_kernel(u_ref, v_ref, g_ref, b_ref, w_ref, bs_ref, o_ref):
    for ch in range(BLK // MIX_CHUNK):
        rs = slice(ch * MIX_CHUNK, (ch + 1) * MIX_CHUNK)
        u = _gelu(u_ref[rs, :].astype(F32))
        v = _gelu(v_ref[rs, :].astype(F32))
        for g in range(GC):
            cs = slice(g * CG, (g + 1) * CG)
            vg = v[:, cs]
            mu = jnp.mean(vg, axis=-1, keepdims=True)
            var = jnp.mean(jnp.square(vg - mu), axis=-1, keepdims=True)
            vn = (vg - mu) * lax.rsqrt(var + EPS) * g_ref[:, cs] + b_ref[:, cs]
            s = _dot(w_ref[g].astype(BF16), vn.astype(BF16)) + bs_ref[g]
            o_ref[rs, cs] = (u[:, cs] * s).astype(BF16)


def _gmlp_call(P, ln_g, ln_b, w_sp, bs_rep, n_blocks):
    bsz = P.shape[0]
    return pl.pallas_call(
        _gmlp_kernel,
        grid=(bsz, n_blocks),
        in_specs=[pl.BlockSpec((None, BLK, WC), lambda b, j: (b, j, C_UC // WC)),
                  pl.BlockSpec((None, BLK, WC), lambda b, j: (b, j, C_VC // WC)),
                  pl.BlockSpec((1, WC), lambda b, j: (0, 0)),
                  pl.BlockSpec((1, WC), lambda b, j: (0, 0)),
                  pl.BlockSpec((GC, MIX_CHUNK, MIX_CHUNK), lambda b, j: (0, 0, 0)),
                  pl.BlockSpec((GC, MIX_CHUNK, 128), lambda b, j: (0, 0, 0))],
        out_specs=pl.BlockSpec((None, BLK, WC), lambda b, j: (b, j, 0)),
        out_shape=jax.ShapeDtypeStruct((bsz, n_blocks * BLK, WC), BF16),
        compiler_params=_cp(2), name="gmlp",
    )(P, P, ln_g.reshape(1, WC), ln_b.reshape(1, WC), w_sp, bs_rep)


def _out_kernel(*refs, split):
    oa_ref, of_ref, ob_ref, gb_ref, oc_ref, w_ref = refs[:6]
    if split:
        x_in = _stream_block(refs[6], refs[7])
    else:
        x_in = refs[6][...]
    gg_ref, gpost_ref, gpre_ref, mod_ref, wr_ref, xo_ref, h2_ref, lg_ref = refs[7 + split:]
    o = of_ref[...].astype(F32) + ob_ref[...].astype(F32)
    gb = gb_ref[...].astype(F32)
    parts = []
    for h in range(HB):
        cs = slice(h * DV_B, (h + 1) * DV_B)
        gh = gb[:, cs]
        parts.append((_rms(o[:, cs]) * gg_ref[:, cs] * (gh * _sigmoid(gh))).astype(BF16))
    gl = jnp.concatenate(parts, axis=-1)
    y = (_dot(oa_ref[...], w_ref[0:WA, :]) + _dot(gl, w_ref[WA:WA + VB, :])
         + _dot(oc_ref[...], w_ref[WA + VB:, :]))
    x1 = x_in + mod_ref[2:3, :] * (_rms(y) * gpost_ref[...])
    xo_ref[...] = x1
    h2 = (_rms(x1) * gpre_ref[...]) * (1.0 + mod_ref[4:5, :]) + mod_ref[3:4, :]
    h2_ref[...] = h2.astype(BF16)
    hh, hl = _split_bf16(h2)
    wh, wl = _split_bf16(wr_ref[...])
    lg_ref[...] = _dot(hh, wh) + _dot(hl, wh) + _dot(hh, wl)


def _out_call(o_a, o_g, P, o_c, w_out_b, stream, g_gla, g_post, g_pre, modtab, wr_pad, n_blocks):
    bsz = P.shape[0]
    rows = n_blocks * BLK
    split = isinstance(stream, tuple)
    rowspec = lambda w: pl.BlockSpec((None, BLK, w), lambda b, j: (b, j, 0))
    vec = lambda w: pl.BlockSpec((1, w), lambda b, j: (0, 0))
    return pl.pallas_call(
        functools.partial(_out_kernel, split=split),
        grid=(bsz, n_blocks),
        in_specs=[rowspec(WA),
                  pl.BlockSpec((None, None, BLK, VB), lambda b, j: (0, b, j, 0)),
                  pl.BlockSpec((None, None, BLK, VB), lambda b, j: (1, b, j, 0)),
                  pl.BlockSpec((None, BLK, VB), lambda b, j: (b, j, C_GB // VB)),
                  rowspec(WC),
                  pl.BlockSpec((D, D), lambda b, j: (0, 0))]
                 + (_stream_specs() if split else [rowspec(D)])
                 + [vec(VB), vec(D), vec(D),
                    pl.BlockSpec((None, None, 8, D), lambda b, j: (b, j // NLAT, 0, 0)),
                    pl.BlockSpec((D, 128), lambda b, j: (0, 0))],
        out_specs=[rowspec(D), rowspec(D), rowspec(128)],
        out_shape=[jax.ShapeDtypeStruct((bsz, rows, D), F32),
                   jax.ShapeDtypeStruct((bsz, rows, D), BF16),
                   jax.ShapeDtypeStruct((bsz, rows, 128), F32)],
        compiler_params=_cp(2), name="proj_out",
    )(o_a, o_g, o_g, P, o_c, w_out_b, *(stream if split else (stream,)), g_gla.reshape(1, VB),
      g_post.reshape(1, D), g_pre.reshape(1, D), modtab, wr_pad)


def _route_set(lg, cap, upper):
    n = lg.shape[0]
    lt = lg.T[:N_EXPERTS, :]
    ex = jnp.exp(lt - lt.max(axis=0, keepdims=True))
    aff = ex / ex.sum(axis=0, keepdims=True)
    bits = pltpu.bitcast(aff, jnp.int32)
    capf = jnp.float32(cap)

    def body(i, prefix):
        cand = prefix | jnp.left_shift(jnp.int32(1), 30 - i)
        cnt = jnp.sum(jnp.where(bits >= cand, 1.0, 0.0), axis=1, keepdims=True)
        return jnp.where(cnt >= capf, cand, prefix)

    thr = lax.fori_loop(0, 31, body, jnp.zeros((N_EXPERTS, 1), jnp.int32))
    gt = jnp.where(bits > thr, 1.0, 0.0)
    eq = jnp.where(bits == thr, 1.0, 0.0)
    need = capf - gt.sum(axis=1, keepdims=True)
    rank_eq = _dot(eq.astype(BF16), upper)
    sel = gt + eq * jnp.where(rank_eq < need, 1.0, 0.0)
    slot = _dot(sel.astype(BF16), upper)
    slot = jnp.where(sel > 0.5, slot, -1.0)
    pad = jnp.full((128 - N_EXPERTS, n), -1.0, F32)
    slot_t = jnp.concatenate([slot, pad], axis=0).T
    return slot, aff, slot_t


def _route_kernel(lg_ref, u_ref, *out_refs, with_ctx):
    sl, af, st = _route_set(lg_ref[0:T, :], CAP_LAT, u_ref[...])
    out_refs[0][...] = sl
    out_refs[1][...] = af
    out_refs[2][0:T, :] = st
    if with_ctx:
        sl, af, st = _route_set(lg_ref[T:TT, :], CAP_CTX, u_ref[0:L, 0:L])
        out_refs[3][...] = sl
        out_refs[4][...] = af
        out_refs[2][T:TT, :] = st


def _route_call(lg, upper, with_ctx):
    bsz, rows, _ = lg.shape
    en = lambda n: pl.BlockSpec((None, N_EXPERTS, n), lambda b: (b, 0, 0))
    out_specs = [en(T), en(T), pl.BlockSpec((None, rows, 128), lambda b: (b, 0, 0))]
    out_shape = [jax.ShapeDtypeStruct((bsz, N_EXPERTS, T), F32),
                 jax.ShapeDtypeStruct((bsz, N_EXPERTS, T), F32),
                 jax.ShapeDtypeStruct((bsz, rows, 128), F32)]
    if with_ctx:
        out_specs += [en(L), en(L)]
        out_shape += [jax.ShapeDtypeStruct((bsz, N_EXPERTS, L), F32)] * 2
    return pl.pallas_call(
        functools.partial(_route_kernel, with_ctx=with_ctx),
        grid=(bsz,),
        in_specs=[pl.BlockSpec((None, rows, 128), lambda b: (b, 0, 0)),
                  pl.BlockSpec((T, T), lambda b: (0, 0))],
        out_specs=out_specs, out_shape=out_shape,
        compiler_params=_cp(1), name="route",
    )(lg, upper)


def _gather_kernel(*refs, n_sets):
    e = pl.program_id(1)
    ins, outs = refs[:3 * n_sets], refs[3 * n_sets:]
    for k in range(n_sets):
        slot_ref, aff_ref, h_ref = ins[3 * k:3 * k + 3]
        xs_ref, g_ref = outs[2 * k:2 * k + 2]
        cap, n = xs_ref.shape[0], h_ref.shape[0]
        srow = slot_ref[pl.ds(e, 1), :]
        arow = aff_ref[pl.ds(e, 1), :]
        sid = lax.broadcasted_iota(jnp.int32, (cap, n), 0).astype(F32)
        hit = sid == srow
        xs_ref[...] = _dot(jnp.where(hit, 1.0, 0.0).astype(BF16), h_ref[...]).astype(BF16)
        g = jnp.sum(jnp.where(hit, arow, 0.0), axis=1, keepdims=True)
        g_ref[...] = jnp.broadcast_to(g, (cap, 128))


def _gather_call(h2, sets):
    bsz = h2.shape[0]
    in_specs, args, out_specs, out_shape = [], [], [], []
    for slot, aff, n, cap, blk_idx in sets:
        in_specs += [pl.BlockSpec((None, N_EXPERTS, n), lambda b, e: (b, 0, 0)),
                     pl.BlockSpec((None, N_EXPERTS, n), lambda b, e: (b, 0, 0)),
                     pl.BlockSpec((None, n, D), lambda b, e, i=blk_idx: (b, i, 0))]
        args += [slot, aff, h2]
        out_specs += [pl.BlockSpec((None, cap, D), lambda b, e: (e, b, 0)),
                      pl.BlockSpec((None, cap, 128), lambda b, e: (e, b, 0))]
        out_shape += [jax.ShapeDtypeStruct((N_EXPERTS, bsz * cap, D), BF16),
                      jax.ShapeDtypeStruct((N_EXPERTS, bsz * cap, 128), F32)]
    return pl.pallas_call(
        functools.partial(_gather_kernel, n_sets=len(sets)),
        grid=(bsz, N_EXPERTS),
        in_specs=in_specs, out_specs=out_specs, out_shape=out_shape,
        compiler_params=_cp(2), name="gather",
    )(*args)


def _ffn_kernel(*refs, rl, rc, nf):
    if rc:
        xl_ref, xc_ref, gl_ref, gc_ref, wg_ref, wu_ref, wd_ref, yl_ref, yc_ref, hid = refs
        groups = ((xl_ref, gl_ref, yl_ref, 0, rl), (xc_ref, gc_ref, yc_ref, rl, rc))
    else:
        xl_ref, gl_ref, wg_ref, wu_ref, wd_ref, yl_ref, hid = refs
        groups = ((xl_ref, gl_ref, yl_ref, 0, rl),)
    s = pl.program_id(1)

    @pl.when(s < nf)
    def _():
        wg = wg_ref[...].astype(BF16)
        wu = wu_ref[...].astype(BF16)
        for x_ref, _, _, r0, nr in groups:
            xv = x_ref[...]
            a = _dot(xv, wg)
            u = _dot(xv, wu)
            hid[s, r0:r0 + nr, :] = ((a * _sigmoid(a)) * u).astype(BF16)

    @pl.when(s >= nf)
    def _():
        wd = wd_ref[...].astype(BF16)
        for _, g_ref, y_ref, r0, nr in groups:
            hm = jnp.concatenate([hid[k, r0:r0 + nr, :] for k in range(nf)], axis=1)
            y = _dot(hm, wd)
            for c in range(FF_TILE // 128):
                cs = slice(128 * c, 128 * (c + 1))
                y_ref[:, cs] = (y[:, cs] * g_ref[...]).astype(BF16)


def _ffn_call(layer, xs_l, g_l, xs_c, g_c, w_gate, w_up, w_down):
    rl = xs_l.shape[1]
    rc = 0 if xs_c is None else xs_c.shape[1]
    nf = EXPERT_FF // FF_TILE
    nd = D // FF_TILE
    up = lambda e, s: (layer, e, 0, jnp.minimum(s, nf - 1))
    down = lambda s: jnp.maximum(s - nf, 0)
    in_specs = [pl.BlockSpec((None, rl, D), lambda e, s: (e, 0, 0))]
    args = [xs_l]
    if rc:
        in_specs.append(pl.BlockSpec((None, rc, D), lambda e, s: (e, 0, 0)))
        args.append(xs_c)
    in_specs.append(pl.BlockSpec((None, rl, 128), lambda e, s: (e, 0, 0)))
    args.append(g_l)
    if rc:
        in_specs.append(pl.BlockSpec((None, rc, 128), lambda e, s: (e, 0, 0)))
        args.append(g_c)
    in_specs += [pl.BlockSpec((None, None, D, FF_TILE), up),
                 pl.BlockSpec((None, None, D, FF_TILE), up),
                 pl.BlockSpec((None, None, EXPERT_FF, FF_TILE), lambda e, s: (layer, e, 0, down(s)))]
    args += [w_gate, w_up, w_down]
    out_specs = [pl.BlockSpec((None, rl, FF_TILE), lambda e, s: (e, 0, down(s)))]
    out_shape = [jax.ShapeDtypeStruct((N_EXPERTS, rl, D), BF16)]
    if rc:
        out_specs.append(pl.BlockSpec((None, rc, FF_TILE), lambda e, s: (e, 0, down(s))))
        out_shape.append(jax.ShapeDtypeStruct((N_EXPERTS, rc, D), BF16))
    return pl.pallas_call(
        functools.partial(_ffn_kernel, rl=rl, rc=rc, nf=nf),
        grid=(N_EXPERTS, nf + nd),
        in_specs=in_specs, out_specs=out_specs, out_shape=out_shape,
        scratch_shapes=[pltpu.VMEM((nf, rl + rc, FF_TILE), BF16)],
        compiler_params=_cp(2), name="expert_ffn",
    )(*args)


def _scatter_kernel(*refs, ne, with_ctx, has_next):
    refs = list(refs)
    slot_ref, yl_ref = refs[:2]
    refs = refs[2:]
    yc_ref = refs.pop(0) if with_ctx else None
    x_ref, gpost_ref, mod_ref = refs[:3]
    refs = refs[3:]
    if has_next:
        gpre_ref, modn_ref = refs[:2]
        refs = refs[2:]
    xo_ref = refs[0]
    hn_ref = refs[1] if has_next else None
    acc = refs[-1]
    eg = pl.program_id(1)
    r = pl.program_id(2)
    ng = pl.num_programs(1)

    def add(cap, y_ref):
        kk = ne * cap
        er = lax.broadcasted_iota(jnp.int32, (128, kk), 0)
        ec = lax.broadcasted_iota(jnp.int32, (128, kk), 1)
        expand = jnp.where(er == eg * ne + ec // cap, 1.0, 0.0).astype(BF16)
        spread = _dot(slot_ref[...].astype(BF16), expand)
        want = (lax.broadcasted_iota(jnp.int32, (BLK, kk), 1) % cap).astype(F32)
        pt = jnp.where(spread == want, 1.0, 0.0).astype(BF16)
        acc[r] += _dot(pt, y_ref[...].reshape(kk, D))

    @pl.when(eg == 0)
    def _():
        acc[r] = jnp.zeros((BLK, D), F32)

    @pl.when(r < NLAT)
    def _():
        add(CAP_LAT, yl_ref)

    if with_ctx:
        @pl.when(r == NLAT)
        def _():
            add(CAP_CTX, yc_ref)

    @pl.when(eg == ng - 1)
    def _():
        x2 = x_ref[...] + mod_ref[5:6, :] * (_rms(acc[r]) * gpost_ref[...])
        xo_ref[...] = x2
        if has_next:
            hn = (_rms(x2) * gpre_ref[...]) * (1.0 + modn_ref[1:2, :]) + modn_ref[0:1, :]
            hn_ref[...] = hn.astype(BF16)


def _scatter_call(slot_t, y_l, y_c, x_mid, g_post, modtab, g_pre_next, modtab_next):
    bsz = x_mid.shape[0]
    with_ctx = y_c is not None
    has_next = g_pre_next is not None
    n_tiles = NBLK if with_ctx else NLAT
    ne = 8
    ng = N_EXPERTS // ne
    vec = pl.BlockSpec((1, D), lambda b, g, r: (0, 0))
    modspec = pl.BlockSpec((None, None, 8, D), lambda b, g, r: (b, r // NLAT, 0, 0))
    late = lambda b, g, r: (b, jnp.where(g == ng - 1, r, 0), 0)
    in_specs = [pl.BlockSpec((None, BLK, 128), lambda b, g, r: (b, r, 0)),
                pl.BlockSpec((ne, CAP_LAT, D), lambda b, g, r: (g, b, 0))]
    args = [slot_t, y_l]
    if with_ctx:
        in_specs.append(pl.BlockSpec((ne, CAP_CTX, D), lambda b, g, r: (g, b, 0)))
        args.append(y_c)
    in_specs += [pl.BlockSpec((None, BLK, D), late), vec, modspec]
    args += [x_mid, g_post.reshape(1, D), modtab]
    if has_next:
        in_specs += [vec, modspec]
        args += [g_pre_next.reshape(1, D), modtab_next]
    out_specs = [pl.BlockSpec((None, BLK, D), late)]
    out_shape = [jax.ShapeDtypeStruct((bsz, n_tiles * BLK, D), F32)]
    if has_next:
        out_specs.append(pl.BlockSpec((None, BLK, D), late))
        out_shape.append(jax.ShapeDtypeStruct((bsz, n_tiles * BLK, D), BF16))
    return pl.pallas_call(
        functools.partial(_scatter_kernel, ne=ne, with_ctx=with_ctx, has_next=has_next),
        grid=(bsz, ng, n_tiles),
        in_specs=in_specs, out_specs=out_specs, out_shape=out_shape,
        scratch_shapes=[pltpu.VMEM((n_tiles, BLK, D), F32)],
        compiler_params=_cp(3), name="scatter",
    )(*args)


def _pack_w_in(w):
    qa, ka, va, kb, vb, dec, qb, gb, uc, vc = jnp.split(
        w, [int(s) for s in np.cumsum((WA, WA, WA, KB, VB, 2 * DEC_RANK, KB, VB, WC))], axis=-1)
    pad = jnp.zeros((D, 128 - 2 * DEC_RANK), w.dtype)
    return jnp.concatenate([vb, gb, qa, ka, va, kb, qb, uc, vc, dec, pad], axis=-1).astype(BF16)


def kernel(x, c, ctx, c_ctx, w_ada, b_ada, g_pre_mix, g_post_mix, g_pre_ffn, g_post_ffn, w_in, w_dec,
           b_dec, rpb, g_gla, ln_v_g, ln_v_b, w_sp, b_sp, w_out, w_router, w_gate, w_up, w_down):
    bsz = x.shape[0]
    c_all = jnp.zeros((16, D), F32).at[:bsz].set(c).at[8].set(c_ctx)
    mods = _ada_call(c_all, w_ada, b_ada).reshape(DEPTH, 16, 6, D)
    lat = mods[:, :bsz]
    cx = jnp.broadcast_to(mods[:, 8:9], lat.shape)
    modtab = jnp.pad(jnp.stack([lat, cx], axis=2), ((0, 0), (0, 0), (0, 0), (0, 2), (0, 0)))

    cos_t, sa_t, sb_t = _rope_tables()
    mstack = jnp.asarray(_gla_matrices(), BF16)
    lev_ids = jnp.asarray(_gla_level_ids())
    hmask = jnp.asarray(np.stack([(np.arange(128) < DH_A), (np.arange(128) >= DH_A)]) * DH_A ** -0.5, F32)
    upper = (lax.broadcasted_iota(jnp.int32, (T, T), 0) < lax.broadcasted_iota(jnp.int32, (T, T), 1)).astype(BF16)

    stream = (x, ctx)
    h = _prenorm_call(x, ctx, g_pre_mix[0], modtab[0])
    for l in range(DEPTH):
        last = l == DEPTH - 1
        nb = NLAT if last else NBLK
        P = _proj_call(h.reshape(bsz * TT, D), _pack_w_in(w_in[l])).reshape(bsz, TT, NP)
        o_a = _na_call(P, _na_bias(rpb[l]), hmask, nb)
        wd_pad = jnp.zeros((2, 128, KB), F32)
        wd_pad = wd_pad.at[0, :DEC_RANK].set(w_dec[l, 0]).at[1, DEC_RANK:2 * DEC_RANK].set(w_dec[l, 1])
        o_g = _gla_call(P, wd_pad, b_dec[l].reshape(2, 1, KB), cos_t, sa_t, sb_t, mstack, lev_ids)
        bs_rep = jnp.broadcast_to(b_sp[l][:, :, None], (GC, MIX_CHUNK, 128))
        o_c = _gmlp_call(P, ln_v_g[l], ln_v_b[l], w_sp[l], bs_rep, nb)
        wr_pad = jnp.pad(w_router[l], ((0, 0), (0, 128 - N_EXPERTS)))
        x_mid, h2, lg = _out_call(o_a, o_g, P, o_c, w_out[l].astype(BF16), stream, g_gla[l], g_post_mix[l],
                                  g_pre_ffn[l], modtab[l], wr_pad, nb)
        routed = _route_call(lg, upper, not last)
        slot_l, aff_l, slot_t = routed[:3]
        if last:
            xs_l, gt_l = _gather_call(h2, [(slot_l, aff_l, T, CAP_LAT, 0)])
            (y_l,) = _ffn_call(l, xs_l, gt_l, None, None, w_gate, w_up, w_down)
            (stream,) = _scatter_call(slot_t, y_l, None, x_mid, g_post_ffn[l], modtab[l], None, None)
        else:
            slot_c, aff_c = routed[3:]
            xs_l, gt_l, xs_c, gt_c = _gather_call(h2, [(slot_l, aff_l, T, CAP_LAT, 0),
                                                       (slot_c, aff_c, L, CAP_CTX, NLAT)])
            y_l, y_c = _ffn_call(l, xs_l, gt_l, xs_c, gt_c, w_gate, w_up, w_down)
            stream, h = _scatter_call(slot_t, y_l, y_c, x_mid, g_post_ffn[l], modtab[l], g_pre_mix[l + 1],
                                      modtab[l + 1])
    return stream
```

```python
import functools

import numpy as np
import jax
import jax.numpy as jnp
from jax import lax
from jax.experimental import pallas as pl
from jax.experimental.pallas import tpu as pltpu

F32 = jnp.float32
BF16 = jnp.bfloat16

D = 2048
T = 2048
L = 256
TT = T + L
BLK = 256
NBLK = TT // BLK
NLAT = T // BLK
DEPTH = 2
GRID_W = 64
ROWS = T // GRID_W
HA, DH_A = 8, 64
WIN_ROWS, WIN_COLS = 8, 16
HB, DK_B, DV_B = 4, 128, 256
DEC_RANK = 16
GATE_TAU = 16.0
GC, CG, MIX_CHUNK = 4, 128, 128
N_EXPERTS = 16
EXPERT_FF = 2048
CAP_FACTOR = 2
ROPE_BASE = 10000.0
EPS = 1e-6
LOG2E = 1.4426950408889634
WA, KB, VB, WC = HA * DH_A, HB * DK_B, HB * DV_B, GC * CG

NP = 3 * D
C_VB, C_GB, C_QA, C_KA, C_VA, C_KB, C_QB, C_UC, C_VC, C_DEC = (
    0, 1024, 2048, 2560, 3072, 3584, 4096, 4608, 5120, 5632)

NA_QROWS = 4
NA_KROWS = 12
NA_KEYS = NA_KROWS * GRID_W
NEG = -1e30

GLA_LEVELS = 8
FF_TILE = 256
CAP_LAT = CAP_FACTOR * T // N_EXPERTS
CAP_CTX = CAP_FACTOR * L // N_EXPERTS
VMEM_LIMIT = 56 * 1024 * 1024


def _cp(n_axes):
    return pltpu.CompilerParams(dimension_semantics=("arbitrary",) * n_axes,
                                vmem_limit_bytes=VMEM_LIMIT)


def _dot(a, b):
    return jnp.dot(a, b, preferred_element_type=F32)


def _dot_nt(a, b):
    return lax.dot_general(a, b, (((1,), (1,)), ((), ())), preferred_element_type=F32)


def _dot_tn(a, b):
    return lax.dot_general(a, b, (((0,), (0,)), ((), ())), preferred_element_type=F32)


def _rms(x):
    return x * lax.rsqrt(jnp.mean(x * x, axis=-1, keepdims=True) + EPS)


def _sigmoid(x):
    return 1.0 / (1.0 + jnp.exp(-x))


def _split_bf16(x):
    hi = x.astype(BF16)
    lo = (x - hi.astype(F32)).astype(BF16)
    return hi, lo


def _ada_kernel(c_ref, w_ref, b_ref, o_ref):
    cv = c_ref[...]
    s = cv * _sigmoid(cv)
    o_ref[...] = jnp.dot(s, w_ref[...], preferred_element_type=F32,
                         precision=lax.Precision.HIGHEST) + b_ref[...]


def _ada_call(c_all, w_ada, b_ada):
    tn = 1024
    n6 = w_ada.shape[-1]
    return pl.pallas_call(
        _ada_kernel,
        grid=(DEPTH, n6 // tn),
        in_specs=[pl.BlockSpec((16, D), lambda l, n: (0, 0)),
                  pl.BlockSpec((None, D, tn), lambda l, n: (l, 0, n)),
                  pl.BlockSpec((None, 1, tn), lambda l, n: (l, 0, n))],
        out_specs=pl.BlockSpec((None, 16, tn), lambda l, n: (l, 0, n)),
        out_shape=jax.ShapeDtypeStruct((DEPTH, 16, n6), F32),
        compiler_params=_cp(2), name="ada",
    )(c_all, w_ada, b_ada.reshape(DEPTH, 1, n6))


def _stream_block(x_ref, c_ref):
    return jnp.where(pl.program_id(1) < NLAT, x_ref[...], c_ref[...])


def _stream_specs():
    return [pl.BlockSpec((None, BLK, D), lambda b, j: (b, jnp.minimum(j, NLAT - 1), 0)),
            pl.BlockSpec((None, BLK, D), lambda b, j: (b, 0, 0))]


def _prenorm_kernel(x_ref, c_ref, g_ref, mod_ref, h_ref):
    y = _rms(_stream_block(x_ref, c_ref)) * g_ref[...]
    h_ref[...] = (y * (1.0 + mod_ref[1:2, :]) + mod_ref[0:1, :]).astype(BF16)


def _prenorm_call(x, ctx, g, modtab):
    bsz = x.shape[0]
    return pl.pallas_call(
        _prenorm_kernel,
        grid=(bsz, NBLK),
        in_specs=_stream_specs() + [
            pl.BlockSpec((1, D), lambda b, j: (0, 0)),
            pl.BlockSpec((None, None, 8, D), lambda b, j: (b, j // NLAT, 0, 0))],
        out_specs=pl.BlockSpec((None, BLK, D), lambda b, j: (b, j, 0)),
        out_shape=jax.ShapeDtypeStruct((bsz, TT, D), BF16),
        compiler_params=_cp(2), name="prenorm",
    )(x, ctx, g.reshape(1, D), modtab)


def _proj_kernel(h_ref, w_ref, o_ref):
    o_ref[...] = _dot(h_ref[...], w_ref[...]).astype(BF16)


def _proj_call(h2d, w_pack):
    m = h2d.shape[0]
    tm = 1024 if m % 1024 == 0 else 768
    tn = NP // 3
    return pl.pallas_call(
        _proj_kernel,
        grid=(NP // tn, m // tm),
        in_specs=[pl.BlockSpec((tm, D), lambda n, i: (i, 0)),
                  pl.BlockSpec((D, tn), lambda n, i: (0, n))],
        out_specs=pl.BlockSpec((tm, tn), lambda n, i: (i, n)),
        out_shape=jax.ShapeDtypeStruct((m, NP), BF16),
        compiler_params=_cp(2), name="proj_in",
    )(h2d, w_pack)


def _softmax_pv(s_list, v_list):
    m = s_list[0].max(axis=-1, keepdims=True)
    for s in s_list[1:]:
        m = jnp.maximum(m, s.max(axis=-1, keepdims=True))
    acc = None
    for s, v in zip(s_list, v_list):
        o = _dot(jnp.exp(s - m).astype(BF16), v)
        acc = o if acc is None else acc + o
    return acc[:, :128] / acc[:, 128:]


def _na_kernel(q_ref, k_ref, v_ref, bias_ref, hm_ref, o_ref):
    j = pl.program_id(1)
    lane = lax.broadcasted_iota(jnp.int32, (BLK, 128), 1)
    low = lane < DH_A

    def run(local_start):
        for p in range(HA // 2):
            sl = slice(128 * p, 128 * p + 128)
            q2 = q_ref[:, sl]
            kc = k_ref[T:TT, sl]
            vc = jnp.concatenate([v_ref[T:TT, sl], jnp.ones((L, 128), BF16)], axis=1)
            if local_start is not None:
                kl = k_ref[pl.ds(local_start, NA_KEYS), sl]
                vl = jnp.concatenate([v_ref[pl.ds(local_start, NA_KEYS), sl],
                                      jnp.ones((NA_KEYS, 128), BF16)], axis=1)
            pair = []
            for hh in range(2):
                qm = (q2.astype(F32) * hm_ref[hh:hh + 1, :]).astype(BF16)
                s_ctx = _dot_nt(qm, kc)
                if local_start is not None:
                    s_loc = _dot_nt(qm, kl) + bias_ref[2 * p + hh]
                    pair.append(_softmax_pv([s_loc, s_ctx], [vl, vc]))
                else:
                    pair.append(_softmax_pv([s_ctx], [vc]))
            o_ref[:, sl] = jnp.where(low, pair[0], pair[1]).astype(BF16)

    @pl.when(j < NLAT)
    def _():
        krow = jnp.clip(j * NA_QROWS - WIN_ROWS // 2, 0, ROWS - NA_KROWS)
        run(pl.multiple_of(krow * GRID_W, GRID_W))

    @pl.when(j == NLAT)
    def _():
        run(None)


def _na_call(P, bias, hmask, n_blocks):
    bsz = P.shape[0]

    def bias_idx(b, j):
        return (jnp.where(j == 0, 0, jnp.where(j == NLAT - 1, 2, 1)), 0, 0, 0)

    return pl.pallas_call(
        _na_kernel,
        grid=(bsz, n_blocks),
        in_specs=[pl.BlockSpec((None, BLK, WA), lambda b, j: (b, j, C_QA // WA)),
                  pl.BlockSpec((None, TT, WA), lambda b, j: (b, 0, C_KA // WA)),
                  pl.BlockSpec((None, TT, WA), lambda b, j: (b, 0, C_VA // WA)),
                  pl.BlockSpec((None, HA, BLK, NA_KEYS), bias_idx),
                  pl.BlockSpec((2, 128), lambda b, j: (0, 0))],
        out_specs=pl.BlockSpec((None, BLK, WA), lambda b, j: (b, j, 0)),
        out_shape=jax.ShapeDtypeStruct((bsz, n_blocks * BLK, WA), BF16),
        compiler_params=_cp(2), name="nbr_attn",
    )(P, P, P, bias, hmask)


def _na_bias(rpb_l):
    n_dr, n_dc = 2 * WIN_ROWS - 1, 2 * WIN_COLS - 1
    cq = np.arange(GRID_W)[:, None]
    ck = np.arange(GRID_W)[None, :]
    cs = np.clip(cq - WIN_COLS // 2, 0, GRID_W - WIN_COLS)
    col_ok = (ck >= cs) & (ck < cs + WIN_COLS)
    dc = np.clip(ck - cq + WIN_COLS - 1, 0, n_dc - 1)
    col_sel = (dc[:, :, None] == np.arange(n_dc)).astype(np.float32)
    colx = jnp.einsum("qke,hde->hdqk", col_sel, rpb_l, precision=lax.Precision.HIGHEST)
    colx = jnp.where(col_ok, colx, NEG).astype(F32)
    return pl.pallas_call(
        _bias_kernel,
        grid=(3, HA),
        in_specs=[pl.BlockSpec((None, n_dr, GRID_W, GRID_W), lambda t, h: (h, 0, 0, 0))],
        out_specs=pl.BlockSpec((None, None, BLK, NA_KEYS), lambda t, h: (t, h, 0, 0)),
        out_shape=jax.ShapeDtypeStruct((3, HA, BLK, NA_KEYS), F32),
        compiler_params=_cp(2), name="na_bias",
    )(colx)


def _bias_kernel(colx_ref, o_ref):
    t = pl.program_id(0)
    neg = jnp.full((GRID_W, GRID_W), NEG, F32)
    for tt, rb in enumerate((0, 1, NLAT - 1)):
        @pl.when(t == tt)
        def _(rb=rb):
            k0 = int(np.clip(rb * NA_QROWS - WIN_ROWS // 2, 0, ROWS - NA_KROWS))
            for qi in range(NA_QROWS):
                r = rb * NA_QROWS + qi
                rs = int(np.clip(r - WIN_ROWS // 2, 0, ROWS - WIN_ROWS))
                for kp in range(NA_KROWS // 2):
                    pair = []
                    for kr in (k0 + 2 * kp, k0 + 2 * kp + 1):
                        inside = rs <= kr < rs + WIN_ROWS
                        pair.append(colx_ref[kr - r + WIN_ROWS - 1] if inside else neg)
                    o_ref[qi * GRID_W:(qi + 1) * GRID_W, kp * 128:(kp + 1) * 128] = (
                        jnp.concatenate(pair, axis=1))


def _gla_matrices():
    n = BLK
    i = np.arange(n)[:, None]
    t = np.arange(n)[None, :]
    out = np.zeros((2, (GLA_LEVELS + 2) * n, n), np.float32)
    for lvl in range(GLA_LEVELS):
        s = n >> (lvl + 1)
        blk0 = (i // (2 * s)) * (2 * s)
        m = blk0 + s - 1
        fwd = np.where(i > m, (t > m) & (t <= i), (t > i) & (t <= m))
        bwd = np.where(i <= m, (t >= i) & (t <= m), (t > m) & (t < i))
        out[0, lvl * n:(lvl + 1) * n] = fwd
        out[1, lvl * n:(lvl + 1) * n] = bwd
    out[0, GLA_LEVELS * n:(GLA_LEVELS + 1) * n] = t <= i
    out[1, GLA_LEVELS * n:(GLA_LEVELS + 1) * n] = t >= i
    out[0, (GLA_LEVELS + 1) * n:] = t > i
    out[1, (GLA_LEVELS + 1) * n:] = t < i
    return out


def _gla_level_ids():
    h = BLK // 2
    i = np.arange(h)[:, None]
    j = np.arange(h)[None, :]
    x = np.maximum(i ^ j, 1)
    lvl = GLA_LEVELS - 1 - np.floor(np.log2(x)).astype(np.int32)
    fwd = np.where(i == j, -1, np.where(i > j, lvl, -2))
    bwd = np.where(i == j, -1, np.where(i < j, lvl, -2))
    return np.stack([fwd, bwd]).astype(np.int32)


def _gla_body(rev, kb_ref, qb_ref, vb_ref, dec_ref, wd_ref, bd_ref, cos_ref, sa_ref, sb_ref,
              m_ref, lev_ref, o_ref, st_ref):
    half = BLK // 2
    z = _dot(dec_ref[...], wd_ref[...].astype(BF16)) + bd_ref[...]
    la = (jnp.minimum(z, 0.0) - jnp.log1p(jnp.exp(-jnp.abs(z)))) * (LOG2E / GATE_TAU)
    la_b = la.astype(BF16)

    def seg_exp(lvl):
        return jnp.exp2(_dot(m_ref[lvl * BLK:(lvl + 1) * BLK, :], la_b))

    row = lax.broadcasted_iota(jnp.int32, (BLK, DK_B), 0)
    lev = lev_ref[...]
    on_diag = lev == -1
    at_level = {lvl: lev == lvl for lvl in range(1, GLA_LEVELS)}
    cosv, sav, sbv = cos_ref[...], sa_ref[...], sb_ref[...]

    def rope(x):
        return x * cosv + pltpu.roll(x, DK_B - 32, 1) * sav + pltpu.roll(x, 32, 1) * sbv

    e_in = seg_exp(GLA_LEVELS)
    e_out = seg_exp(GLA_LEVELS + 1)
    e_lvls = [seg_exp(lvl) for lvl in range(GLA_LEVELS)]
    q_rows = slice(0, half) if rev else slice(half, BLK)
    k_rows = slice(half, BLK) if rev else slice(0, half)

    outs = []
    for hh in range(HB):
        ks = slice(DK_B * hh, DK_B * (hh + 1))
        q = rope(qb_ref[:, ks].astype(F32)) * (DK_B ** -0.5)
        k = rope(kb_ref[:, ks].astype(F32))
        v = vb_ref[:, DV_B * hh:DV_B * (hh + 1)]
        self_w = jnp.sum(q * k, axis=-1, keepdims=True)
        e = e_lvls[0][:, ks]
        cross = _dot_nt((q[q_rows] * e[q_rows]).astype(BF16), (k[k_rows] * e[k_rows]).astype(BF16))
        diag = [jnp.where(on_diag, self_w[c * half:(c + 1) * half], 0.0) for c in range(2)]
        for lvl in range(1, GLA_LEVELS):
            s = BLK >> (lvl + 1)
            e = e_lvls[lvl][:, ks]
            if s >= 8:
                parts = []
                for i in range(BLK // s):
                    rs = slice(i * s, (i + 1) * s)
                    src = q if ((i % 2 == 1) != rev) else k
                    parts.append(src[rs] * e[rs])
                x = jnp.concatenate(parts, axis=0)
            else:
                second = ((row // s) & 1) == 1
                x = (jnp.where(second, k, q) if rev else jnp.where(second, q, k)) * e
            xb = x.astype(BF16)
            for c in range(2):
                xc = xb[c * half:(c + 1) * half]
                diag[c] = jnp.where(at_level[lvl], _dot_nt(xc, xc), diag[c])
        zero = jnp.zeros((half, half), F32)
        if rev:
            att = jnp.concatenate([jnp.concatenate([diag[0], cross], axis=1),
                                   jnp.concatenate([zero, diag[1]], axis=1)], axis=0)
        else:
            att = jnp.concatenate([jnp.concatenate([diag[0], zero], axis=1),
                                   jnp.concatenate([cross, diag[1]], axis=1)], axis=0)
        e_q = e_in[:, ks]
        st = st_ref[hh]
        o = _dot_nt((q * e_q).astype(BF16), st.astype(BF16)) + _dot(att.astype(BF16), v)
        outs.append(o.astype(BF16))
        carry = e_q[0:1, :] if rev else e_q[BLK - 1:BLK, :]
        kt = (k * e_out[:, ks]).astype(BF16)
        st_ref[hh] = st * carry + _dot_tn(v, kt)
    o_ref[...] = jnp.concatenate(outs, axis=-1)


def _gla_kernel(*refs):
    st_ref = refs[-1]
    d = pl.program_id(1)

    @pl.when(pl.program_id(2) == 0)
    def _():
        st_ref[...] = jnp.zeros_like(st_ref)

    @pl.when(d == 0)
    def _():
        _gla_body(False, *refs)

    @pl.when(d == 1)
    def _():
        _gla_body(True, *refs)


def _gla_call(P, wd_pad, bd, cos_t, sa_t, sb_t, mstack, lev_ids):
    bsz = P.shape[0]

    def blk(dd, s):
        return jnp.where(s == 0, NLAT, jnp.where(dd == 0, s - 1, NLAT - s))

    return pl.pallas_call(
        _gla_kernel,
        grid=(bsz, 2, NBLK),
        in_specs=[
            pl.BlockSpec((None, BLK, KB), lambda b, dd, s: (b, blk(dd, s), C_KB // KB)),
            pl.BlockSpec((None, BLK, KB), lambda b, dd, s: (b, blk(dd, s), C_QB // KB)),
            pl.BlockSpec((None, BLK, VB), lambda b, dd, s: (b, blk(dd, s), C_VB // VB)),
            pl.BlockSpec((None, BLK, 128), lambda b, dd, s: (b, blk(dd, s), C_DEC // 128)),
            pl.BlockSpec((None, 128, KB), lambda b, dd, s: (dd, 0, 0)),
            pl.BlockSpec((None, 1, KB), lambda b, dd, s: (dd, 0, 0)),
            pl.BlockSpec((BLK, DK_B), lambda b, dd, s: (blk(dd, s), 0)),
            pl.BlockSpec((BLK, DK_B), lambda b, dd, s: (blk(dd, s), 0)),
            pl.BlockSpec((BLK, DK_B), lambda b, dd, s: (blk(dd, s), 0)),
            pl.BlockSpec((None, (GLA_LEVELS + 2) * BLK, BLK), lambda b, dd, s: (dd, 0, 0)),
            pl.BlockSpec((None, BLK // 2, BLK // 2), lambda b, dd, s: (dd, 0, 0)),
        ],
        out_specs=pl.BlockSpec((None, None, BLK, VB), lambda b, dd, s: (dd, b, blk(dd, s), 0)),
        out_shape=jax.ShapeDtypeStruct((2, bsz, TT, VB), BF16),
        scratch_shapes=[pltpu.VMEM((HB, DV_B, DK_B), F32)],
        compiler_params=_cp(3), name="gla",
    )(P, P, P, P, wd_pad, bd, cos_t, sa_t, sb_t, mstack, lev_ids)


def _rope_tables():
    t = jnp.arange(T)
    half = DK_B // 2
    inv = ROPE_BASE ** (-jnp.arange(0, half, 2, dtype=F32) / half)

    def tab(pos):
        ang = pos.astype(F32)[:, None] * inv[None, :]
        return jnp.concatenate([ang, ang], axis=-1)

    ang = jnp.concatenate([tab(t // GRID_W), tab(t % GRID_W)], axis=-1)
    cos, sin = jnp.cos(ang), jnp.sin(ang)
    first = (np.arange(DK_B) % half) < (half // 2)
    sa = jnp.where(first[None, :], -sin, 0.0)
    sb = jnp.where(first[None, :], 0.0, sin)
    ident = jnp.ones((L, DK_B), F32)
    zero = jnp.zeros((L, DK_B), F32)
    return (jnp.concatenate([cos, ident]), jnp.concatenate([sa, zero]), jnp.concatenate([sb, zero]))


def _gelu(x):
    return 0.5 * x * (1.0 + jnp.tanh(0.7978845608028654 * (x + 0.044715 * (x * x * x))))


def _gmlp_kernel(u_ref, v_ref, g_ref, b_ref, w_ref, bs_ref, o_ref):
    for ch in range(BLK // MIX_CHUNK):
        rs = slice(ch * MIX_CHUNK, (ch + 1) * MIX_CHUNK)
        u = _gelu(u_ref[rs, :].astype(F32))
        v = _gelu(v_ref[rs, :].astype(F32))
        for g in range(GC):
            cs = slice(g * CG, (g + 1) * CG)
            vg = v[:, cs]
            mu = jnp.mean(vg, axis=-1, keepdims=True)
            var = jnp.mean(jnp.square(vg - mu), axis=-1, keepdims=True)
            vn = (vg - mu) * lax.rsqrt(var + EPS) * g_ref[:, cs] + b_ref[:, cs]
            s = _dot(w_ref[g].astype(BF16), vn.astype(BF16)) + bs_ref[g]
            o_ref[rs, cs] = (u[:, cs] * s).astype(BF16)


def _gmlp_call(P, ln_g, ln_b, w_sp, bs_rep, n_blocks):
    bsz = P.shape[0]
    return pl.pallas_call(
        _gmlp_kernel,
        grid=(bsz, n_blocks),
        in_specs=[pl.BlockSpec((None, BLK, WC), lambda b, j: (b, j, C_UC // WC)),
                  pl.BlockSpec((None, BLK, WC), lambda b, j: (b, j, C_VC // WC)),
                  pl.BlockSpec((1, WC), lambda b, j: (0, 0)),
                  pl.BlockSpec((1, WC), lambda b, j: (0, 0)),
                  pl.BlockSpec((GC, MIX_CHUNK, MIX_CHUNK), lambda b, j: (0, 0, 0)),
                  pl.BlockSpec((GC, MIX_CHUNK, 128), lambda b, j: (0, 0, 0))],
        out_specs=pl.BlockSpec((None, BLK, WC), lambda b, j: (b, j, 0)),
        out_shape=jax.ShapeDtypeStruct((bsz, n_blocks * BLK, WC), BF16),
        compiler_params=_cp(2), name="gmlp",
    )(P, P, ln_g.reshape(1, WC), ln_b.reshape(1, WC), w_sp, bs_rep)


def _out_kernel(*refs, split):
    oa_ref, of_ref, ob_ref, gb_ref, oc_ref, w_ref = refs[:6]
    if split:
        x_in = _stream_block(refs[6], refs[7])
    else:
        x_in = refs[6][...]
    gg_ref, gpost_ref, gpre_ref, mod_ref, wr_ref, xo_ref, h2_ref, lg_ref = refs[7 + split:]
    o = of_ref[...].astype(F32) + ob_ref[...].astype(F32)
    gb = gb_ref[...].astype(F32)
    parts = []
    for h in range(HB):
        cs = slice(h * DV_B, (h + 1) * DV_B)
        gh = gb[:, cs]
        parts.append((_rms(o[:, cs]) * gg_ref[:, cs] * (gh * _sigmoid(gh))).astype(BF16))
    mixed = jnp.concatenate([oa_ref[...]] + parts + [oc_ref[...]], axis=-1)
    y = _dot(mixed, w_ref[...])
    x1 = x_in + mod_ref[2:3, :] * (_rms(y) * gpost_ref[...])
    xo_ref[...] = x1
    h2 = ((_rms(x1) * gpre_ref[...]) * (1.0 + mod_ref[4:5, :]) + mod_ref[3:4, :]).astype(BF16)
    h2_ref[...] = h2
    wh, wl = _split_bf16(wr_ref[...])
    lg_ref[...] = _dot(h2, wh) + _dot(h2, wl)


def _out_call(o_a, o_g, P, o_c, w_out_b, stream, g_gla, g_post, g_pre, modtab, wr_pad, n_blocks):
    bsz = P.shape[0]
    rows = n_blocks * BLK
    split = isinstance(stream, tuple)
    rowspec = lambda w: pl.BlockSpec((None, BLK, w), lambda b, j: (b, j, 0))
    vec = lambda w: pl.BlockSpec((1, w), lambda b, j: (0, 0))
    return pl.pallas_call(
        functools.partial(_out_kernel, split=split),
        grid=(bsz, n_blocks),
        in_specs=[rowspec(WA),
                  pl.BlockSpec((None, None, BLK, VB), lambda b, j: (0, b, j, 0)),
                  pl.BlockSpec((None, None, BLK, VB), lambda b, j: (1, b, j, 0)),
                  pl.BlockSpec((None, BLK, VB), lambda b, j: (b, j, C_GB // VB)),
                  rowspec(WC),
                  pl.BlockSpec((D, D), lambda b, j: (0, 0))]
                 + (_stream_specs() if split else [rowspec(D)])
                 + [vec(VB), vec(D), vec(D),
                    pl.BlockSpec((None, None, 8, D), lambda b, j: (b, j // NLAT, 0, 0)),
                    pl.BlockSpec((D, 128), lambda b, j: (0, 0))],
        out_specs=[rowspec(D), rowspec(D), rowspec(128)],
        out_shape=[jax.ShapeDtypeStruct((bsz, rows, D), F32),
                   jax.ShapeDtypeStruct((bsz, rows, D), BF16),
                   jax.ShapeDtypeStruct((bsz, rows, 128), F32)],
        compiler_params=_cp(2), name="proj_out",
    )(o_a, o_g, o_g, P, o_c, w_out_b, *(stream if split else (stream,)), g_gla.reshape(1, VB),
      g_post.reshape(1, D), g_pre.reshape(1, D), modtab, wr_pad)


def _route_set(lg, cap, upper):
    n = lg.shape[0]
    lt = lg.T[:N_EXPERTS, :]
    ex = jnp.exp(lt - lt.max(axis=0, keepdims=True))
    aff = ex / ex.sum(axis=0, keepdims=True)
    bits = pltpu.bitcast(aff, jnp.int32)
    capf = jnp.float32(cap)

    def body(i, prefix):
        cand = prefix | jnp.left_shift(jnp.int32(1), 30 - i)
        cnt = jnp.sum(jnp.where(bits >= cand, 1.0, 0.0), axis=1, keepdims=True)
        return jnp.where(cnt >= capf, cand, prefix)

    thr = lax.fori_loop(0, 31, body, jnp.zeros((N_EXPERTS, 1), jnp.int32))
    gt = jnp.where(bits > thr, 1.0, 0.0)
    eq = jnp.where(bits == thr, 1.0, 0.0)
    need = capf - gt.sum(axis=1, keepdims=True)
    rank_eq = _dot(eq.astype(BF16), upper)
    sel = gt + eq * jnp.where(rank_eq < need, 1.0, 0.0)
    slot = _dot(sel.astype(BF16), upper)
    slot = jnp.where(sel > 0.5, slot, -1.0)
    pad = jnp.full((128 - N_EXPERTS, n), -1.0, F32)
    slot_t = jnp.concatenate([slot, pad], axis=0).T
    return slot, aff, slot_t


def _route_kernel(lg_ref, u_ref, *out_refs, with_ctx):
    sl, af, st = _route_set(lg_ref[0:T, :], CAP_LAT, u_ref[...])
    out_refs[0][...] = sl
    out_refs[1][...] = af
    out_refs[2][0:T, :] = st
    if with_ctx:
        sl, af, st = _route_set(lg_ref[T:TT, :], CAP_CTX, u_ref[0:L, 0:L])
        out_refs[3][...] = sl
        out_refs[4][...] = af
        out_refs[2][T:TT, :] = st


def _route_call(lg, upper, with_ctx):
    bsz, rows, _ = lg.shape
    en = lambda n: pl.BlockSpec((None, N_EXPERTS, n), lambda b: (b, 0, 0))
    out_specs = [en(T), en(T), pl.BlockSpec((None, rows, 128), lambda b: (b, 0, 0))]
    out_shape = [jax.ShapeDtypeStruct((bsz, N_EXPERTS, T), F32),
                 jax.ShapeDtypeStruct((bsz, N_EXPERTS, T), F32),
                 jax.ShapeDtypeStruct((bsz, rows, 128), F32)]
    if with_ctx:
        out_specs += [en(L), en(L)]
        out_shape += [jax.ShapeDtypeStruct((bsz, N_EXPERTS, L), F32)] * 2
    return pl.pallas_call(
        functools.partial(_route_kernel, with_ctx=with_ctx),
        grid=(bsz,),
        in_specs=[pl.BlockSpec((None, rows, 128), lambda b: (b, 0, 0)),
                  pl.BlockSpec((T, T), lambda b: (0, 0))],
        out_specs=out_specs, out_shape=out_shape,
        compiler_params=_cp(1), name="route",
    )(lg, upper)


def _gather_kernel(*refs, n_sets):
    e = pl.program_id(1)
    ins, outs = refs[:3 * n_sets], refs[3 * n_sets:]
    for k in range(n_sets):
        slot_ref, aff_ref, h_ref = ins[3 * k:3 * k + 3]
        xs_ref, g_ref = outs[2 * k:2 * k + 2]
        cap, n = xs_ref.shape[0], h_ref.shape[0]
        srow = slot_ref[pl.ds(e, 1), :]
        arow = aff_ref[pl.ds(e, 1), :]
        sid = lax.broadcasted_iota(jnp.int32, (cap, n), 0).astype(F32)
        hit = sid == srow
        xs_ref[...] = _dot(jnp.where(hit, 1.0, 0.0).astype(BF16), h_ref[...]).astype(BF16)
        g = jnp.sum(jnp.where(hit, arow, 0.0), axis=1, keepdims=True)
        g_ref[...] = jnp.broadcast_to(g, (cap, 128))


def _gather_call(h2, sets):
    bsz = h2.shape[0]
    in_specs, args, out_specs, out_shape = [], [], [], []
    for slot, aff, n, cap, blk_idx in sets:
        in_specs += [pl.BlockSpec((None, N_EXPERTS, n), lambda b, e: (b, 0, 0)),
                     pl.BlockSpec((None, N_EXPERTS, n), lambda b, e: (b, 0, 0)),
                     pl.BlockSpec((None, n, D), lambda b, e, i=blk_idx: (b, i, 0))]
        args += [slot, aff, h2]
        out_specs += [pl.BlockSpec((None, cap, D), lambda b, e: (e, b, 0)),
                      pl.BlockSpec((None, cap, 128), lambda b, e: (e, b, 0))]
        out_shape += [jax.ShapeDtypeStruct((N_EXPERTS, bsz * cap, D), BF16),
                      jax.ShapeDtypeStruct((N_EXPERTS, bsz * cap, 128), F32)]
    return pl.pallas_call(
        functools.partial(_gather_kernel, n_sets=len(sets)),
        grid=(bsz, N_EXPERTS),
        in_specs=in_specs, out_specs=out_specs, out_shape=out_shape,
        compiler_params=_cp(2), name="gather",
    )(*args)


def _ffn_kernel(*refs, rl, rc, nf):
    if rc:
        xl_ref, xc_ref, gl_ref, gc_ref, wg_ref, wu_ref, wd_ref, yl_ref, yc_ref, hid = refs
        groups = ((xl_ref, gl_ref, yl_ref, 0, rl), (xc_ref, gc_ref, yc_ref, rl, rc))
    else:
        xl_ref, gl_ref, wg_ref, wu_ref, wd_ref, yl_ref, hid = refs
        groups = ((xl_ref, gl_ref, yl_ref, 0, rl),)
    s = pl.program_id(1)

    @pl.when(s < nf)
    def _():
        wg = wg_ref[...].astype(BF16)
        wu = wu_ref[...].astype(BF16)
        for x_ref, _, _, r0, nr in groups:
            xv = x_ref[...]
            a = _dot(xv, wg)
            u = _dot(xv, wu)
            hid[s, r0:r0 + nr, :] = ((a * _sigmoid(a)) * u).astype(BF16)

    @pl.when(s >= nf)
    def _():
        wd = wd_ref[...].astype(BF16)
        for _, g_ref, y_ref, r0, nr in groups:
            hm = jnp.concatenate([hid[k, r0:r0 + nr, :] for k in range(nf)], axis=1)
            y = _dot(hm, wd)
            for c in range(FF_TILE // 128):
                cs = slice(128 * c, 128 * (c + 1))
                y_ref[:, cs] = (y[:, cs] * g_ref[...]).astype(BF16)


def _ffn_call(layer, xs_l, g_l, xs_c, g_c, w_gate, w_up, w_down):
    rl = xs_l.shape[1]
    rc = 0 if xs_c is None else xs_c.shape[1]
    nf = EXPERT_FF // FF_TILE
    nd = D // FF_TILE
    up = lambda e, s: (layer, e, 0, jnp.minimum(s, nf - 1))
    down = lambda s: jnp.maximum(s - nf, 0)
    in_specs = [pl.BlockSpec((None, rl, D), lambda e, s: (e, 0, 0))]
    args = [xs_l]
    if rc:
        in_specs.append(pl.BlockSpec((None, rc, D), lambda e, s: (e, 0, 0)))
        args.append(xs_c)
    in_specs.append(pl.BlockSpec((None, rl, 128), lambda e, s: (e, 0, 0)))
    args.append(g_l)
    if rc:
        in_specs.append(pl.BlockSpec((None, rc, 128), lambda e, s: (e, 0, 0)))
        args.append(g_c)
    in_specs += [pl.BlockSpec((None, None, D, FF_TILE), up),
                 pl.BlockSpec((None, None, D, FF_TILE), up),
                 pl.BlockSpec((None, None, EXPERT_FF, FF_TILE), lambda e, s: (layer, e, 0, down(s)))]
    args += [w_gate, w_up, w_down]
    out_specs = [pl.BlockSpec((None, rl, FF_TILE), lambda e, s: (e, 0, down(s)))]
    out_shape = [jax.ShapeDtypeStruct((N_EXPERTS, rl, D), BF16)]
    if rc:
        out_specs.append(pl.BlockSpec((None, rc, FF_TILE), lambda e, s: (e, 0, down(s))))
        out_shape.append(jax.ShapeDtypeStruct((N_EXPERTS, rc, D), BF16))
    return pl.pallas_call(
        functools.partial(_ffn_kernel, rl=rl, rc=rc, nf=nf),
        grid=(N_EXPERTS, nf + nd),
        in_specs=in_specs, out_specs=out_specs, out_shape=out_shape,
        scratch_shapes=[pltpu.VMEM((nf, rl + rc, FF_TILE), BF16)],
        compiler_params=_cp(2), name="expert_ffn",
    )(*args)


def _scatter_kernel(*refs, ne, with_ctx, has_next):
    refs = list(refs)
    slot_ref, yl_ref = refs[:2]
    refs = refs[2:]
    yc_ref = refs.pop(0) if with_ctx else None
    x_ref, gpost_ref, mod_ref = refs[:3]
    refs = refs[3:]
    if has_next:
        gpre_ref, modn_ref = refs[:2]
        refs = refs[2:]
    xo_ref = refs[0]
    hn_ref = refs[1] if has_next else None
    acc = refs[-1]
    eg = pl.program_id(1)
    r = pl.program_id(2)
    ng = pl.num_programs(1)

    def add(cap, y_ref):
        kk = ne * cap
        er = lax.broadcasted_iota(jnp.int32, (128, kk), 0)
        ec = lax.broadcasted_iota(jnp.int32, (128, kk), 1)
        expand = jnp.where(er == eg * ne + ec // cap, 1.0, 0.0).astype(BF16)
        spread = _dot(slot_ref[...].astype(BF16), expand)
        want = (lax.broadcasted_iota(jnp.int32, (BLK, kk), 1) % cap).astype(F32)
        pt = jnp.where(spread == want, 1.0, 0.0).astype(BF16)
        acc[r] += _dot(pt, y_ref[...].reshape(kk, D))

    @pl.when(eg == 0)
    def _():
        acc[r] = jnp.zeros((BLK, D), F32)

    @pl.when(r < NLAT)
    def _():
        add(CAP_LAT, yl_ref)

    if with_ctx:
        @pl.when(r == NLAT)
        def _():
            add(CAP_CTX, yc_ref)

    @pl.when(eg == ng - 1)
    def _():
        x2 = x_ref[...] + mod_ref[5:6, :] * (_rms(acc[r]) * gpost_ref[...])
        xo_ref[...] = x2
        if has_next:
            hn = (_rms(x2) * gpre_ref[...]) * (1.0 + modn_ref[1:2, :]) + modn_ref[0:1, :]
            hn_ref[...] = hn.astype(BF16)


def _scatter_call(slot_t, y_l, y_c, x_mid, g_post, modtab, g_pre_next, modtab_next):
    bsz = x_mid.shape[0]
    with_ctx = y_c is not None
    has_next = g_pre_next is not None
    n_tiles = NBLK if with_ctx else NLAT
    ne = 8
    ng = N_EXPERTS // ne
    vec = pl.BlockSpec((1, D), lambda b, g, r: (0, 0))
    modspec = pl.BlockSpec((None, None, 8, D), lambda b, g, r: (b, r // NLAT, 0, 0))
    late = lambda b, g, r: (b, jnp.where(g == ng - 1, r, 0), 0)
    in_specs = [pl.BlockSpec((None, BLK, 128), lambda b, g, r: (b, r, 0)),
                pl.BlockSpec((ne, CAP_LAT, D), lambda b, g, r: (g, b, 0))]
    args = [slot_t, y_l]
    if with_ctx:
        in_specs.append(pl.BlockSpec((ne, CAP_CTX, D), lambda b, g, r: (g, b, 0)))
        args.append(y_c)
    in_specs += [pl.BlockSpec((None, BLK, D), late), vec, modspec]
    args += [x_mid, g_post.reshape(1, D), modtab]
    if has_next:
        in_specs += [vec, modspec]
        args += [g_pre_next.reshape(1, D), modtab_next]
    out_specs = [pl.BlockSpec((None, BLK, D), late)]
    out_shape = [jax.ShapeDtypeStruct((bsz, n_tiles * BLK, D), F32)]
    if has_next:
        out_specs.append(pl.BlockSpec((None, BLK, D), late))
        out_shape.append(jax.ShapeDtypeStruct((bsz, n_tiles * BLK, D), BF16))
    return pl.pallas_call(
        functools.partial(_scatter_kernel, ne=ne, with_ctx=with_ctx, has_next=has_next),
        grid=(bsz, ng, n_tiles),
        in_specs=in_specs, out_specs=out_specs, out_shape=out_shape,
        scratch_shapes=[pltpu.VMEM((n_tiles, BLK, D), F32)],
        compiler_params=_cp(3), name="scatter",
    )(*args)


def _pack_w_in(w):
    qa, ka, va, kb, vb, dec, qb, gb, uc, vc = jnp.split(
        w, [int(s) for s in np.cumsum((WA, WA, WA, KB, VB, 2 * DEC_RANK, KB, VB, WC))], axis=-1)
    pad = jnp.zeros((D, NP - C_DEC - 2 * DEC_RANK), w.dtype)
    return jnp.concatenate([vb, gb, qa, ka, va, kb, qb, uc, vc, dec, pad], axis=-1).astype(BF16)


def kernel(x, c, ctx, c_ctx, w_ada, b_ada, g_pre_mix, g_post_mix, g_pre_ffn, g_post_ffn, w_in, w_dec,
           b_dec, rpb, g_gla, ln_v_g, ln_v_b, w_sp, b_sp, w_out, w_router, w_gate, w_up, w_down):
    bsz = x.shape[0]
    c_all = jnp.zeros((16, D), F32).at[:bsz].set(c).at[8].set(c_ctx)
    mods = _ada_call(c_all, w_ada, b_ada).reshape(DEPTH, 16, 6, D)
    lat = mods[:, :bsz]
    cx = jnp.broadcast_to(mods[:, 8:9], lat.shape)
    modtab = jnp.pad(jnp.stack([lat, cx], axis=2), ((0, 0), (0, 0), (0, 0), (0, 2), (0, 0)))

    cos_t, sa_t, sb_t = _rope_tables()
    mstack = jnp.asarray(_gla_matrices(), BF16)
    lev_ids = jnp.asarray(_gla_level_ids())
    hmask = jnp.asarray(np.stack([(np.arange(128) < DH_A), (np.arange(128) >= DH_A)]) * DH_A ** -0.5, F32)
    upper = (lax.broadcasted_iota(jnp.int32, (T, T), 0) < lax.broadcasted_iota(jnp.int32, (T, T), 1)).astype(BF16)

    stream = (x, ctx)
    h = _prenorm_call(x, ctx, g_pre_mix[0], modtab[0])
    for l in range(DEPTH):
        last = l == DEPTH - 1
        nb = NLAT if last else NBLK
        P = _proj_call(h.reshape(bsz * TT, D), _pack_w_in(w_in[l])).reshape(bsz, TT, NP)
        o_a = _na_call(P, _na_bias(rpb[l]), hmask, nb)
        wd_pad = jnp.zeros((2, 128, KB), F32)
        wd_pad = wd_pad.at[0, :DEC_RANK].set(w_dec[l, 0]).at[1, DEC_RANK:2 * DEC_RANK].set(w_dec[l, 1])
        o_g = _gla_call(P, wd_pad, b_dec[l].reshape(2, 1, KB), cos_t, sa_t, sb_t, mstack, lev_ids)
        bs_rep = jnp.broadcast_to(b_sp[l][:, :, None], (GC, MIX_CHUNK, 128))
        o_c = _gmlp_call(P, ln_v_g[l], ln_v_b[l], w_sp[l], bs_rep, nb)
        wr_pad = jnp.pad(w_router[l], ((0, 0), (0, 128 - N_EXPERTS)))
        x_mid, h2, lg = _out_call(o_a, o_g, P, o_c, w_out[l].astype(BF16), stream, g_gla[l], g_post_mix[l],
                                  g_pre_ffn[l], modtab[l], wr_pad, nb)
        routed = _route_call(lg, upper, not last)
        slot_l, aff_l, slot_t = routed[:3]
        if last:
            xs_l, gt_l = _gather_call(h2, [(slot_l, aff_l, T, CAP_LAT, 0)])
            (y_l,) = _ffn_call(l, xs_l, gt_l, None, None, w_gate, w_up, w_down)
            (stream,) = _scatter_call(slot_t, y_l, None, x_mid, g_post_ffn[l], modtab[l], None, None)
        else:
            slot_c, aff_c = routed[3:]
            xs_l, gt_l, xs_c, gt_c = _gather_call(h2, [(slot_l, aff_l, T, CAP_LAT, 0),
                                                       (slot_c, aff_c, L, CAP_CTX, NLAT)])
            y_l, y_c = _ffn_call(l, xs_l, gt_l, xs_c, gt_c, w_gate, w_up, w_down)
            stream, h = _scatter_call(slot_t, y_l, y_c, x_mid, g_post_ffn[l], modtab[l], g_pre_mix[l + 1],
                                      modtab[l + 1])
    return stream
```

```python
import functools

import numpy as np
import jax
import jax.numpy as jnp
from jax import lax
from jax.experimental import pallas as pl
from jax.experimental.pallas import tpu as pltpu

F32 = jnp.float32
BF16 = jnp.bfloat16

D = 2048
T = 2048
L = 256
TT = T + L
BLK = 256
NBLK = TT // BLK
NLAT = T // BLK
DEPTH = 2
GRID_W = 64
ROWS = T // GRID_W
HA, DH_A = 8, 64
WIN_ROWS, WIN_COLS = 8, 16
HB, DK_B, DV_B = 4, 128, 256
DEC_RANK = 16
GATE_TAU = 16.0
GC, CG, MIX_CHUNK = 4, 128, 128
N_EXPERTS = 16
EXPERT_FF = 2048
CAP_FACTOR = 2
ROPE_BASE = 10000.0
EPS = 1e-6
LOG2E = 1.4426950408889634
WA, KB, VB, WC = HA * DH_A, HB * DK_B, HB * DV_B, GC * CG

NP = 3 * D
C_VB, C_GB, C_QA, C_KA, C_VA, C_KB, C_QB, C_UC, C_VC, C_DEC = (
    0, 1024, 2048, 2560, 3072, 3584, 4096, 4608, 5120, 5632)

NA_QROWS = 4
NA_KROWS = 12
NA_KEYS = NA_KROWS * GRID_W
NEG = -1e30

GLA_LEVELS = 8
GLA_FINE = (5, 6, 7)
FF_TILE = 256
GATHER_EXPERTS = 2
CAP_LAT = CAP_FACTOR * T // N_EXPERTS
CAP_CTX = CAP_FACTOR * L // N_EXPERTS
VMEM_LIMIT = 56 * 1024 * 1024


def _cp(n_axes):
    return pltpu.CompilerParams(dimension_semantics=("arbitrary",) * n_axes,
                                vmem_limit_bytes=VMEM_LIMIT)


def _dot(a, b):
    return jnp.dot(a, b, preferred_element_type=F32)


def _dot_nt(a, b):
    return lax.dot_general(a, b, (((1,), (1,)), ((), ())), preferred_element_type=F32)


def _dot_tn(a, b):
    return lax.dot_general(a, b, (((0,), (0,)), ((), ())), preferred_element_type=F32)


def _rms(x):
    return x * lax.rsqrt(jnp.mean(x * x, axis=-1, keepdims=True) + EPS)


def _sigmoid(x):
    return 1.0 / (1.0 + jnp.exp(-x))


def _split_bf16(x):
    hi = x.astype(BF16)
    lo = (x - hi.astype(F32)).astype(BF16)
    return hi, lo


def _ada_kernel(c_ref, w_ref, b_ref, o_ref):
    cv = c_ref[...]
    hi, lo = _split_bf16(cv * _sigmoid(cv))
    r = _dot(jnp.concatenate([hi, lo], axis=0), w_ref[...].astype(BF16))
    o_ref[...] = r[:16] + r[16:] + b_ref[...]


def _ada_call(c_all, w_ada, b_ada):
    tn = 1024
    n6 = w_ada.shape[-1]
    return pl.pallas_call(
        _ada_kernel,
        grid=(DEPTH, n6 // tn),
        in_specs=[pl.BlockSpec((16, D), lambda l, n: (0, 0)),
                  pl.BlockSpec((None, D, tn), lambda l, n: (l, 0, n)),
                  pl.BlockSpec((None, 1, tn), lambda l, n: (l, 0, n))],
        out_specs=pl.BlockSpec((None, 16, tn), lambda l, n: (l, 0, n)),
        out_shape=jax.ShapeDtypeStruct((DEPTH, 16, n6), F32),
        compiler_params=_cp(2), name="ada",
    )(c_all, w_ada, b_ada.reshape(DEPTH, 1, n6))


def _stream_block(x_ref, c_ref):
    return jnp.where(pl.program_id(1) < NLAT, x_ref[...], c_ref[...])


def _stream_specs():
    return [pl.BlockSpec((None, BLK, D), lambda b, j: (b, jnp.minimum(j, NLAT - 1), 0)),
            pl.BlockSpec((None, BLK, D), lambda b, j: (b, 0, 0))]


def _prenorm_kernel(x_ref, c_ref, g_ref, mod_ref, h_ref):
    y = _rms(_stream_block(x_ref, c_ref)) * g_ref[...]
    h_ref[...] = (y * (1.0 + mod_ref[1:2, :]) + mod_ref[0:1, :]).astype(BF16)


def _prenorm_call(x, ctx, g, modtab):
    bsz = x.shape[0]
    return pl.pallas_call(
        _prenorm_kernel,
        grid=(bsz, NBLK),
        in_specs=_stream_specs() + [
            pl.BlockSpec((1, D), lambda b, j: (0, 0)),
            pl.BlockSpec((None, None, 8, D), lambda b, j: (b, j // NLAT, 0, 0))],
        out_specs=pl.BlockSpec((None, BLK, D), lambda b, j: (b, j, 0)),
        out_shape=jax.ShapeDtypeStruct((bsz, TT, D), BF16),
        compiler_params=_cp(2), name="prenorm",
    )(x, ctx, g.reshape(1, D), modtab)


def _proj_kernel(h_ref, w_ref, o_ref):
    o_ref[...] = _dot(h_ref[...], w_ref[...]).astype(BF16)


def _proj_call(h2d, w_pack):
    m = h2d.shape[0]
    tm = 1024 if m % 1024 == 0 else 768
    tn = NP // 3
    return pl.pallas_call(
        _proj_kernel,
        grid=(NP // tn, m // tm),
        in_specs=[pl.BlockSpec((tm, D), lambda n, i: (i, 0)),
                  pl.BlockSpec((D, tn), lambda n, i: (0, n))],
        out_specs=pl.BlockSpec((tm, tn), lambda n, i: (i, n)),
        out_shape=jax.ShapeDtypeStruct((m, NP), BF16),
        compiler_params=_cp(2), name="proj_in",
    )(h2d, w_pack)


def _softmax_pv(s_list, v_list):
    m = s_list[0].max(axis=-1, keepdims=True)
    for s in s_list[1:]:
        m = jnp.maximum(m, s.max(axis=-1, keepdims=True))
    acc = None
    for s, v in zip(s_list, v_list):
        o = _dot(jnp.exp(s - m).astype(BF16), v)
        acc = o if acc is None else acc + o
    return acc[:, :128] / acc[:, 128:]


def _na_kernel(q_ref, k_ref, v_ref, bias_ref, hm_ref, *mlp_refs_o_ref):
    mlp_in, o_ref, oc_ref = mlp_refs_o_ref[:-2], mlp_refs_o_ref[-2], mlp_refs_o_ref[-1]
    j = pl.program_id(1)
    lane = lax.broadcasted_iota(jnp.int32, (BLK, 128), 1)
    low = lane < DH_A

    def run(local_start):
        _gmlp_body(*mlp_in, oc_ref)
        for p in range(HA // 2):
            sl = slice(128 * p, 128 * p + 128)
            q2 = q_ref[:, sl]
            kc = k_ref[T:TT, sl]
            vc = jnp.concatenate([v_ref[T:TT, sl], jnp.ones((L, 128), BF16)], axis=1)
            if local_start is not None:
                kl = k_ref[pl.ds(local_start, NA_KEYS), sl]
                vl = jnp.concatenate([v_ref[pl.ds(local_start, NA_KEYS), sl],
                                      jnp.ones((NA_KEYS, 128), BF16)], axis=1)
            pair = []
            for hh in range(2):
                qm = (q2.astype(F32) * hm_ref[hh:hh + 1, :]).astype(BF16)
                s_ctx = _dot_nt(qm, kc)
                if local_start is not None:
                    s_loc = _dot_nt(qm, kl) + bias_ref[2 * p + hh]
                    pair.append(_softmax_pv([s_loc, s_ctx], [vl, vc]))
                else:
                    pair.append(_softmax_pv([s_ctx], [vc]))
            o_ref[:, sl] = jnp.where(low, pair[0], pair[1]).astype(BF16)

    @pl.when(j < NLAT)
    def _():
        krow = jnp.clip(j * NA_QROWS - WIN_ROWS // 2, 0, ROWS - NA_KROWS)
        run(pl.multiple_of(krow * GRID_W, GRID_W))

    @pl.when(j == NLAT)
    def _():
        run(None)


def _na_gmlp_call(P, bias, hmask, ln_g, ln_b, w_sp, bs_rep, n_blocks):
    bsz = P.shape[0]

    def bias_idx(b, j):
        return (jnp.where(j == 0, 0, jnp.where(j == NLAT - 1, 2, 1)), 0, 0, 0)

    rows = n_blocks * BLK
    return pl.pallas_call(
        _na_kernel,
        grid=(bsz, n_blocks),
        in_specs=[pl.BlockSpec((None, BLK, WA), lambda b, j: (b, j, C_QA // WA)),
                  pl.BlockSpec((None, TT, WA), lambda b, j: (b, 0, C_KA // WA)),
                  pl.BlockSpec((None, TT, WA), lambda b, j: (b, 0, C_VA // WA)),
                  pl.BlockSpec((None, HA, BLK, NA_KEYS), bias_idx),
                  pl.BlockSpec((2, 128), lambda b, j: (0, 0)),
                  pl.BlockSpec((None, BLK, WC), lambda b, j: (b, j, C_UC // WC)),
                  pl.BlockSpec((None, BLK, WC), lambda b, j: (b, j, C_VC // WC)),
                  pl.BlockSpec((1, WC), lambda b, j: (0, 0)),
                  pl.BlockSpec((1, WC), lambda b, j: (0, 0)),
                  pl.BlockSpec((GC, MIX_CHUNK, MIX_CHUNK), lambda b, j: (0, 0, 0)),
                  pl.BlockSpec((GC, MIX_CHUNK, 128), lambda b, j: (0, 0, 0))],
        out_specs=[pl.BlockSpec((None, BLK, WA), lambda b, j: (b, j, 0)),
                   pl.BlockSpec((None, BLK, WC), lambda b, j: (b, j, 0))],
        out_shape=[jax.ShapeDtypeStruct((bsz, rows, WA), BF16),
                   jax.ShapeDtypeStruct((bsz, rows, WC), BF16)],
        compiler_params=_cp(2), name="nbr_attn_gmlp",
    )(P, P, P, bias, hmask, P, P, ln_g.reshape(1, WC), ln_b.reshape(1, WC), w_sp, bs_rep)


def _na_bias(rpb_l):
    n_dr, n_dc = 2 * WIN_ROWS - 1, 2 * WIN_COLS - 1
    cq = np.arange(GRID_W)[:, None]
    ck = np.arange(GRID_W)[None, :]
    cs = np.clip(cq - WIN_COLS // 2, 0, GRID_W - WIN_COLS)
    col_ok = (ck >= cs) & (ck < cs + WIN_COLS)
    dc = np.clip(ck - cq + WIN_COLS - 1, 0, n_dc - 1)
    col_sel = (dc[:, :, None] == np.arange(n_dc)).astype(np.float32)
    colx = jnp.einsum("qke,hde->hdqk", col_sel, rpb_l, precision=lax.Precision.HIGHEST)
    colx = jnp.where(col_ok, colx, NEG).astype(F32)
    return pl.pallas_call(
        _bias_kernel,
        grid=(3, HA),
        in_specs=[pl.BlockSpec((None, n_dr, GRID_W, GRID_W), lambda t, h: (h, 0, 0, 0))],
        out_specs=pl.BlockSpec((None, None, BLK, NA_KEYS), lambda t, h: (t, h, 0, 0)),
        out_shape=jax.ShapeDtypeStruct((3, HA, BLK, NA_KEYS), F32),
        compiler_params=_cp(2), name="na_bias",
    )(colx)


def _bias_kernel(colx_ref, o_ref):
    t = pl.program_id(0)
    neg = jnp.full((GRID_W, GRID_W), NEG, F32)
    for tt, rb in enumerate((0, 1, NLAT - 1)):
        @pl.when(t == tt)
        def _(rb=rb):
            k0 = int(np.clip(rb * NA_QROWS - WIN_ROWS // 2, 0, ROWS - NA_KROWS))
            for qi in range(NA_QROWS):
                r = rb * NA_QROWS + qi
                rs = int(np.clip(r - WIN_ROWS // 2, 0, ROWS - WIN_ROWS))
                for kp in range(NA_KROWS // 2):
                    pair = []
                    for kr in (k0 + 2 * kp, k0 + 2 * kp + 1):
                        inside = rs <= kr < rs + WIN_ROWS
                        pair.append(colx_ref[kr - r + WIN_ROWS - 1] if inside else neg)
                    o_ref[qi * GRID_W:(qi + 1) * GRID_W, kp * 128:(kp + 1) * 128] = (
                        jnp.concatenate(pair, axis=1))


def _gla_matrices():
    n = BLK
    i = np.arange(n)[:, None]
    t = np.arange(n)[None, :]
    out = np.zeros((2, (len(GLA_FINE) + 1) * n, n), np.float32)
    for f, lvl in enumerate(GLA_FINE):
        s = n >> (lvl + 1)
        blk0 = (i // (2 * s)) * (2 * s)
        m = blk0 + s - 1
        fwd = np.where(i > m, (t > m) & (t <= i), (t > i) & (t <= m))
        bwd = np.where(i <= m, (t >= i) & (t <= m), (t > m) & (t < i))
        out[0, f * n:(f + 1) * n] = fwd
        out[1, f * n:(f + 1) * n] = bwd
    out[0, len(GLA_FINE) * n:] = t <= i
    out[1, len(GLA_FINE) * n:] = t >= i
    return out


def _gla_level_ids():
    h = BLK // 2
    i = np.arange(h)[:, None]
    j = np.arange(h)[None, :]
    x = np.maximum(i ^ j, 1)
    lvl = GLA_LEVELS - 1 - np.floor(np.log2(x)).astype(np.int32)
    fwd = np.where(i == j, -1, np.where(i > j, lvl, -2))
    bwd = np.where(i == j, -1, np.where(i < j, lvl, -2))
    return np.stack([fwd, bwd]).astype(np.int32)


def _gla_body(rev, kb_ref, qb_ref, vb_ref, dec_ref, wd_ref, bd_ref, cos_ref, sa_ref, sb_ref,
              m_ref, lev_ref, o_ref, st_ref):
    half = BLK // 2
    z = _dot(dec_ref[...], wd_ref[...].astype(BF16)) + bd_ref[...]
    la = (jnp.minimum(z, 0.0) - jnp.log1p(jnp.exp(-jnp.abs(z)))) * (LOG2E / GATE_TAU)
    la_b = la.astype(BF16)

    def seg_sum(f):
        return _dot(m_ref[f * BLK:(f + 1) * BLK, :], la_b)

    run = seg_sum(len(GLA_FINE))
    e_in = jnp.exp2(run)
    e_out = jnp.exp2((run[0:1] if rev else run[BLK - 1:BLK]) - run)
    e_fine = {lvl: jnp.exp2(seg_sum(f)) for f, lvl in enumerate(GLA_FINE)}

    def slab_decay(i, s, ks):
        base = (i // 2) * 2 * s
        ref = base + s if rev else base + s - 1
        d = run[i * s:(i + 1) * s, ks] - run[ref:ref + 1, ks]
        return jnp.exp2(-d if ((i % 2 == 0) != rev) else d)

    row = lax.broadcasted_iota(jnp.int32, (BLK, DK_B), 0)
    lev = lev_ref[...]
    on_diag = lev == -1
    at_level = {lvl: lev == lvl for lvl in range(1, GLA_LEVELS)}
    cosv, sav, sbv = cos_ref[...], sa_ref[...], sb_ref[...]

    def rope(x):
        return x * cosv + pltpu.roll(x, DK_B - 32, 1) * sav + pltpu.roll(x, 32, 1) * sbv

    qi, ki = (0, 1) if rev else (1, 0)
    q_rows = slice(qi * half, (qi + 1) * half)
    k_rows = slice(ki * half, (ki + 1) * half)

    outs = []
    for hh in range(HB):
        ks = slice(DK_B * hh, DK_B * (hh + 1))
        q = rope(qb_ref[:, ks].astype(F32)) * (DK_B ** -0.5)
        k = rope(kb_ref[:, ks].astype(F32))
        v = vb_ref[:, DV_B * hh:DV_B * (hh + 1)]
        self_w = jnp.sum(q * k, axis=-1, keepdims=True)
        cross = _dot_nt((q[q_rows] * slab_decay(qi, half, ks)).astype(BF16),
                        (k[k_rows] * slab_decay(ki, half, ks)).astype(BF16))
        diag = [jnp.where(on_diag, self_w[c * half:(c + 1) * half], 0.0) for c in range(2)]
        for lvl in range(1, GLA_LEVELS):
            s = BLK >> (lvl + 1)
            if lvl not in GLA_FINE:
                parts = []
                for i in range(BLK // s):
                    src = q if ((i % 2 == 1) != rev) else k
                    parts.append(src[i * s:(i + 1) * s] * slab_decay(i, s, ks))
                x = jnp.concatenate(parts, axis=0)
            else:
                second = ((row // s) & 1) == 1
                x = (jnp.where(second, k, q) if rev else jnp.where(second, q, k)) * e_fine[lvl][:, ks]
            xb = x.astype(BF16)
            for c in range(2):
                xc = xb[c * half:(c + 1) * half]
                diag[c] = jnp.where(at_level[lvl], _dot_nt(xc, xc), diag[c])
        zero = jnp.zeros((half, half), F32)
        if rev:
            att = jnp.concatenate([jnp.concatenate([diag[0], cross], axis=1),
                                   jnp.concatenate([zero, diag[1]], axis=1)], axis=0)
        else:
            att = jnp.concatenate([jnp.concatenate([diag[0], zero], axis=1),
                                   jnp.concatenate([cross, diag[1]], axis=1)], axis=0)
        e_q = e_in[:, ks]
        st = st_ref[hh]
        o = _dot_nt((q * e_q).astype(BF16), st.astype(BF16)) + _dot(att.astype(BF16), v)
        outs.append(o.astype(BF16))
        carry = e_q[0:1, :] if rev else e_q[BLK - 1:BLK, :]
        kt = (k * e_out[:, ks]).astype(BF16)
        st_ref[hh] = st * carry + _dot_tn(v, kt)
    o_ref[...] = jnp.concatenate(outs, axis=-1)


def _gla_kernel(*refs):
    st_ref = refs[-1]
    d = pl.program_id(1)

    @pl.when(pl.program_id(2) == 0)
    def _():
        st_ref[...] = jnp.zeros_like(st_ref)

    @pl.when(d == 0)
    def _():
        _gla_body(False, *refs)

    @pl.when(d == 1)
    def _():
        _gla_body(True, *refs)


def _gla_call(P, wd_pad, bd, cos_t, sa_t, sb_t, mstack, lev_ids):
    bsz = P.shape[0]

    def blk(dd, s):
        return jnp.where(s == 0, NLAT, jnp.where(dd == 0, s - 1, NLAT - s))

    return pl.pallas_call(
        _gla_kernel,
        grid=(bsz, 2, NBLK),
        in_specs=[
            pl.BlockSpec((None, BLK, KB), lambda b, dd, s: (b, blk(dd, s), C_KB // KB)),
            pl.BlockSpec((None, BLK, KB), lambda b, dd, s: (b, blk(dd, s), C_QB // KB)),
            pl.BlockSpec((None, BLK, VB), lambda b, dd, s: (b, blk(dd, s), C_VB // VB)),
            pl.BlockSpec((None, BLK, 128), lambda b, dd, s: (b, blk(dd, s), C_DEC // 128)),
            pl.BlockSpec((None, 128, KB), lambda b, dd, s: (dd, 0, 0)),
            pl.BlockSpec((None, 1, KB), lambda b, dd, s: (dd, 0, 0)),
            pl.BlockSpec((BLK, DK_B), lambda b, dd, s: (blk(dd, s), 0)),
            pl.BlockSpec((BLK, DK_B), lambda b, dd, s: (blk(dd, s), 0)),
            pl.BlockSpec((BLK, DK_B), lambda b, dd, s: (blk(dd, s), 0)),
            pl.BlockSpec((None, (len(GLA_FINE) + 1) * BLK, BLK), lambda b, dd, s: (dd, 0, 0)),
            pl.BlockSpec((None, BLK // 2, BLK // 2), lambda b, dd, s: (dd, 0, 0)),
        ],
        out_specs=pl.BlockSpec((None, None, BLK, VB), lambda b, dd, s: (dd, b, blk(dd, s), 0)),
        out_shape=jax.ShapeDtypeStruct((2, bsz, TT, VB), BF16),
        scratch_shapes=[pltpu.VMEM((HB, DV_B, DK_B), F32)],
        compiler_params=_cp(3), name="gla",
    )(P, P, P, P, wd_pad, bd, cos_t, sa_t, sb_t, mstack, lev_ids)


def _rope_tables():
    t = jnp.arange(T)
    half = DK_B // 2
    inv = ROPE_BASE ** (-jnp.arange(0, half, 2, dtype=F32) / half)

    def tab(pos):
        ang = pos.astype(F32)[:, None] * inv[None, :]
        return jnp.concatenate([ang, ang], axis=-1)

    ang = jnp.concatenate([tab(t // GRID_W), tab(t % GRID_W)], axis=-1)
    cos, sin = jnp.cos(ang), jnp.sin(ang)
    first = (np.arange(DK_B) % half) < (half // 2)
    sa = jnp.where(first[None, :], -sin, 0.0)
    sb = jnp.where(first[None, :], 0.0, sin)
    ident = jnp.ones((L, DK_B), F32)
    zero = jnp.zeros((L, DK_B), F32)
    return (jnp.concatenate([cos, ident]), jnp.concatenate([sa, zero]), jnp.concatenate([sb, zero]))


def _gelu(x):
    return 0.5 * x * (1.0 + jnp.tanh(0.7978845608028654 * (x + 0.044715 * (x * x * x))))


def _gmlp_body(u_ref, v_ref, g_ref, b_ref, w_ref, bs_ref, o_ref):
    for ch in range(BLK // MIX_CHUNK):
        rs = slice(ch * MIX_CHUNK, (ch + 1) * MIX_CHUNK)
        u = _gelu(u_ref[rs, :].astype(F32))
        v = _gelu(v_ref[rs, :].astype(F32))
        for g in range(GC):
            cs = slice(g * CG, (g + 1) * CG)
            vg = v[:, cs]
            mu = jnp.mean(vg, axis=-1, keepdims=True)
            var = jnp.mean(jnp.square(vg - mu), axis=-1, keepdims=True)
            vn = (vg - mu) * lax.rsqrt(var + EPS) * g_ref[:, cs] + b_ref[:, cs]
            s = _dot(w_ref[g].astype(BF16), vn.astype(BF16)) + bs_ref[g]
            o_ref[rs, cs] = (u[:, cs] * s).astype(BF16)


def _out_kernel(*refs, split):
    oa_ref, of_ref, ob_ref, gb_ref, oc_ref, w_ref = refs[:6]
    if split:
        x_in = _stream_block(refs[6], refs[7])
    else:
        x_in = refs[6][...]
    gg_ref, gpost_ref, gpre_ref, mod_ref, wr_ref, xo_ref, h2_ref, lg_ref = refs[7 + split:]
    o = of_ref[...].astype(F32) + ob_ref[...].astype(F32)
    gb = gb_ref[...].astype(F32)
    parts = []
    for h in range(HB):
        cs = slice(h * DV_B, (h + 1) * DV_B)
        gh = gb[:, cs]
        parts.append((_rms(o[:, cs]) * gg_ref[:, cs] * (gh * _sigmoid(gh))).astype(BF16))
    mixed = jnp.concatenate([oa_ref[...]] + parts + [oc_ref[...]], axis=-1)
    y = _dot(mixed, w_ref[...])
    x1 = x_in + mod_ref[2:3, :] * (_rms(y) * gpost_ref[...])
    xo_ref[...] = x1
    h2 = ((_rms(x1) * gpre_ref[...]) * (1.0 + mod_ref[4:5, :]) + mod_ref[3:4, :]).astype(BF16)
    h2_ref[...] = h2
    wh, wl = _split_bf16(wr_ref[...])
    lg_ref[...] = _dot(h2, wh) + _dot(h2, wl)


def _out_call(o_a, o_g, P, o_c, w_out_b, stream, g_gla, g_post, g_pre, modtab, wr_pad, n_blocks):
    bsz = P.shape[0]
    rows = n_blocks * BLK
    split = isinstance(stream, tuple)
    rowspec = lambda w: pl.BlockSpec((None, BLK, w), lambda b, j: (b, j, 0))
    vec = lambda w: pl.BlockSpec((1, w), lambda b, j: (0, 0))
    return pl.pallas_call(
        functools.partial(_out_kernel, split=split),
        grid=(bsz, n_blocks),
        in_specs=[rowspec(WA),
                  pl.BlockSpec((None, None, BLK, VB), lambda b, j: (0, b, j, 0)),
                  pl.BlockSpec((None, None, BLK, VB), lambda b, j: (1, b, j, 0)),
                  pl.BlockSpec((None, BLK, VB), lambda b, j: (b, j, C_GB // VB)),
                  rowspec(WC),
                  pl.BlockSpec((D, D), lambda b, j: (0, 0))]
                 + (_stream_specs() if split else [rowspec(D)])
                 + [vec(VB), vec(D), vec(D),
                    pl.BlockSpec((None, None, 8, D), lambda b, j: (b, j // NLAT, 0, 0)),
                    pl.BlockSpec((D, 128), lambda b, j: (0, 0))],
        out_specs=[rowspec(D), rowspec(D), rowspec(128)],
        out_shape=[jax.ShapeDtypeStruct((bsz, rows, D), F32),
                   jax.ShapeDtypeStruct((bsz, rows, D), BF16),
                   jax.ShapeDtypeStruct((bsz, rows, 128), F32)],
        compiler_params=_cp(2), name="proj_out",
    )(o_a, o_g, o_g, P, o_c, w_out_b, *(stream if split else (stream,)), g_gla.reshape(1, VB),
      g_post.reshape(1, D), g_pre.reshape(1, D), modtab, wr_pad)


def _route_set(lg, cap, upper):
    n = lg.shape[0]
    lt = lg.T[:N_EXPERTS, :]
    ex = jnp.exp(lt - lt.max(axis=0, keepdims=True))
    aff = ex / ex.sum(axis=0, keepdims=True)
    bits = pltpu.bitcast(aff, jnp.int32)
    capf = jnp.float32(cap)

    def body(i, prefix):
        cand = prefix | jnp.left_shift(jnp.int32(1), 30 - i)
        cnt = jnp.sum(jnp.where(bits >= cand, 1.0, 0.0), axis=1, keepdims=True)
        return jnp.where(cnt >= capf, cand, prefix)

    thr = lax.fori_loop(0, 31, body, jnp.zeros((N_EXPERTS, 1), jnp.int32))
    gt = jnp.where(bits > thr, 1.0, 0.0)
    eq = jnp.where(bits == thr, 1.0, 0.0)
    need = capf - gt.sum(axis=1, keepdims=True)
    rank_eq = _dot(eq.astype(BF16), upper)
    sel = gt + eq * jnp.where(rank_eq < need, 1.0, 0.0)
    slot = _dot(sel.astype(BF16), upper)
    slot = jnp.where(sel > 0.5, slot, -1.0)
    pad = jnp.full((128 - N_EXPERTS, n), -1.0, F32)
    slot_t = jnp.concatenate([slot, pad], axis=0).T
    return slot, aff, slot_t


def _route_kernel(lg_ref, u_ref, *out_refs, with_ctx):
    sl, af, st = _route_set(lg_ref[0:T, :], CAP_LAT, u_ref[...])
    out_refs[0][...] = sl
    out_refs[1][...] = af
    out_refs[2][0:T, :] = st
    if with_ctx:
        sl, af, st = _route_set(lg_ref[T:TT, :], CAP_CTX, u_ref[0:L, 0:L])
        out_refs[3][...] = sl
        out_refs[4][...] = af
        out_refs[2][T:TT, :] = st


def _route_call(lg, upper, with_ctx):
    bsz, rows, _ = lg.shape
    en = lambda n: pl.BlockSpec((None, N_EXPERTS, n), lambda b: (b, 0, 0))
    out_specs = [en(T), en(T), pl.BlockSpec((None, rows, 128), lambda b: (b, 0, 0))]
    out_shape = [jax.ShapeDtypeStruct((bsz, N_EXPERTS, T), F32),
                 jax.ShapeDtypeStruct((bsz, N_EXPERTS, T), F32),
                 jax.ShapeDtypeStruct((bsz, rows, 128), F32)]
    if with_ctx:
        out_specs += [en(L), en(L)]
        out_shape += [jax.ShapeDtypeStruct((bsz, N_EXPERTS, L), F32)] * 2
    return pl.pallas_call(
        functools.partial(_route_kernel, with_ctx=with_ctx),
        grid=(bsz,),
        in_specs=[pl.BlockSpec((None, rows, 128), lambda b: (b, 0, 0)),
                  pl.BlockSpec((T, T), lambda b: (0, 0))],
        out_specs=out_specs, out_shape=out_shape,
        compiler_params=_cp(1), name="route",
    )(lg, upper)


def _gather_kernel(*refs, n_sets):
    ins, outs = refs[:3 * n_sets], refs[3 * n_sets:]
    for k in range(n_sets):
        slot_ref, aff_ref, h_ref = ins[3 * k:3 * k + 3]
        xs_ref, g_ref = outs[2 * k:2 * k + 2]
        cap, n = xs_ref.shape[1], h_ref.shape[0]
        sid = lax.broadcasted_iota(jnp.int32, (cap, n), 0).astype(F32)
        picks = []
        for j in range(GATHER_EXPERTS):
            e = pl.program_id(1) * GATHER_EXPERTS + j
            hit = sid == slot_ref[pl.ds(e, 1), :]
            picks.append(jnp.where(hit, 1.0, 0.0).astype(BF16))
            g = jnp.sum(jnp.where(hit, aff_ref[pl.ds(e, 1), :], 0.0), axis=1, keepdims=True)
            g_ref[j] = jnp.broadcast_to(g, (cap, 128))
        xs = _dot(jnp.concatenate(picks, axis=0), h_ref[...]).astype(BF16)
        xs_ref[...] = xs.reshape(GATHER_EXPERTS, cap, D)


def _gather_call(h2, sets):
    bsz = h2.shape[0]
    in_specs, args, out_specs, out_shape = [], [], [], []
    for slot, aff, n, cap, blk_idx in sets:
        in_specs += [pl.BlockSpec((None, N_EXPERTS, n), lambda b, e: (b, 0, 0)),
                     pl.BlockSpec((None, N_EXPERTS, n), lambda b, e: (b, 0, 0)),
                     pl.BlockSpec((None, n, D), lambda b, e, i=blk_idx: (b, i, 0))]
        args += [slot, aff, h2]
        out_specs += [pl.BlockSpec((GATHER_EXPERTS, cap, D), lambda b, e: (e, b, 0)),
                      pl.BlockSpec((GATHER_EXPERTS, cap, 128), lambda b, e: (e, b, 0))]
        out_shape += [jax.ShapeDtypeStruct((N_EXPERTS, bsz * cap, D), BF16),
                      jax.ShapeDtypeStruct((N_EXPERTS, bsz * cap, 128), F32)]
    return pl.pallas_call(
        functools.partial(_gather_kernel, n_sets=len(sets)),
        grid=(bsz, N_EXPERTS // GATHER_EXPERTS),
        in_specs=in_specs, out_specs=out_specs, out_shape=out_shape,
        compiler_params=_cp(2), name="gather",
    )(*args)


def _ffn_kernel(*refs, rl, rc, nf):
    if rc:
        xl_ref, xc_ref, gl_ref, gc_ref, wg_ref, wu_ref, wd_ref, yl_ref, yc_ref, hid = refs
        groups = ((xl_ref, gl_ref, yl_ref, 0, rl), (xc_ref, gc_ref, yc_ref, rl, rc))
    else:
        xl_ref, gl_ref, wg_ref, wu_ref, wd_ref, yl_ref, hid = refs
        groups = ((xl_ref, gl_ref, yl_ref, 0, rl),)
    s = pl.program_id(1)

    @pl.when(s < nf)
    def _():
        wg = wg_ref[...].astype(BF16)
        wu = wu_ref[...].astype(BF16)
        for x_ref, _, _, r0, nr in groups:
            xv = x_ref[...]
            a = _dot(xv, wg)
            u = _dot(xv, wu)
            hid[s, r0:r0 + nr, :] = ((a * _sigmoid(a)) * u).astype(BF16)

    @pl.when(s >= nf)
    def _():
        wd = wd_ref[...].astype(BF16)
        for _, g_ref, y_ref, r0, nr in groups:
            hm = jnp.concatenate([hid[k, r0:r0 + nr, :] for k in range(nf)], axis=1)
            y = _dot(hm, wd)
            for c in range(FF_TILE // 128):
                cs = slice(128 * c, 128 * (c + 1))
                y_ref[:, cs] = (y[:, cs] * g_ref[...]).astype(BF16)


def _ffn_call(layer, xs_l, g_l, xs_c, g_c, w_gate, w_up, w_down):
    rl = xs_l.shape[1]
    rc = 0 if xs_c is None else xs_c.shape[1]
    nf = EXPERT_FF // FF_TILE
    nd = D // FF_TILE
    up = lambda e, s: (layer, e, 0, jnp.minimum(s, nf - 1))
    down = lambda s: jnp.maximum(s - nf, 0)
    in_specs = [pl.BlockSpec((None, rl, D), lambda e, s: (e, 0, 0))]
    args = [xs_l]
    if rc:
        in_specs.append(pl.BlockSpec((None, rc, D), lambda e, s: (e, 0, 0)))
        args.append(xs_c)
    in_specs.append(pl.BlockSpec((None, rl, 128), lambda e, s: (e, 0, 0)))
    args.append(g_l)
    if rc:
        in_specs.append(pl.BlockSpec((None, rc, 128), lambda e, s: (e, 0, 0)))
        args.append(g_c)
    in_specs += [pl.BlockSpec((None, None, D, FF_TILE), up),
                 pl.BlockSpec((None, None, D, FF_TILE), up),
                 pl.BlockSpec((None, None, EXPERT_FF, FF_TILE), lambda e, s: (layer, e, 0, down(s)))]
    args += [w_gate, w_up, w_down]
    out_specs = [pl.BlockSpec((None, rl, FF_TILE), lambda e, s: (e, 0, down(s)))]
    out_shape = [jax.ShapeDtypeStruct((N_EXPERTS, rl, D), BF16)]
    if rc:
        out_specs.append(pl.BlockSpec((None, rc, FF_TILE), lambda e, s: (e, 0, down(s))))
        out_shape.append(jax.ShapeDtypeStruct((N_EXPERTS, rc, D), BF16))
    return pl.pallas_call(
        functools.partial(_ffn_kernel, rl=rl, rc=rc, nf=nf),
        grid=(N_EXPERTS, nf + nd),
        in_specs=in_specs, out_specs=out_specs, out_shape=out_shape,
        scratch_shapes=[pltpu.VMEM((nf, rl + rc, FF_TILE), BF16)],
        compiler_params=_cp(2), name="expert_ffn",
    )(*args)


def _scatter_kernel(*refs, ne, with_ctx, has_next):
    refs = list(refs)
    slot_ref, yl_ref = refs[:2]
    refs = refs[2:]
    yc_ref = refs.pop(0) if with_ctx else None
    x_ref, gpost_ref, mod_ref = refs[:3]
    refs = refs[3:]
    if has_next:
        gpre_ref, modn_ref = refs[:2]
        refs = refs[2:]
    xo_ref = refs[0]
    hn_ref = refs[1] if has_next else None
    acc = refs[-1]
    eg = pl.program_id(1)
    r = pl.program_id(2)
    ng = pl.num_programs(1)

    def add(cap, y_ref):
        kk = ne * cap
        er = lax.broadcasted_iota(jnp.int32, (128, kk), 0)
        ec = lax.broadcasted_iota(jnp.int32, (128, kk), 1)
        expand = jnp.where(er == eg * ne + ec // cap, 1.0, 0.0).astype(BF16)
        spread = _dot(slot_ref[...].astype(BF16), expand)
        want = (lax.broadcasted_iota(jnp.int32, (BLK, kk), 1) % cap).astype(F32)
        pt = jnp.where(spread == want, 1.0, 0.0).astype(BF16)
        acc[r] += _dot(pt, y_ref[...].reshape(kk, D))

    @pl.when(eg == 0)
    def _():
        acc[r] = jnp.zeros((BLK, D), F32)

    @pl.when(r < NLAT)
    def _():
        add(CAP_LAT, yl_ref)

    if with_ctx:
        @pl.when(r == NLAT)
        def _():
            add(CAP_CTX, yc_ref)

    @pl.when(eg == ng - 1)
    def _():
        x2 = x_ref[...] + mod_ref[5:6, :] * (_rms(acc[r]) * gpost_ref[...])
        xo_ref[...] = x2
        if has_next:
            hn = (_rms(x2) * gpre_ref[...]) * (1.0 + modn_ref[1:2, :]) + modn_ref[0:1, :]
            hn_ref[...] = hn.astype(BF16)


def _scatter_call(slot_t, y_l, y_c, x_mid, g_post, modtab, g_pre_next, modtab_next):
    bsz = x_mid.shape[0]
    with_ctx = y_c is not None
    has_next = g_pre_next is not None
    n_tiles = NBLK if with_ctx else NLAT
    ne = 8
    ng = N_EXPERTS // ne
    vec = pl.BlockSpec((1, D), lambda b, g, r: (0, 0))
    modspec = pl.BlockSpec((None, None, 8, D), lambda b, g, r: (b, r // NLAT, 0, 0))
    late = lambda b, g, r: (b, jnp.where(g == ng - 1, r, 0), 0)
    in_specs = [pl.BlockSpec((None, BLK, 128), lambda b, g, r: (b, r, 0)),
                pl.BlockSpec((ne, CAP_LAT, D), lambda b, g, r: (g, b, 0))]
    args = [slot_t, y_l]
    if with_ctx:
        in_specs.append(pl.BlockSpec((ne, CAP_CTX, D), lambda b, g, r: (g, b, 0)))
        args.append(y_c)
    in_specs += [pl.BlockSpec((None, BLK, D), late), vec, modspec]
    args += [x_mid, g_post.reshape(1, D), modtab]
    if has_next:
        in_specs += [vec, modspec]
        args += [g_pre_next.reshape(1, D), modtab_next]
    out_specs = [pl.BlockSpec((None, BLK, D), late)]
    out_shape = [jax.ShapeDtypeStruct((bsz, n_tiles * BLK, D), F32)]
    if has_next:
        out_specs.append(pl.BlockSpec((None, BLK, D), late))
        out_shape.append(jax.ShapeDtypeStruct((bsz, n_tiles * BLK, D), BF16))
    return pl.pallas_call(
        functools.partial(_scatter_kernel, ne=ne, with_ctx=with_ctx, has_next=has_next),
        grid=(bsz, ng, n_tiles),
        in_specs=in_specs, out_specs=out_specs, out_shape=out_shape,
        scratch_shapes=[pltpu.VMEM((n_tiles, BLK, D), F32)],
        compiler_params=_cp(3), name="scatter",
    )(*args)


def _pack_w_in(w):
    qa, ka, va, kb, vb, dec, qb, gb, uc, vc = jnp.split(
        w, [int(s) for s in np.cumsum((WA, WA, WA, KB, VB, 2 * DEC_RANK, KB, VB, WC))], axis=-1)
    pad = jnp.zeros((D, NP - C_DEC - 2 * DEC_RANK), w.dtype)
    return jnp.concatenate([vb, gb, qa, ka, va, kb, qb, uc, vc, dec, pad], axis=-1).astype(BF16)


def kernel(x, c, ctx, c_ctx, w_ada, b_ada, g_pre_mix, g_post_mix, g_pre_ffn, g_post_ffn, w_in, w_dec,
           b_dec, rpb, g_gla, ln_v_g, ln_v_b, w_sp, b_sp, w_out, w_router, w_gate, w_up, w_down):
    bsz = x.shape[0]
    c_all = jnp.zeros((16, D), F32).at[:bsz].set(c).at[8].set(c_ctx)
    mods = _ada_call(c_all, w_ada, b_ada).reshape(DEPTH, 16, 6, D)
    lat = mods[:, :bsz]
    cx = jnp.broadcast_to(mods[:, 8:9], lat.shape)
    modtab = jnp.pad(jnp.stack([lat, cx], axis=2), ((0, 0), (0, 0), (0, 0), (0, 2), (0, 0)))

    cos_t, sa_t, sb_t = _rope_tables()
    mstack = jnp.asarray(_gla_matrices(), BF16)
    lev_ids = jnp.asarray(_gla_level_ids())
    hmask = jnp.asarray(np.stack([(np.arange(128) < DH_A), (np.arange(128) >= DH_A)]) * DH_A ** -0.5, F32)
    upper = jnp.asarray(np.triu(np.ones((T, T), np.float32), 1), BF16)

    stream = (x, ctx)
    h = _prenorm_call(x, ctx, g_pre_mix[0], modtab[0])
    for l in range(DEPTH):
        last = l == DEPTH - 1
        nb = NLAT if last else NBLK
        P = _proj_call(h.reshape(bsz * TT, D), _pack_w_in(w_in[l])).reshape(bsz, TT, NP)
        bs_rep = jnp.broadcast_to(b_sp[l][:, :, None], (GC, MIX_CHUNK, 128))
        o_a, o_c = _na_gmlp_call(P, _na_bias(rpb[l]), hmask, ln_v_g[l], ln_v_b[l], w_sp[l], bs_rep, nb)
        wd_pad = jnp.zeros((2, 128, KB), F32)
        wd_pad = wd_pad.at[0, :DEC_RANK].set(w_dec[l, 0]).at[1, DEC_RANK:2 * DEC_RANK].set(w_dec[l, 1])
        o_g = _gla_call(P, wd_pad, b_dec[l].reshape(2, 1, KB), cos_t, sa_t, sb_t, mstack, lev_ids)
        wr_pad = jnp.pad(w_router[l], ((0, 0), (0, 128 - N_EXPERTS)))
        x_mid, h2, lg = _out_call(o_a, o_g, P, o_c, w_out[l].astype(BF16), stream, g_gla[l], g_post_mix[l],
                                  g_pre_ffn[l], modtab[l], wr_pad, nb)
        routed = _route_call(lg, upper, not last)
        slot_l, aff_l, slot_t = routed[:3]
        if last:
            xs_l, gt_l = _gather_call(h2, [(slot_l, aff_l, T, CAP_LAT, 0)])
            (y_l,) = _ffn_call(l, xs_l, gt_l, None, None, w_gate, w_up, w_down)
            (stream,) = _scatter_call(slot_t, y_l, None, x_mid, g_post_ffn[l], modtab[l], None, None)
        else:
            slot_c, aff_c = routed[3:]
            xs_l, gt_l, xs_c, gt_c = _gather_call(h2, [(slot_l, aff_l, T, CAP_LAT, 0),
                                                       (slot_c, aff_c, L, CAP_CTX, NLAT)])
            y_l, y_c = _ffn_call(l, xs_l, gt_l, xs_c, gt_c, w_gate, w_up, w_down)
            stream, h = _scatter_call(slot_t, y_l, y_c, x_mid, g_post_ffn[l], modtab[l], g_pre_mix[l + 1],
                                      modtab[l + 1])
    return stream
```

```python
import functools

import numpy as np
import jax
import jax.numpy as jnp
from jax import lax
from jax.experimental import pallas as pl
from jax.experimental.pallas import tpu as pltpu

F32 = jnp.float32
BF16 = jnp.bfloat16

D = 2048
T = 2048
L = 256
TT = T + L
BLK = 256
NBLK = TT // BLK
NLAT = T // BLK
DEPTH = 2
GRID_W = 64
ROWS = T // GRID_W
HA, DH_A = 8, 64
WIN_ROWS, WIN_COLS = 8, 16
HB, DK_B, DV_B = 4, 128, 256
DEC_RANK = 16
GATE_TAU = 16.0
GC, CG, MIX_CHUNK = 4, 128, 128
N_EXPERTS = 16
EXPERT_FF = 2048
CAP_FACTOR = 2
ROPE_BASE = 10000.0
EPS = 1e-6
LOG2E = 1.4426950408889634
WA, KB, VB, WC = HA * DH_A, HB * DK_B, HB * DV_B, GC * CG

NP = 3 * D
C_VB, C_GB, C_QA, C_KA, C_VA, C_KB, C_QB, C_UC, C_VC, C_DEC = (
    0, 1024, 2048, 2560, 3072, 3584, 4096, 4608, 5120, 5632)

NA_QROWS = 4
NA_KROWS = 12
NA_KEYS = NA_KROWS * GRID_W
NEG = -1e30

GLA_LEVELS = 8
GLA_FINE = (5, 6, 7)
FF_TILE = 256
SCATTER_EXPERTS = 8
GATHER_EXPERTS = 2
CAP_LAT = CAP_FACTOR * T // N_EXPERTS
CAP_CTX = CAP_FACTOR * L // N_EXPERTS
VMEM_LIMIT = 56 * 1024 * 1024


def _cp(n_axes):
    return pltpu.CompilerParams(dimension_semantics=("arbitrary",) * n_axes,
                                vmem_limit_bytes=VMEM_LIMIT)


def _dot(a, b):
    return jnp.dot(a, b, preferred_element_type=F32)


def _dot_nt(a, b):
    return lax.dot_general(a, b, (((1,), (1,)), ((), ())), preferred_element_type=F32)


def _dot_tn(a, b):
    return lax.dot_general(a, b, (((0,), (0,)), ((), ())), preferred_element_type=F32)


def _rms(x):
    return x * lax.rsqrt(jnp.mean(x * x, axis=-1, keepdims=True) + EPS)


def _sigmoid(x):
    return 1.0 / (1.0 + jnp.exp(-x))


def _split_bf16(x):
    hi = x.astype(BF16)
    lo = (x - hi.astype(F32)).astype(BF16)
    return hi, lo


def _ada_kernel(c_ref, w_ref, b_ref, o_ref):
    cv = c_ref[...]
    hi, lo = _split_bf16(cv * _sigmoid(cv))
    r = _dot(jnp.concatenate([hi, lo], axis=0), w_ref[...].astype(BF16))
    o_ref[...] = r[:16] + r[16:] + b_ref[...]


def _ada_call(c_all, w_ada, b_ada):
    tn = 1024
    n6 = w_ada.shape[-1]
    return pl.pallas_call(
        _ada_kernel,
        grid=(DEPTH, n6 // tn),
        in_specs=[pl.BlockSpec((16, D), lambda l, n: (0, 0)),
                  pl.BlockSpec((None, D, tn), lambda l, n: (l, 0, n)),
                  pl.BlockSpec((None, 1, tn), lambda l, n: (l, 0, n))],
        out_specs=pl.BlockSpec((None, 16, tn), lambda l, n: (l, 0, n)),
        out_shape=jax.ShapeDtypeStruct((DEPTH, 16, n6), F32),
        compiler_params=_cp(2), name="ada",
    )(c_all, w_ada, b_ada.reshape(DEPTH, 1, n6))


def _stream_block(x_ref, c_ref):
    return jnp.where(pl.program_id(1) < NLAT, x_ref[...], c_ref[...])


def _stream_specs():
    return [pl.BlockSpec((None, BLK, D), lambda b, j: (b, jnp.minimum(j, NLAT - 1), 0)),
            pl.BlockSpec((None, BLK, D), lambda b, j: (b, 0, 0))]


def _prenorm_kernel(x_ref, c_ref, g_ref, mod_ref, h_ref):
    y = _rms(_stream_block(x_ref, c_ref)) * g_ref[...]
    h_ref[...] = (y * (1.0 + mod_ref[1:2, :]) + mod_ref[0:1, :]).astype(BF16)


def _prenorm_call(x, ctx, g, modtab):
    bsz = x.shape[0]
    return pl.pallas_call(
        _prenorm_kernel,
        grid=(bsz, NBLK),
        in_specs=_stream_specs() + [
            pl.BlockSpec((1, D), lambda b, j: (0, 0)),
            pl.BlockSpec((None, None, 8, D), lambda b, j: (b, j // NLAT, 0, 0))],
        out_specs=pl.BlockSpec((None, BLK, D), lambda b, j: (b, j, 0)),
        out_shape=jax.ShapeDtypeStruct((bsz, TT, D), BF16),
        compiler_params=_cp(2), name="prenorm",
    )(x, ctx, g.reshape(1, D), modtab)


def _proj_kernel(h_ref, w_ref, o_ref):
    o_ref[...] = _dot(h_ref[...], w_ref[...]).astype(BF16)


def _proj_call(h2d, w_pack):
    m = h2d.shape[0]
    tm = 1024 if m % 1024 == 0 else 768
    tn = NP // 3
    return pl.pallas_call(
        _proj_kernel,
        grid=(NP // tn, m // tm),
        in_specs=[pl.BlockSpec((tm, D), lambda n, i: (i, 0)),
                  pl.BlockSpec((D, tn), lambda n, i: (0, n))],
        out_specs=pl.BlockSpec((tm, tn), lambda n, i: (i, n)),
        out_shape=jax.ShapeDtypeStruct((m, NP), BF16),
        compiler_params=_cp(2), name="proj_in",
    )(h2d, w_pack)


def _softmax_pv(s_list, v_list):
    m = s_list[0].max(axis=-1, keepdims=True)
    for s in s_list[1:]:
        m = jnp.maximum(m, s.max(axis=-1, keepdims=True))
    acc = None
    for s, v in zip(s_list, v_list):
        o = _dot(jnp.exp(s - m).astype(BF16), v)
        acc = o if acc is None else acc + o
    return acc[:, :128] / acc[:, 128:]


def _na_kernel(q_ref, k_ref, v_ref, bias_ref, hm_ref, *mlp_refs_o_ref):
    mlp_in, o_ref, oc_ref = mlp_refs_o_ref[:-2], mlp_refs_o_ref[-2], mlp_refs_o_ref[-1]
    j = pl.program_id(1)
    lane = lax.broadcasted_iota(jnp.int32, (BLK, 128), 1)
    low = lane < DH_A

    def run(local_start):
        _gmlp_body(*mlp_in, oc_ref)
        for p in range(HA // 2):
            sl = slice(128 * p, 128 * p + 128)
            q2 = q_ref[:, sl]
            kc = k_ref[T:TT, sl]
            vc = jnp.concatenate([v_ref[T:TT, sl], jnp.ones((L, 128), BF16)], axis=1)
            if local_start is not None:
                kl = k_ref[pl.ds(local_start, NA_KEYS), sl]
                vl = jnp.concatenate([v_ref[pl.ds(local_start, NA_KEYS), sl],
                                      jnp.ones((NA_KEYS, 128), BF16)], axis=1)
            pair = []
            for hh in range(2):
                qm = (q2.astype(F32) * hm_ref[hh:hh + 1, :]).astype(BF16)
                s_ctx = _dot_nt(qm, kc)
                if local_start is not None:
                    s_loc = _dot_nt(qm, kl) + bias_ref[2 * p + hh]
                    pair.append(_softmax_pv([s_loc, s_ctx], [vl, vc]))
                else:
                    pair.append(_softmax_pv([s_ctx], [vc]))
            o_ref[:, sl] = jnp.where(low, pair[0], pair[1]).astype(BF16)

    @pl.when(j < NLAT)
    def _():
        krow = jnp.clip(j * NA_QROWS - WIN_ROWS // 2, 0, ROWS - NA_KROWS)
        run(pl.multiple_of(krow * GRID_W, GRID_W))

    @pl.when(j == NLAT)
    def _():
        run(None)


def _na_gmlp_call(P, bias, hmask, ln_g, ln_b, w_sp, bs_rep, n_blocks):
    bsz = P.shape[0]

    def bias_idx(b, j):
        return (jnp.where(j == 0, 0, jnp.where(j == NLAT - 1, 2, 1)), 0, 0, 0)

    rows = n_blocks * BLK
    return pl.pallas_call(
        _na_kernel,
        grid=(bsz, n_blocks),
        in_specs=[pl.BlockSpec((None, BLK, WA), lambda b, j: (b, j, C_QA // WA)),
                  pl.BlockSpec((None, TT, WA), lambda b, j: (b, 0, C_KA // WA)),
                  pl.BlockSpec((None, TT, WA), lambda b, j: (b, 0, C_VA // WA)),
                  pl.BlockSpec((None, HA, BLK, NA_KEYS), bias_idx),
                  pl.BlockSpec((2, 128), lambda b, j: (0, 0)),
                  pl.BlockSpec((None, BLK, WC), lambda b, j: (b, j, C_UC // WC)),
                  pl.BlockSpec((None, BLK, WC), lambda b, j: (b, j, C_VC // WC)),
                  pl.BlockSpec((1, WC), lambda b, j: (0, 0)),
                  pl.BlockSpec((1, WC), lambda b, j: (0, 0)),
                  pl.BlockSpec((GC, MIX_CHUNK, MIX_CHUNK), lambda b, j: (0, 0, 0)),
                  pl.BlockSpec((GC, MIX_CHUNK, 128), lambda b, j: (0, 0, 0))],
        out_specs=[pl.BlockSpec((None, BLK, WA), lambda b, j: (b, j, 0)),
                   pl.BlockSpec((None, BLK, WC), lambda b, j: (b, j, 0))],
        out_shape=[jax.ShapeDtypeStruct((bsz, rows, WA), BF16),
                   jax.ShapeDtypeStruct((bsz, rows, WC), BF16)],
        compiler_params=_cp(2), name="nbr_attn_gmlp",
    )(P, P, P, bias, hmask, P, P, ln_g.reshape(1, WC), ln_b.reshape(1, WC), w_sp, bs_rep)


def _na_bias(rpb_l):
    n_dr, n_dc = 2 * WIN_ROWS - 1, 2 * WIN_COLS - 1
    cq = np.arange(GRID_W)[:, None]
    ck = np.arange(GRID_W)[None, :]
    cs = np.clip(cq - WIN_COLS // 2, 0, GRID_W - WIN_COLS)
    col_ok = (ck >= cs) & (ck < cs + WIN_COLS)
    dc = np.clip(ck - cq + WIN_COLS - 1, 0, n_dc - 1)
    col_sel = (dc[:, :, None] == np.arange(n_dc)).astype(np.float32)
    colx = jnp.einsum("qke,hde->hdqk", col_sel, rpb_l, precision=lax.Precision.HIGHEST)
    colx = jnp.where(col_ok, colx, NEG).astype(F32)
    return pl.pallas_call(
        _bias_kernel,
        grid=(3, HA),
        in_specs=[pl.BlockSpec((None, n_dr, GRID_W, GRID_W), lambda t, h: (h, 0, 0, 0))],
        out_specs=pl.BlockSpec((None, None, BLK, NA_KEYS), lambda t, h: (t, h, 0, 0)),
        out_shape=jax.ShapeDtypeStruct((3, HA, BLK, NA_KEYS), F32),
        compiler_params=_cp(2), name="na_bias",
    )(colx)


def _bias_kernel(colx_ref, o_ref):
    t = pl.program_id(0)
    neg = jnp.full((GRID_W, GRID_W), NEG, F32)
    for tt, rb in enumerate((0, 1, NLAT - 1)):
        @pl.when(t == tt)
        def _(rb=rb):
            k0 = int(np.clip(rb * NA_QROWS - WIN_ROWS // 2, 0, ROWS - NA_KROWS))
            for qi in range(NA_QROWS):
                r = rb * NA_QROWS + qi
                rs = int(np.clip(r - WIN_ROWS // 2, 0, ROWS - WIN_ROWS))
                for kp in range(NA_KROWS // 2):
                    pair = []
                    for kr in (k0 + 2 * kp, k0 + 2 * kp + 1):
                        inside = rs <= kr < rs + WIN_ROWS
                        pair.append(colx_ref[kr - r + WIN_ROWS - 1] if inside else neg)
                    o_ref[qi * GRID_W:(qi + 1) * GRID_W, kp * 128:(kp + 1) * 128] = (
                        jnp.concatenate(pair, axis=1))


def _gla_matrices():
    n = BLK
    i = np.arange(n)[:, None]
    t = np.arange(n)[None, :]
    out = np.zeros((2, (len(GLA_FINE) + 1) * n, n), np.float32)
    for f, lvl in enumerate(GLA_FINE):
        s = n >> (lvl + 1)
        blk0 = (i // (2 * s)) * (2 * s)
        m = blk0 + s - 1
        fwd = np.where(i > m, (t > m) & (t <= i), (t > i) & (t <= m))
        bwd = np.where(i <= m, (t >= i) & (t <= m), (t > m) & (t < i))
        out[0, f * n:(f + 1) * n] = fwd
        out[1, f * n:(f + 1) * n] = bwd
    out[0, len(GLA_FINE) * n:] = t <= i
    out[1, len(GLA_FINE) * n:] = t >= i
    return out


def _gla_level_ids():
    h = BLK // 2
    i = np.arange(h)[:, None]
    j = np.arange(h)[None, :]
    x = np.maximum(i ^ j, 1)
    lvl = GLA_LEVELS - 1 - np.floor(np.log2(x)).astype(np.int32)
    fwd = np.where(i == j, -1, np.where(i > j, lvl, -2))
    bwd = np.where(i == j, -1, np.where(i < j, lvl, -2))
    return np.stack([fwd, bwd]).astype(np.int32)


def _gla_body(rev, kb_ref, qb_ref, vb_ref, dec_ref, wd_ref, bd_ref, rope_ref, m_ref, lev_ref, o_ref,
              st_ref):
    half = BLK // 2
    z = _dot(dec_ref[...], wd_ref[...].astype(BF16)) + bd_ref[...]
    la = (jnp.minimum(z, 0.0) - jnp.log1p(jnp.exp(-jnp.abs(z)))) * (LOG2E / GATE_TAU)
    la_b = la.astype(BF16)

    def seg_sum(f):
        return _dot(m_ref[f * BLK:(f + 1) * BLK, :], la_b)

    run = seg_sum(len(GLA_FINE))
    e_in = jnp.exp2(run)
    e_out = jnp.exp2((run[0:1] if rev else run[BLK - 1:BLK]) - run)
    e_fine = {lvl: jnp.exp2(seg_sum(f)) for f, lvl in enumerate(GLA_FINE)}

    def slab_decay(i, s, ks):
        base = (i // 2) * 2 * s
        ref = base + s if rev else base + s - 1
        d = run[i * s:(i + 1) * s, ks] - run[ref:ref + 1, ks]
        return jnp.exp2(-d if ((i % 2 == 0) != rev) else d)

    row = lax.broadcasted_iota(jnp.int32, (BLK, DK_B), 0)
    lev = lev_ref[...]
    on_diag = lev == -1
    at_level = {lvl: lev == lvl for lvl in range(1, GLA_LEVELS)}
    cosv, sav, sbv = rope_ref[0], rope_ref[1], rope_ref[2]

    def rope(x):
        return x * cosv + pltpu.roll(x, DK_B - 32, 1) * sav + pltpu.roll(x, 32, 1) * sbv

    qi, ki = (0, 1) if rev else (1, 0)
    q_rows = slice(qi * half, (qi + 1) * half)
    k_rows = slice(ki * half, (ki + 1) * half)

    outs = []
    for hh in range(HB):
        ks = slice(DK_B * hh, DK_B * (hh + 1))
        q = rope(qb_ref[:, ks].astype(F32)) * (DK_B ** -0.5)
        k = rope(kb_ref[:, ks].astype(F32))
        v = vb_ref[:, DV_B * hh:DV_B * (hh + 1)]
        self_w = jnp.sum(q * k, axis=-1, keepdims=True)
        cross = _dot_nt((q[q_rows] * slab_decay(qi, half, ks)).astype(BF16),
                        (k[k_rows] * slab_decay(ki, half, ks)).astype(BF16))
        diag = [jnp.where(on_diag, self_w[c * half:(c + 1) * half], 0.0) for c in range(2)]
        for lvl in range(1, GLA_LEVELS):
            s = BLK >> (lvl + 1)
            if lvl not in GLA_FINE:
                parts = []
                for i in range(BLK // s):
                    src = q if ((i % 2 == 1) != rev) else k
                    parts.append(src[i * s:(i + 1) * s] * slab_decay(i, s, ks))
                x = jnp.concatenate(parts, axis=0)
            else:
                second = ((row // s) & 1) == 1
                x = (jnp.where(second, k, q) if rev else jnp.where(second, q, k)) * e_fine[lvl][:, ks]
            xb = x.astype(BF16)
            for c in range(2):
                xc = xb[c * half:(c + 1) * half]
                diag[c] = jnp.where(at_level[lvl], _dot_nt(xc, xc), diag[c])
        zero = jnp.zeros((half, half), F32)
        if rev:
            att = jnp.concatenate([jnp.concatenate([diag[0], cross], axis=1),
                                   jnp.concatenate([zero, diag[1]], axis=1)], axis=0)
        else:
            att = jnp.concatenate([jnp.concatenate([diag[0], zero], axis=1),
                                   jnp.concatenate([cross, diag[1]], axis=1)], axis=0)
        e_q = e_in[:, ks]
        st = st_ref[hh]
        o = _dot_nt((q * e_q).astype(BF16), st.astype(BF16)) + _dot(att.astype(BF16), v)
        outs.append(o.astype(BF16))
        carry = e_q[0:1, :] if rev else e_q[BLK - 1:BLK, :]
        kt = (k * e_out[:, ks]).astype(BF16)
        st_ref[hh] = st * carry + _dot_tn(v, kt)
    o_ref[...] = jnp.concatenate(outs, axis=-1)


def _gla_kernel(*refs):
    st_ref = refs[-1]
    d = pl.program_id(1)

    @pl.when(pl.program_id(2) == 0)
    def _():
        st_ref[...] = jnp.zeros_like(st_ref)

    @pl.when(d == 0)
    def _():
        _gla_body(False, *refs)

    @pl.when(d == 1)
    def _():
        _gla_body(True, *refs)


def _gla_call(P, wd_pad, bd, rope, mstack, lev_ids):
    bsz = P.shape[0]

    def blk(dd, s):
        return jnp.where(s == 0, NLAT, jnp.where(dd == 0, s - 1, NLAT - s))

    return pl.pallas_call(
        _gla_kernel,
        grid=(bsz, 2, NBLK),
        in_specs=[
            pl.BlockSpec((None, BLK, KB), lambda b, dd, s: (b, blk(dd, s), C_KB // KB)),
            pl.BlockSpec((None, BLK, KB), lambda b, dd, s: (b, blk(dd, s), C_QB // KB)),
            pl.BlockSpec((None, BLK, VB), lambda b, dd, s: (b, blk(dd, s), C_VB // VB)),
            pl.BlockSpec((None, BLK, 128), lambda b, dd, s: (b, blk(dd, s), C_DEC // 128)),
            pl.BlockSpec((None, 128, KB), lambda b, dd, s: (dd, 0, 0)),
            pl.BlockSpec((None, 1, KB), lambda b, dd, s: (dd, 0, 0)),
            pl.BlockSpec((3, BLK, DK_B), lambda b, dd, s: (0, blk(dd, s), 0)),
            pl.BlockSpec((None, (len(GLA_FINE) + 1) * BLK, BLK), lambda b, dd, s: (dd, 0, 0)),
            pl.BlockSpec((None, BLK // 2, BLK // 2), lambda b, dd, s: (dd, 0, 0)),
        ],
        out_specs=pl.BlockSpec((None, None, BLK, VB), lambda b, dd, s: (dd, b, blk(dd, s), 0)),
        out_shape=jax.ShapeDtypeStruct((2, bsz, TT, VB), BF16),
        scratch_shapes=[pltpu.VMEM((HB, DV_B, DK_B), F32)],
        compiler_params=_cp(3), name="gla",
    )(P, P, P, P, wd_pad, bd, rope, mstack, lev_ids)


def _rope_tables():
    t = np.arange(T)
    half = DK_B // 2
    inv = np.float32(ROPE_BASE) ** (-np.arange(0, half, 2, dtype=np.float32) / np.float32(half))

    def tab(pos):
        ang = pos.astype(np.float32)[:, None] * inv[None, :]
        return np.concatenate([ang, ang], axis=-1)

    ang = np.concatenate([tab(t // GRID_W), tab(t % GRID_W)], axis=-1)
    cos, sin = np.cos(ang), np.sin(ang)
    first = (np.arange(DK_B) % half) < (half // 2)
    sa = np.where(first[None, :], -sin, 0.0)
    sb = np.where(first[None, :], 0.0, sin)
    ident = np.ones((L, DK_B), np.float32)
    zero = np.zeros((L, DK_B), np.float32)
    tabs = np.stack([np.concatenate([cos, ident]), np.concatenate([sa, zero]), np.concatenate([sb, zero])])
    return jnp.asarray(tabs, F32)


def _gelu(x):
    return 0.5 * x * (1.0 + jnp.tanh(0.7978845608028654 * (x + 0.044715 * (x * x * x))))


def _gmlp_body(u_ref, v_ref, g_ref, b_ref, w_ref, bs_ref, o_ref):
    for ch in range(BLK // MIX_CHUNK):
        rs = slice(ch * MIX_CHUNK, (ch + 1) * MIX_CHUNK)
        u = _gelu(u_ref[rs, :].astype(F32))
        v = _gelu(v_ref[rs, :].astype(F32))
        for g in range(GC):
            cs = slice(g * CG, (g + 1) * CG)
            vg = v[:, cs]
            mu = jnp.mean(vg, axis=-1, keepdims=True)
            var = jnp.mean(jnp.square(vg - mu), axis=-1, keepdims=True)
            vn = (vg - mu) * lax.rsqrt(var + EPS) * g_ref[:, cs] + b_ref[:, cs]
            s = _dot(w_ref[g].astype(BF16), vn.astype(BF16)) + bs_ref[g]
            o_ref[rs, cs] = (u[:, cs] * s).astype(BF16)


def _out_kernel(*refs, split):
    oa_ref, og_ref, gb_ref, oc_ref, w_ref = refs[:5]
    if split:
        x_in = _stream_block(refs[5], refs[6])
    else:
        x_in = refs[5][...]
    gg_ref, gpost_ref, gpre_ref, mod_ref, wr_ref, xo_ref, h2_ref, lg_ref = refs[6 + split:]
    o = og_ref[0].astype(F32) + og_ref[1].astype(F32)
    gb = gb_ref[...].astype(F32)
    parts = []
    for h in range(HB):
        cs = slice(h * DV_B, (h + 1) * DV_B)
        gh = gb[:, cs]
        parts.append((_rms(o[:, cs]) * gg_ref[:, cs] * (gh * _sigmoid(gh))).astype(BF16))
    mixed = jnp.concatenate([oa_ref[...]] + parts + [oc_ref[...]], axis=-1)
    y = _dot(mixed, w_ref[...])
    x1 = x_in + mod_ref[2:3, :] * (_rms(y) * gpost_ref[...])
    xo_ref[...] = x1
    h2 = ((_rms(x1) * gpre_ref[...]) * (1.0 + mod_ref[4:5, :]) + mod_ref[3:4, :]).astype(BF16)
    h2_ref[...] = h2
    wh, wl = _split_bf16(wr_ref[...])
    lg_ref[...] = _dot(h2, wh) + _dot(h2, wl)


def _out_call(o_a, o_g, P, o_c, w_out_b, stream, g_gla, g_post, g_pre, modtab, wr_pad, n_blocks):
    bsz = P.shape[0]
    rows = n_blocks * BLK
    split = isinstance(stream, tuple)
    rowspec = lambda w: pl.BlockSpec((None, BLK, w), lambda b, j: (b, j, 0))
    vec = lambda w: pl.BlockSpec((1, w), lambda b, j: (0, 0))
    return pl.pallas_call(
        functools.partial(_out_kernel, split=split),
        grid=(bsz, n_blocks),
        in_specs=[rowspec(WA),
                  pl.BlockSpec((2, None, BLK, VB), lambda b, j: (0, b, j, 0)),
                  pl.BlockSpec((None, BLK, VB), lambda b, j: (b, j, C_GB // VB)),
                  rowspec(WC),
                  pl.BlockSpec((D, D), lambda b, j: (0, 0))]
                 + (_stream_specs() if split else [rowspec(D)])
                 + [vec(VB), vec(D), vec(D),
                    pl.BlockSpec((None, None, 8, D), lambda b, j: (b, j // NLAT, 0, 0)),
                    pl.BlockSpec((D, 128), lambda b, j: (0, 0))],
        out_specs=[rowspec(D), rowspec(D), rowspec(128)],
        out_shape=[jax.ShapeDtypeStruct((bsz, rows, D), F32),
                   jax.ShapeDtypeStruct((bsz, rows, D), BF16),
                   jax.ShapeDtypeStruct((bsz, rows, 128), F32)],
        compiler_params=_cp(2), name="proj_out",
    )(o_a, o_g, P, o_c, w_out_b, *(stream if split else (stream,)), g_gla.reshape(1, VB),
      g_post.reshape(1, D), g_pre.reshape(1, D), modtab, wr_pad)


def _route_set(lg, cap, upper):
    n = lg.shape[0]
    lt = lg.T[:N_EXPERTS, :]
    ex = jnp.exp(lt - lt.max(axis=0, keepdims=True))
    aff = ex / ex.sum(axis=0, keepdims=True)
    bits = pltpu.bitcast(aff, jnp.int32)
    capf = jnp.float32(cap)

    def body(i, prefix):
        cand = prefix | jnp.left_shift(jnp.int32(1), 30 - i)
        cnt = jnp.sum(jnp.where(bits >= cand, 1.0, 0.0), axis=1, keepdims=True)
        return jnp.where(cnt >= capf, cand, prefix)

    thr = lax.fori_loop(0, 31, body, jnp.zeros((N_EXPERTS, 1), jnp.int32))
    gt = jnp.where(bits > thr, 1.0, 0.0)
    eq = jnp.where(bits == thr, 1.0, 0.0)
    need = capf - gt.sum(axis=1, keepdims=True)
    rank_eq = _dot(eq.astype(BF16), upper)
    sel = gt + eq * jnp.where(rank_eq < need, 1.0, 0.0)
    slot = _dot(sel.astype(BF16), upper)
    slot = jnp.where(sel > 0.5, slot, -1.0)
    pad = jnp.full((128 - SCATTER_EXPERTS, n), -1.0, F32)
    slot_t = [jnp.concatenate([slot[g * SCATTER_EXPERTS:(g + 1) * SCATTER_EXPERTS], pad], axis=0).T
              for g in range(N_EXPERTS // SCATTER_EXPERTS)]
    return slot, aff, slot_t


def _route_kernel(lg_ref, u_ref, *out_refs, with_ctx):
    sl, af, st = _route_set(lg_ref[0:T, :], CAP_LAT, u_ref[...])
    out_refs[0][...] = sl
    out_refs[1][...] = af
    for g, s in enumerate(st):
        out_refs[2][g, 0:T, :] = s
    if with_ctx:
        sl, af, st = _route_set(lg_ref[T:TT, :], CAP_CTX, u_ref[0:L, 0:L])
        out_refs[3][...] = sl
        out_refs[4][...] = af
        for g, s in enumerate(st):
            out_refs[2][g, T:TT, :] = s


def _route_call(lg, upper, with_ctx):
    bsz, rows, _ = lg.shape
    ng = N_EXPERTS // SCATTER_EXPERTS
    en = lambda n: pl.BlockSpec((None, N_EXPERTS, n), lambda b: (b, 0, 0))
    out_specs = [en(T), en(T), pl.BlockSpec((None, ng, rows, 128), lambda b: (b, 0, 0, 0))]
    out_shape = [jax.ShapeDtypeStruct((bsz, N_EXPERTS, T), F32),
                 jax.ShapeDtypeStruct((bsz, N_EXPERTS, T), F32),
                 jax.ShapeDtypeStruct((bsz, ng, rows, 128), F32)]
    if with_ctx:
        out_specs += [en(L), en(L)]
        out_shape += [jax.ShapeDtypeStruct((bsz, N_EXPERTS, L), F32)] * 2
    return pl.pallas_call(
        functools.partial(_route_kernel, with_ctx=with_ctx),
        grid=(bsz,),
        in_specs=[pl.BlockSpec((None, rows, 128), lambda b: (b, 0, 0)),
                  pl.BlockSpec((T, T), lambda b: (0, 0))],
        out_specs=out_specs, out_shape=out_shape,
        compiler_params=_cp(1), name="route",
    )(lg, upper)


def _gather_kernel(*refs, n_sets):
    ins, outs = refs[:3 * n_sets], refs[3 * n_sets:]
    for k in range(n_sets):
        slot_ref, aff_ref, h_ref = ins[3 * k:3 * k + 3]
        xs_ref, g_ref = outs[2 * k:2 * k + 2]
        cap, n = xs_ref.shape[1], h_ref.shape[0]
        sid = lax.broadcasted_iota(jnp.int32, (cap, n), 0).astype(F32)
        picks = []
        for j in range(GATHER_EXPERTS):
            e = pl.program_id(1) * GATHER_EXPERTS + j
            hit = sid == slot_ref[pl.ds(e, 1), :]
            picks.append(jnp.where(hit, 1.0, 0.0).astype(BF16))
            g = jnp.sum(jnp.where(hit, aff_ref[pl.ds(e, 1), :], 0.0), axis=1, keepdims=True)
            g_ref[j] = jnp.broadcast_to(g, (cap, 128))
        xs = _dot(jnp.concatenate(picks, axis=0), h_ref[...]).astype(BF16)
        xs_ref[...] = xs.reshape(GATHER_EXPERTS, cap, D)


def _gather_call(h2, sets):
    bsz = h2.shape[0]
    in_specs, args, out_specs, out_shape = [], [], [], []
    for slot, aff, n, cap, blk_idx in sets:
        in_specs += [pl.BlockSpec((None, N_EXPERTS, n), lambda b, e: (b, 0, 0)),
                     pl.BlockSpec((None, N_EXPERTS, n), lambda b, e: (b, 0, 0)),
                     pl.BlockSpec((None, n, D), lambda b, e, i=blk_idx: (b, i, 0))]
        args += [slot, aff, h2]
        out_specs += [pl.BlockSpec((GATHER_EXPERTS, cap, D), lambda b, e: (e, b, 0)),
                      pl.BlockSpec((GATHER_EXPERTS, cap, 128), lambda b, e: (e, b, 0))]
        out_shape += [jax.ShapeDtypeStruct((N_EXPERTS, bsz * cap, D), BF16),
                      jax.ShapeDtypeStruct((N_EXPERTS, bsz * cap, 128), F32)]
    return pl.pallas_call(
        functools.partial(_gather_kernel, n_sets=len(sets)),
        grid=(bsz, N_EXPERTS // GATHER_EXPERTS),
        in_specs=in_specs, out_specs=out_specs, out_shape=out_shape,
        compiler_params=_cp(2), name="gather",
    )(*args)


def _ffn_kernel(*refs, rl, rc, nf):
    if rc:
        xl_ref, xc_ref, gl_ref, gc_ref, wg_ref, wu_ref, wd_ref, yl_ref, yc_ref, hid = refs
        groups = ((xl_ref, gl_ref, yl_ref, 0, rl), (xc_ref, gc_ref, yc_ref, rl, rc))
    else:
        xl_ref, gl_ref, wg_ref, wu_ref, wd_ref, yl_ref, hid = refs
        groups = ((xl_ref, gl_ref, yl_ref, 0, rl),)
    s = pl.program_id(1)

    @pl.when(s < nf)
    def _():
        wg = wg_ref[...].astype(BF16)
        wu = wu_ref[...].astype(BF16)
        for x_ref, _, _, r0, nr in groups:
            xv = x_ref[...]
            a = _dot(xv, wg)
            u = _dot(xv, wu)
            hid[s, r0:r0 + nr, :] = ((a * _sigmoid(a)) * u).astype(BF16)

    @pl.when(s >= nf)
    def _():
        wd = wd_ref[...].astype(BF16)
        for _, g_ref, y_ref, r0, nr in groups:
            hm = jnp.concatenate([hid[k, r0:r0 + nr, :] for k in range(nf)], axis=1)
            y = _dot(hm, wd)
            for c in range(FF_TILE // 128):
                cs = slice(128 * c, 128 * (c + 1))
                y_ref[:, cs] = (y[:, cs] * g_ref[...]).astype(BF16)


def _ffn_call(layer, xs_l, g_l, xs_c, g_c, w_gate, w_up, w_down):
    rl = xs_l.shape[1]
    rc = 0 if xs_c is None else xs_c.shape[1]
    nf = EXPERT_FF // FF_TILE
    nd = D // FF_TILE
    up = lambda e, s: (layer, e, 0, jnp.minimum(s, nf - 1))
    down = lambda s: jnp.maximum(s - nf, 0)
    in_specs = [pl.BlockSpec((None, rl, D), lambda e, s: (e, 0, 0))]
    args = [xs_l]
    if rc:
        in_specs.append(pl.BlockSpec((None, rc, D), lambda e, s: (e, 0, 0)))
        args.append(xs_c)
    in_specs.append(pl.BlockSpec((None, rl, 128), lambda e, s: (e, 0, 0)))
    args.append(g_l)
    if rc:
        in_specs.append(pl.BlockSpec((None, rc, 128), lambda e, s: (e, 0, 0)))
        args.append(g_c)
    in_specs += [pl.BlockSpec((None, None, D, FF_TILE), up),
                 pl.BlockSpec((None, None, D, FF_TILE), up),
                 pl.BlockSpec((None, None, EXPERT_FF, FF_TILE), lambda e, s: (layer, e, 0, down(s)))]
    args += [w_gate, w_up, w_down]
    out_specs = [pl.BlockSpec((None, rl, FF_TILE), lambda e, s: (e, 0, down(s)))]
    out_shape = [jax.ShapeDtypeStruct((N_EXPERTS, rl, D), BF16)]
    if rc:
        out_specs.append(pl.BlockSpec((None, rc, FF_TILE), lambda e, s: (e, 0, down(s))))
        out_shape.append(jax.ShapeDtypeStruct((N_EXPERTS, rc, D), BF16))
    return pl.pallas_call(
        functools.partial(_ffn_kernel, rl=rl, rc=rc, nf=nf),
        grid=(N_EXPERTS, nf + nd),
        in_specs=in_specs, out_specs=out_specs, out_shape=out_shape,
        scratch_shapes=[pltpu.VMEM((nf, rl + rc, FF_TILE), BF16)],
        compiler_params=_cp(2), name="expert_ffn",
    )(*args)


def _scatter_kernel(*refs, ne, with_ctx, has_next):
    refs = list(refs)
    slot_ref, yl_ref = refs[:2]
    refs = refs[2:]
    yc_ref = refs.pop(0) if with_ctx else None
    x_ref, gpost_ref, mod_ref = refs[:3]
    refs = refs[3:]
    if has_next:
        gpre_ref, modn_ref = refs[:2]
        refs = refs[2:]
    xo_ref = refs[0]
    hn_ref = refs[1] if has_next else None
    acc = refs[-1]
    eg = pl.program_id(1)
    r = pl.program_id(2)
    ng = pl.num_programs(1)

    def add(cap, y_ref):
        kk = ne * cap
        if cap % 128 == 0:
            want = lax.broadcasted_iota(jnp.int32, (BLK, cap), 1).astype(F32)
            pt = jnp.concatenate([jnp.where(slot_ref[:, j:j + 1] == want, 1.0, 0.0).astype(BF16)
                                  for j in range(ne)], axis=1)
        else:
            er = lax.broadcasted_iota(jnp.int32, (128, kk), 0)
            ec = lax.broadcasted_iota(jnp.int32, (128, kk), 1)
            expand = jnp.where(er == ec // cap, 1.0, 0.0).astype(BF16)
            spread = _dot(slot_ref[...].astype(BF16), expand)
            want = (lax.broadcasted_iota(jnp.int32, (BLK, kk), 1) % cap).astype(F32)
            pt = jnp.where(spread == want, 1.0, 0.0).astype(BF16)
        acc[r] += _dot(pt, y_ref[...].reshape(kk, D))

    @pl.when(eg == 0)
    def _():
        acc[r] = jnp.zeros((BLK, D), F32)

    @pl.when(r < NLAT)
    def _():
        add(CAP_LAT, yl_ref)

    if with_ctx:
        @pl.when(r == NLAT)
        def _():
            add(CAP_CTX, yc_ref)

    @pl.when(eg == ng - 1)
    def _():
        x2 = x_ref[...] + mod_ref[5:6, :] * (_rms(acc[r]) * gpost_ref[...])
        xo_ref[...] = x2
        if has_next:
            hn = (_rms(x2) * gpre_ref[...]) * (1.0 + modn_ref[1:2, :]) + modn_ref[0:1, :]
            hn_ref[...] = hn.astype(BF16)


def _scatter_call(slot_t, y_l, y_c, x_mid, g_post, modtab, g_pre_next, modtab_next):
    bsz = x_mid.shape[0]
    with_ctx = y_c is not None
    has_next = g_pre_next is not None
    n_tiles = NBLK if with_ctx else NLAT
    ne = SCATTER_EXPERTS
    ng = N_EXPERTS // ne
    vec = pl.BlockSpec((1, D), lambda b, g, r: (0, 0))
    modspec = pl.BlockSpec((None, None, 8, D), lambda b, g, r: (b, r // NLAT, 0, 0))
    late = lambda b, g, r: (b, jnp.where(g == ng - 1, r, 0), 0)
    in_specs = [pl.BlockSpec((None, None, BLK, 128), lambda b, g, r: (b, g, r, 0)),
                pl.BlockSpec((ne, CAP_LAT, D), lambda b, g, r: (g, b, 0))]
    args = [slot_t, y_l]
    if with_ctx:
        in_specs.append(pl.BlockSpec((ne, CAP_CTX, D), lambda b, g, r: (g, b, 0)))
        args.append(y_c)
    in_specs += [pl.BlockSpec((None, BLK, D), late), vec, modspec]
    args += [x_mid, g_post.reshape(1, D), modtab]
    if has_next:
        in_specs += [vec, modspec]
        args += [g_pre_next.reshape(1, D), modtab_next]
    out_specs = [pl.BlockSpec((None, BLK, D), late)]
    out_shape = [jax.ShapeDtypeStruct((bsz, n_tiles * BLK, D), F32)]
    if has_next:
        out_specs.append(pl.BlockSpec((None, BLK, D), late))
        out_shape.append(jax.ShapeDtypeStruct((bsz, n_tiles * BLK, D), BF16))
    return pl.pallas_call(
        functools.partial(_scatter_kernel, ne=ne, with_ctx=with_ctx, has_next=has_next),
        grid=(bsz, ng, n_tiles),
        in_specs=in_specs, out_specs=out_specs, out_shape=out_shape,
        scratch_shapes=[pltpu.VMEM((n_tiles, BLK, D), F32)],
        compiler_params=_cp(3), name="scatter",
    )(*args)


def _pack_w_in(w):
    qa, ka, va, kb, vb, dec, qb, gb, uc, vc = jnp.split(
        w, [int(s) for s in np.cumsum((WA, WA, WA, KB, VB, 2 * DEC_RANK, KB, VB, WC))], axis=-1)
    pad = jnp.zeros((D, NP - C_DEC - 2 * DEC_RANK), w.dtype)
    return jnp.concatenate([vb, gb, qa, ka, va, kb, qb, uc, vc, dec, pad], axis=-1).astype(BF16)


def kernel(x, c, ctx, c_ctx, w_ada, b_ada, g_pre_mix, g_post_mix, g_pre_ffn, g_post_ffn, w_in, w_dec,
           b_dec, rpb, g_gla, ln_v_g, ln_v_b, w_sp, b_sp, w_out, w_router, w_gate, w_up, w_down):
    bsz = x.shape[0]
    c_all = jnp.zeros((16, D), F32).at[:bsz].set(c).at[8].set(c_ctx)
    mods = _ada_call(c_all, w_ada, b_ada).reshape(DEPTH, 16, 6, D)
    lat = mods[:, :bsz]
    cx = jnp.broadcast_to(mods[:, 8:9], lat.shape)
    modtab = jnp.pad(jnp.stack([lat, cx], axis=2), ((0, 0), (0, 0), (0, 0), (0, 2), (0, 0)))

    rope = _rope_tables()
    mstack = jnp.asarray(_gla_matrices(), BF16)
    lev_ids = jnp.asarray(_gla_level_ids())
    hmask = jnp.asarray(np.stack([(np.arange(128) < DH_A), (np.arange(128) >= DH_A)]) * DH_A ** -0.5, F32)
    upper = jnp.asarray(np.triu(np.ones((T, T), np.float32), 1), BF16)

    stream = (x, ctx)
    h = _prenorm_call(x, ctx, g_pre_mix[0], modtab[0])
    for l in range(DEPTH):
        last = l == DEPTH - 1
        nb = NLAT if last else NBLK
        P = _proj_call(h.reshape(bsz * TT, D), _pack_w_in(w_in[l])).reshape(bsz, TT, NP)
        bs_rep = jnp.broadcast_to(b_sp[l][:, :, None], (GC, MIX_CHUNK, 128))
        o_a, o_c = _na_gmlp_call(P, _na_bias(rpb[l]), hmask, ln_v_g[l], ln_v_b[l], w_sp[l], bs_rep, nb)
        wd_pad = jnp.zeros((2, 128, KB), F32)
        wd_pad = wd_pad.at[0, :DEC_RANK].set(w_dec[l, 0]).at[1, DEC_RANK:2 * DEC_RANK].set(w_dec[l, 1])
        o_g = _gla_call(P, wd_pad, b_dec[l].reshape(2, 1, KB), rope, mstack, lev_ids)
        wr_pad = jnp.pad(w_router[l], ((0, 0), (0, 128 - N_EXPERTS)))
        x_mid, h2, lg = _out_call(o_a, o_g, P, o_c, w_out[l].astype(BF16), stream, g_gla[l], g_post_mix[l],
                                  g_pre_ffn[l], modtab[l], wr_pad, nb)
        routed = _route_call(lg, upper, not last)
        slot_l, aff_l, slot_t = routed[:3]
        if last:
            xs_l, gt_l = _gather_call(h2, [(slot_l, aff_l, T, CAP_LAT, 0)])
            (y_l,) = _ffn_call(l, xs_l, gt_l, None, None, w_gate, w_up, w_down)
            (stream,) = _scatter_call(slot_t, y_l, None, x_mid, g_post_ffn[l], modtab[l], None, None)
        else:
            slot_c, aff_c = routed[3:]
            xs_l, gt_l, xs_c, gt_c = _gather_call(h2, [(slot_l, aff_l, T, CAP_LAT, 0),
                                                       (slot_c, aff_c, L, CAP_CTX, NLAT)])
            y_l, y_c = _ffn_call(l, xs_l, gt_l, xs_c, gt_c, w_gate, w_up, w_down)
            stream, h = _scatter_call(slot_t, y_l, y_c, x_mid, g_post_ffn[l], modtab[l], g_pre_mix[l + 1],
                                      modtab[l + 1])
    return stream
```

```python
import functools

import numpy as np
import jax
import jax.numpy as jnp
from jax import lax
from jax.experimental import pallas as pl
from jax.experimental.pallas import tpu as pltpu

F32 = jnp.float32
BF16 = jnp.bfloat16

D = 2048
T = 2048
L = 256
TT = T + L
BLK = 256
NBLK = TT // BLK
NLAT = T // BLK
DEPTH = 2
GRID_W = 64
ROWS = T // GRID_W
HA, DH_A = 8, 64
WIN_ROWS, WIN_COLS = 8, 16
HB, DK_B, DV_B = 4, 128, 256
DEC_RANK = 16
GATE_TAU = 16.0
GC, CG, MIX_CHUNK = 4, 128, 128
N_EXPERTS = 16
EXPERT_FF = 2048
CAP_FACTOR = 2
ROPE_BASE = 10000.0
EPS = 1e-6
LOG2E = 1.4426950408889634
WA, KB, VB, WC = HA * DH_A, HB * DK_B, HB * DV_B, GC * CG

NP = 3 * D
C_VB, C_GB, C_QA, C_KA, C_VA, C_KB, C_QB, C_UC, C_VC, C_DEC = (
    0, 1024, 2048, 2560, 3072, 3584, 4096, 4608, 5120, 5632)

NA_QROWS = 4
NA_KROWS = 12
NA_KEYS = NA_KROWS * GRID_W
NEG = -1e30

GLA_LEVELS = 8
GLA_FINE = (5, 6, 7)
FF_TILE = 256
SCATTER_EXPERTS = 8
GATHER_EXPERTS = 2
CAP_LAT = CAP_FACTOR * T // N_EXPERTS
CAP_CTX = CAP_FACTOR * L // N_EXPERTS
VMEM_LIMIT = 56 * 1024 * 1024


def _cp(n_axes):
    return pltpu.CompilerParams(dimension_semantics=("arbitrary",) * n_axes,
                                vmem_limit_bytes=VMEM_LIMIT)


def _dot(a, b):
    return jnp.dot(a, b, preferred_element_type=F32)


def _dot_nt(a, b):
    return lax.dot_general(a, b, (((1,), (1,)), ((), ())), preferred_element_type=F32)


def _dot_tn(a, b):
    return lax.dot_general(a, b, (((0,), (0,)), ((), ())), preferred_element_type=F32)


def _rms(x):
    return x * lax.rsqrt(jnp.mean(x * x, axis=-1, keepdims=True) + EPS)


def _sigmoid(x):
    return 1.0 / (1.0 + jnp.exp(-x))


def _split_bf16(x):
    hi = x.astype(BF16)
    lo = (x - hi.astype(F32)).astype(BF16)
    return hi, lo


def _ada_kernel(c_ref, w_ref, b_ref, o_ref):
    cv = c_ref[...]
    hi, lo = _split_bf16(cv * _sigmoid(cv))
    r = _dot(jnp.concatenate([hi, lo], axis=0), w_ref[...].astype(BF16))
    o_ref[...] = r[:16] + r[16:] + b_ref[...]


def _ada_call(c_all, w_ada, b_ada):
    tn = 1024
    n6 = w_ada.shape[-1]
    return pl.pallas_call(
        _ada_kernel,
        grid=(DEPTH, n6 // tn),
        in_specs=[pl.BlockSpec((16, D), lambda l, n: (0, 0)),
                  pl.BlockSpec((None, D, tn), lambda l, n: (l, 0, n)),
                  pl.BlockSpec((None, 1, tn), lambda l, n: (l, 0, n))],
        out_specs=pl.BlockSpec((None, 16, tn), lambda l, n: (l, 0, n)),
        out_shape=jax.ShapeDtypeStruct((DEPTH, 16, n6), F32),
        compiler_params=_cp(2), name="ada",
    )(c_all, w_ada, b_ada.reshape(DEPTH, 1, n6))


def _stream_block(x_ref, c_ref):
    return jnp.where(pl.program_id(1) < NLAT, x_ref[...], c_ref[...])


def _stream_specs():
    return [pl.BlockSpec((None, BLK, D), lambda b, j: (b, jnp.minimum(j, NLAT - 1), 0)),
            pl.BlockSpec((None, BLK, D), lambda b, j: (b, 0, 0))]


def _prenorm_kernel(x_ref, c_ref, g_ref, mod_ref, h_ref):
    y = _rms(_stream_block(x_ref, c_ref)) * g_ref[...]
    h_ref[...] = (y * (1.0 + mod_ref[1:2, :]) + mod_ref[0:1, :]).astype(BF16)


def _prenorm_call(x, ctx, g, modtab):
    bsz = x.shape[0]
    return pl.pallas_call(
        _prenorm_kernel,
        grid=(bsz, NBLK),
        in_specs=_stream_specs() + [
            pl.BlockSpec((1, D), lambda b, j: (0, 0)),
            pl.BlockSpec((None, None, 8, D), lambda b, j: (b, j // NLAT, 0, 0))],
        out_specs=pl.BlockSpec((None, BLK, D), lambda b, j: (b, j, 0)),
        out_shape=jax.ShapeDtypeStruct((bsz, TT, D), BF16),
        compiler_params=_cp(2), name="prenorm",
    )(x, ctx, g.reshape(1, D), modtab)


def _proj_kernel(h_ref, w_ref, o_ref):
    o_ref[...] = _dot(h_ref[...], w_ref[...]).astype(BF16)


def _proj_call(h2d, w_pack):
    m = h2d.shape[0]
    tm = 1024 if m % 1024 == 0 else 768
    tn = NP // 3
    return pl.pallas_call(
        _proj_kernel,
        grid=(NP // tn, m // tm),
        in_specs=[pl.BlockSpec((tm, D), lambda n, i: (i, 0)),
                  pl.BlockSpec((D, tn), lambda n, i: (0, n))],
        out_specs=pl.BlockSpec((tm, tn), lambda n, i: (i, n)),
        out_shape=jax.ShapeDtypeStruct((m, NP), BF16),
        compiler_params=_cp(2), name="proj_in",
    )(h2d, w_pack)


def _softmax_pv(s_list, v_list):
    m = s_list[0].max(axis=-1, keepdims=True)
    for s in s_list[1:]:
        m = jnp.maximum(m, s.max(axis=-1, keepdims=True))
    acc = None
    for s, v in zip(s_list, v_list):
        o = _dot(jnp.exp(s - m).astype(BF16), v)
        acc = o if acc is None else acc + o
    return acc[:, :128] / acc[:, 128:]


def _na_kernel(q_ref, k_ref, v_ref, bias_ref, hm_ref, *mlp_refs_o_ref):
    mlp_in, o_ref, oc_ref = mlp_refs_o_ref[:-2], mlp_refs_o_ref[-2], mlp_refs_o_ref[-1]
    j = pl.program_id(1)
    lane = lax.broadcasted_iota(jnp.int32, (BLK, 128), 1)
    low = lane < DH_A

    def run(local_start):
        _gmlp_body(*mlp_in, oc_ref)
        for p in range(HA // 2):
            sl = slice(128 * p, 128 * p + 128)
            q2 = q_ref[:, sl]
            kc = k_ref[T:TT, sl]
            vc = jnp.concatenate([v_ref[T:TT, sl], jnp.ones((L, 128), BF16)], axis=1)
            if local_start is not None:
                kl = k_ref[pl.ds(local_start, NA_KEYS), sl]
                vl = jnp.concatenate([v_ref[pl.ds(local_start, NA_KEYS), sl],
                                      jnp.ones((NA_KEYS, 128), BF16)], axis=1)
            pair = []
            for hh in range(2):
                qm = (q2.astype(F32) * hm_ref[hh:hh + 1, :]).astype(BF16)
                s_ctx = _dot_nt(qm, kc)
                if local_start is not None:
                    s_loc = _dot_nt(qm, kl) + bias_ref[2 * p + hh]
                    pair.append(_softmax_pv([s_loc, s_ctx], [vl, vc]))
                else:
                    pair.append(_softmax_pv([s_ctx], [vc]))
            o_ref[:, sl] = jnp.where(low, pair[0], pair[1]).astype(BF16)

    @pl.when(j < NLAT)
    def _():
        krow = jnp.clip(j * NA_QROWS - WIN_ROWS // 2, 0, ROWS - NA_KROWS)
        run(pl.multiple_of(krow * GRID_W, GRID_W))

    @pl.when(j == NLAT)
    def _():
        run(None)


def _na_gmlp_call(P, bias, hmask, ln_g, ln_b, w_sp, bs_rep, n_blocks):
    bsz = P.shape[0]

    def bias_idx(b, j):
        return (jnp.where(j == 0, 0, jnp.where(j == NLAT - 1, 2, 1)), 0, 0, 0)

    rows = n_blocks * BLK
    return pl.pallas_call(
        _na_kernel,
        grid=(bsz, n_blocks),
        in_specs=[pl.BlockSpec((None, BLK, WA), lambda b, j: (b, j, C_QA // WA)),
                  pl.BlockSpec((None, TT, WA), lambda b, j: (b, 0, C_KA // WA)),
                  pl.BlockSpec((None, TT, WA), lambda b, j: (b, 0, C_VA // WA)),
                  pl.BlockSpec((None, HA, BLK, NA_KEYS), bias_idx),
                  pl.BlockSpec((2, 128), lambda b, j: (0, 0)),
                  pl.BlockSpec((None, BLK, WC), lambda b, j: (b, j, C_UC // WC)),
                  pl.BlockSpec((None, BLK, WC), lambda b, j: (b, j, C_VC // WC)),
                  pl.BlockSpec((1, WC), lambda b, j: (0, 0)),
                  pl.BlockSpec((1, WC), lambda b, j: (0, 0)),
                  pl.BlockSpec((GC, MIX_CHUNK, MIX_CHUNK), lambda b, j: (0, 0, 0)),
                  pl.BlockSpec((GC, MIX_CHUNK, 128), lambda b, j: (0, 0, 0))],
        out_specs=[pl.BlockSpec((None, BLK, WA), lambda b, j: (b, j, 0)),
                   pl.BlockSpec((None, BLK, WC), lambda b, j: (b, j, 0))],
        out_shape=[jax.ShapeDtypeStruct((bsz, rows, WA), BF16),
                   jax.ShapeDtypeStruct((bsz, rows, WC), BF16)],
        compiler_params=_cp(2), name="nbr_attn_gmlp",
    )(P, P, P, bias, hmask, P, P, ln_g.reshape(1, WC), ln_b.reshape(1, WC), w_sp, bs_rep)


def _na_bias(rpb_l):
    n_dr, n_dc = 2 * WIN_ROWS - 1, 2 * WIN_COLS - 1
    cq = np.arange(GRID_W)[:, None]
    ck = np.arange(GRID_W)[None, :]
    cs = np.clip(cq - WIN_COLS // 2, 0, GRID_W - WIN_COLS)
    col_ok = (ck >= cs) & (ck < cs + WIN_COLS)
    dc = np.clip(ck - cq + WIN_COLS - 1, 0, n_dc - 1)
    col_sel = (dc[:, :, None] == np.arange(n_dc)).astype(np.float32)
    colx = jnp.einsum("qke,hde->hdqk", col_sel, rpb_l, precision=lax.Precision.HIGHEST)
    colx = jnp.where(col_ok, colx, NEG).astype(F32)
    return pl.pallas_call(
        _bias_kernel,
        grid=(3, HA),
        in_specs=[pl.BlockSpec((None, n_dr, GRID_W, GRID_W), lambda t, h: (h, 0, 0, 0))],
        out_specs=pl.BlockSpec((None, None, BLK, NA_KEYS), lambda t, h: (t, h, 0, 0)),
        out_shape=jax.ShapeDtypeStruct((3, HA, BLK, NA_KEYS), F32),
        compiler_params=_cp(2), name="na_bias",
    )(colx)


def _bias_kernel(colx_ref, o_ref):
    t = pl.program_id(0)
    neg = jnp.full((GRID_W, GRID_W), NEG, F32)
    for tt, rb in enumerate((0, 1, NLAT - 1)):
        @pl.when(t == tt)
        def _(rb=rb):
            k0 = int(np.clip(rb * NA_QROWS - WIN_ROWS // 2, 0, ROWS - NA_KROWS))
            for qi in range(NA_QROWS):
                r = rb * NA_QROWS + qi
                rs = int(np.clip(r - WIN_ROWS // 2, 0, ROWS - WIN_ROWS))
                for kp in range(NA_KROWS // 2):
                    pair = []
                    for kr in (k0 + 2 * kp, k0 + 2 * kp + 1):
                        inside = rs <= kr < rs + WIN_ROWS
                        pair.append(colx_ref[kr - r + WIN_ROWS - 1] if inside else neg)
                    o_ref[qi * GRID_W:(qi + 1) * GRID_W, kp * 128:(kp + 1) * 128] = (
                        jnp.concatenate(pair, axis=1))


def _gla_matrices():
    n = BLK
    i = np.arange(n)[:, None]
    t = np.arange(n)[None, :]
    out = np.zeros((2, (len(GLA_FINE) + 1) * n, n), np.float32)
    for f, lvl in enumerate(GLA_FINE):
        s = n >> (lvl + 1)
        blk0 = (i // (2 * s)) * (2 * s)
        m = blk0 + s - 1
        fwd = np.where(i > m, (t > m) & (t <= i), (t > i) & (t <= m))
        bwd = np.where(i <= m, (t >= i) & (t <= m), (t > m) & (t < i))
        out[0, f * n:(f + 1) * n] = fwd
        out[1, f * n:(f + 1) * n] = bwd
    out[0, len(GLA_FINE) * n:] = t <= i
    out[1, len(GLA_FINE) * n:] = t >= i
    return out


def _gla_level_ids():
    h = BLK // 2
    i = np.arange(h)[:, None]
    j = np.arange(h)[None, :]
    x = np.maximum(i ^ j, 1)
    lvl = GLA_LEVELS - 1 - np.floor(np.log2(x)).astype(np.int32)
    fwd = np.where(i == j, -1, np.where(i > j, lvl, -2))
    bwd = np.where(i == j, -1, np.where(i < j, lvl, -2))
    return np.stack([fwd, bwd]).astype(np.int32)


def _gla_body(rev, kb_ref, qb_ref, vb_ref, dec_ref, wd_ref, bd_ref, rope_ref, m_ref, lev_ref, o_ref,
              st_ref):
    half = BLK // 2
    z = _dot(dec_ref[...], wd_ref[...].astype(BF16)) + bd_ref[...]
    la = (jnp.minimum(z, 0.0) - jnp.log1p(jnp.exp(-jnp.abs(z)))) * (LOG2E / GATE_TAU)
    la_b = la.astype(BF16)

    def seg_sum(f):
        return _dot(m_ref[f * BLK:(f + 1) * BLK, :], la_b)

    run = seg_sum(len(GLA_FINE))
    e_in = jnp.exp2(run)
    e_out = jnp.exp2((run[0:1] if rev else run[BLK - 1:BLK]) - run)
    e_fine = {lvl: jnp.exp2(seg_sum(f)) for f, lvl in enumerate(GLA_FINE)}

    def slab_decay(i, s, ks):
        base = (i // 2) * 2 * s
        ref = base + s if rev else base + s - 1
        d = run[i * s:(i + 1) * s, ks] - run[ref:ref + 1, ks]
        return jnp.exp2(-d if ((i % 2 == 0) != rev) else d)

    row = lax.broadcasted_iota(jnp.int32, (BLK, DK_B), 0)
    lev = lev_ref[...]
    on_diag = lev == -1
    at_level = {lvl: lev == lvl for lvl in range(1, GLA_LEVELS)}
    cosv, sav, sbv = rope_ref[0], rope_ref[1], rope_ref[2]

    def rope(x):
        return x * cosv + pltpu.roll(x, DK_B - 32, 1) * sav + pltpu.roll(x, 32, 1) * sbv

    qi, ki = (0, 1) if rev else (1, 0)
    q_rows = slice(qi * half, (qi + 1) * half)
    k_rows = slice(ki * half, (ki + 1) * half)

    outs = []
    for hh in range(HB):
        ks = slice(DK_B * hh, DK_B * (hh + 1))
        q = rope(qb_ref[:, ks].astype(F32)) * (DK_B ** -0.5)
        k = rope(kb_ref[:, ks].astype(F32))
        v = vb_ref[:, DV_B * hh:DV_B * (hh + 1)]
        self_w = jnp.sum(q * k, axis=-1, keepdims=True)
        cross = _dot_nt((q[q_rows] * slab_decay(qi, half, ks)).astype(BF16),
                        (k[k_rows] * slab_decay(ki, half, ks)).astype(BF16))
        diag = [jnp.where(on_diag, self_w[c * half:(c + 1) * half], 0.0) for c in range(2)]
        for lvl in range(1, GLA_LEVELS):
            s = BLK >> (lvl + 1)
            if lvl not in GLA_FINE:
                parts = []
                for i in range(BLK // s):
                    src = q if ((i % 2 == 1) != rev) else k
                    parts.append(src[i * s:(i + 1) * s] * slab_decay(i, s, ks))
                x = jnp.concatenate(parts, axis=0)
            else:
                second = ((row // s) & 1) == 1
                x = (jnp.where(second, k, q) if rev else jnp.where(second, q, k)) * e_fine[lvl][:, ks]
            xb = x.astype(BF16)
            for c in range(2):
                xc = xb[c * half:(c + 1) * half]
                diag[c] = jnp.where(at_level[lvl], _dot_nt(xc, xc), diag[c])
        zero = jnp.zeros((half, half), F32)
        if rev:
            att = jnp.concatenate([jnp.concatenate([diag[0], cross], axis=1),
                                   jnp.concatenate([zero, diag[1]], axis=1)], axis=0)
        else:
            att = jnp.concatenate([jnp.concatenate([diag[0], zero], axis=1),
                                   jnp.concatenate([cross, diag[1]], axis=1)], axis=0)
        e_q = e_in[:, ks]
        st = st_ref[hh]
        o = _dot_nt((q * e_q).astype(BF16), st.astype(BF16)) + _dot(att.astype(BF16), v)
        outs.append(o.astype(BF16))
        carry = e_q[0:1, :] if rev else e_q[BLK - 1:BLK, :]
        kt = (k * e_out[:, ks]).astype(BF16)
        st_ref[hh] = st * carry + _dot_tn(v, kt)
    o_ref[...] = jnp.concatenate(outs, axis=-1)


def _gla_kernel(*refs):
    st_ref = refs[-1]
    d = pl.program_id(1)

    @pl.when(pl.program_id(2) == 0)
    def _():
        st_ref[...] = jnp.zeros_like(st_ref)

    @pl.when(d == 0)
    def _():
        _gla_body(False, *refs)

    @pl.when(d == 1)
    def _():
        _gla_body(True, *refs)


def _gla_call(P, wd_pad, bd, rope, mstack, lev_ids):
    bsz = P.shape[0]

    def blk(dd, s):
        return jnp.where(s == 0, NLAT, jnp.where(dd == 0, s - 1, NLAT - s))

    return pl.pallas_call(
        _gla_kernel,
        grid=(bsz, 2, NBLK),
        in_specs=[
            pl.BlockSpec((None, BLK, KB), lambda b, dd, s: (b, blk(dd, s), C_KB // KB)),
            pl.BlockSpec((None, BLK, KB), lambda b, dd, s: (b, blk(dd, s), C_QB // KB)),
            pl.BlockSpec((None, BLK, VB), lambda b, dd, s: (b, blk(dd, s), C_VB // VB)),
            pl.BlockSpec((None, BLK, 128), lambda b, dd, s: (b, blk(dd, s), C_DEC // 128)),
            pl.BlockSpec((None, 128, KB), lambda b, dd, s: (dd, 0, 0)),
            pl.BlockSpec((None, 1, KB), lambda b, dd, s: (dd, 0, 0)),
            pl.BlockSpec((3, BLK, DK_B), lambda b, dd, s: (0, blk(dd, s), 0)),
            pl.BlockSpec((None, (len(GLA_FINE) + 1) * BLK, BLK), lambda b, dd, s: (dd, 0, 0)),
            pl.BlockSpec((None, BLK // 2, BLK // 2), lambda b, dd, s: (dd, 0, 0)),
        ],
        out_specs=pl.BlockSpec((None, None, BLK, VB), lambda b, dd, s: (dd, b, blk(dd, s), 0)),
        out_shape=jax.ShapeDtypeStruct((2, bsz, TT, VB), BF16),
        scratch_shapes=[pltpu.VMEM((HB, DV_B, DK_B), F32)],
        compiler_params=_cp(3), name="gla",
    )(P, P, P, P, wd_pad, bd, rope, mstack, lev_ids)


def _rope_tables():
    t = np.arange(T)
    half = DK_B // 2
    inv = np.float32(ROPE_BASE) ** (-np.arange(0, half, 2, dtype=np.float32) / np.float32(half))

    def tab(pos):
        ang = pos.astype(np.float32)[:, None] * inv[None, :]
        return np.concatenate([ang, ang], axis=-1)

    ang = np.concatenate([tab(t // GRID_W), tab(t % GRID_W)], axis=-1)
    cos, sin = np.cos(ang), np.sin(ang)
    first = (np.arange(DK_B) % half) < (half // 2)
    sa = np.where(first[None, :], -sin, 0.0)
    sb = np.where(first[None, :], 0.0, sin)
    ident = np.ones((L, DK_B), np.float32)
    zero = np.zeros((L, DK_B), np.float32)
    tabs = np.stack([np.concatenate([cos, ident]), np.concatenate([sa, zero]), np.concatenate([sb, zero])])
    return jnp.asarray(tabs, F32)


def _gelu(x):
    return 0.5 * x * (1.0 + jnp.tanh(0.7978845608028654 * (x + 0.044715 * (x * x * x))))


def _gmlp_body(u_ref, v_ref, g_ref, b_ref, w_ref, bs_ref, o_ref):
    for ch in range(BLK // MIX_CHUNK):
        rs = slice(ch * MIX_CHUNK, (ch + 1) * MIX_CHUNK)
        u = _gelu(u_ref[rs, :].astype(F32))
        v = _gelu(v_ref[rs, :].astype(F32))
        for g in range(GC):
            cs = slice(g * CG, (g + 1) * CG)
            vg = v[:, cs]
            mu = jnp.mean(vg, axis=-1, keepdims=True)
            var = jnp.mean(jnp.square(vg - mu), axis=-1, keepdims=True)
            vn = (vg - mu) * lax.rsqrt(var + EPS) * g_ref[:, cs] + b_ref[:, cs]
            s = _dot(w_ref[g].astype(BF16), vn.astype(BF16)) + bs_ref[g]
            o_ref[rs, cs] = (u[:, cs] * s).astype(BF16)


def _out_kernel(*refs, split):
    oa_ref, og_ref, gb_ref, oc_ref, w_ref = refs[:5]
    if split:
        x_in = _stream_block(refs[5], refs[6])
    else:
        x_in = refs[5][...]
    gg_ref, gpost_ref, gpre_ref, mod_ref, wr_ref, xo_ref, h2_ref, lg_ref = refs[6 + split:]
    o = og_ref[0].astype(F32) + og_ref[1].astype(F32)
    gb = gb_ref[...].astype(F32)
    parts = []
    for h in range(HB):
        cs = slice(h * DV_B, (h + 1) * DV_B)
        gh = gb[:, cs]
        parts.append((_rms(o[:, cs]) * gg_ref[:, cs] * (gh * _sigmoid(gh))).astype(BF16))
    mixed = jnp.concatenate([oa_ref[...]] + parts + [oc_ref[...]], axis=-1)
    y = _dot(mixed, w_ref[...])
    x1 = x_in + mod_ref[2:3, :] * (_rms(y) * gpost_ref[...])
    xo_ref[...] = x1
    h2 = ((_rms(x1) * gpre_ref[...]) * (1.0 + mod_ref[4:5, :]) + mod_ref[3:4, :]).astype(BF16)
    h2_ref[...] = h2
    wh, wl = _split_bf16(wr_ref[...])
    lg_ref[...] = _dot(h2, wh) + _dot(h2, wl)


def _out_call(o_a, o_g, P, o_c, w_out_b, stream, g_gla, g_post, g_pre, modtab, wr_pad, n_blocks):
    bsz = P.shape[0]
    rows = n_blocks * BLK
    split = isinstance(stream, tuple)
    rowspec = lambda w: pl.BlockSpec((None, BLK, w), lambda b, j: (b, j, 0))
    vec = lambda w: pl.BlockSpec((1, w), lambda b, j: (0, 0))
    return pl.pallas_call(
        functools.partial(_out_kernel, split=split),
        grid=(bsz, n_blocks),
        in_specs=[rowspec(WA),
                  pl.BlockSpec((2, None, BLK, VB), lambda b, j: (0, b, j, 0)),
                  pl.BlockSpec((None, BLK, VB), lambda b, j: (b, j, C_GB // VB)),
                  rowspec(WC),
                  pl.BlockSpec((D, D), lambda b, j: (0, 0))]
                 + (_stream_specs() if split else [rowspec(D)])
                 + [vec(VB), vec(D), vec(D),
                    pl.BlockSpec((None, None, 8, D), lambda b, j: (b, j // NLAT, 0, 0)),
                    pl.BlockSpec((D, 128), lambda b, j: (0, 0))],
        out_specs=[rowspec(D), rowspec(D), rowspec(128)],
        out_shape=[jax.ShapeDtypeStruct((bsz, rows, D), F32),
                   jax.ShapeDtypeStruct((bsz, rows, D), BF16),
                   jax.ShapeDtypeStruct((bsz, rows, 128), F32)],
        compiler_params=_cp(2), name="proj_out",
    )(o_a, o_g, P, o_c, w_out_b, *(stream if split else (stream,)), g_gla.reshape(1, VB),
      g_post.reshape(1, D), g_pre.reshape(1, D), modtab, wr_pad)


def _route_set(lg, cap, upper):
    n = lg.shape[0]
    lt = lg.T[:N_EXPERTS, :]
    ex = jnp.exp(lt - lt.max(axis=0, keepdims=True))
    aff = ex / ex.sum(axis=0, keepdims=True)
    bits = pltpu.bitcast(aff, jnp.int32)
    capf = jnp.float32(cap)

    def keep(cand, prefix):
        cnt = jnp.sum(jnp.where(bits >= cand, 1.0, 0.0), axis=1, keepdims=True)
        return jnp.where(cnt >= capf, cand, prefix)

    def two_bits(i, prefix):
        lo = jnp.left_shift(jnp.int32(1), 29 - 2 * i)
        best = keep(prefix | lo, prefix)
        best = keep(prefix | (lo + lo), best)
        return keep(prefix | (lo + lo + lo), best)

    thr = lax.fori_loop(0, 15, two_bits, jnp.zeros((N_EXPERTS, 1), jnp.int32))
    thr = keep(thr | 1, thr)
    gt = jnp.where(bits > thr, 1.0, 0.0)
    eq = jnp.where(bits == thr, 1.0, 0.0)
    need = capf - gt.sum(axis=1, keepdims=True)
    rank_eq = _dot(eq.astype(BF16), upper)
    sel = gt + eq * jnp.where(rank_eq < need, 1.0, 0.0)
    slot = _dot(sel.astype(BF16), upper)
    slot = jnp.where(sel > 0.5, slot, -1.0)
    pad = jnp.full((128 - SCATTER_EXPERTS, n), -1.0, F32)
    slot_t = [jnp.concatenate([slot[g * SCATTER_EXPERTS:(g + 1) * SCATTER_EXPERTS], pad], axis=0).T
              for g in range(N_EXPERTS // SCATTER_EXPERTS)]
    return slot, aff, slot_t


def _route_kernel(lg_ref, u_ref, *out_refs, with_ctx):
    sl, af, st = _route_set(lg_ref[0:T, :], CAP_LAT, u_ref[...])
    out_refs[0][...] = sl
    out_refs[1][...] = af
    for g, s in enumerate(st):
        out_refs[2][g, 0:T, :] = s
    if with_ctx:
        sl, af, st = _route_set(lg_ref[T:TT, :], CAP_CTX, u_ref[0:L, 0:L])
        out_refs[3][...] = sl
        out_refs[4][...] = af
        for g, s in enumerate(st):
            out_refs[2][g, T:TT, :] = s


def _route_call(lg, upper, with_ctx):
    bsz, rows, _ = lg.shape
    ng = N_EXPERTS // SCATTER_EXPERTS
    en = lambda n: pl.BlockSpec((None, N_EXPERTS, n), lambda b: (b, 0, 0))
    out_specs = [en(T), en(T), pl.BlockSpec((None, ng, rows, 128), lambda b: (b, 0, 0, 0))]
    out_shape = [jax.ShapeDtypeStruct((bsz, N_EXPERTS, T), F32),
                 jax.ShapeDtypeStruct((bsz, N_EXPERTS, T), F32),
                 jax.ShapeDtypeStruct((bsz, ng, rows, 128), F32)]
    if with_ctx:
        out_specs += [en(L), en(L)]
        out_shape += [jax.ShapeDtypeStruct((bsz, N_EXPERTS, L), F32)] * 2
    return pl.pallas_call(
        functools.partial(_route_kernel, with_ctx=with_ctx),
        grid=(bsz,),
        in_specs=[pl.BlockSpec((None, rows, 128), lambda b: (b, 0, 0)),
                  pl.BlockSpec((T, T), lambda b: (0, 0))],
        out_specs=out_specs, out_shape=out_shape,
        compiler_params=_cp(1), name="route",
    )(lg, upper)


def _gather_kernel(*refs, n_sets):
    ins, outs = refs[:3 * n_sets], refs[3 * n_sets:]
    for k in range(n_sets):
        slot_ref, aff_ref, h_ref = ins[3 * k:3 * k + 3]
        xs_ref, g_ref = outs[2 * k:2 * k + 2]
        cap, n = xs_ref.shape[1], h_ref.shape[0]
        sid = lax.broadcasted_iota(jnp.int32, (cap, n), 0).astype(F32)
        picks = []
        for j in range(GATHER_EXPERTS):
            e = pl.program_id(1) * GATHER_EXPERTS + j
            hit = sid == slot_ref[pl.ds(e, 1), :]
            picks.append(jnp.where(hit, 1.0, 0.0).astype(BF16))
            g = jnp.sum(jnp.where(hit, aff_ref[pl.ds(e, 1), :], 0.0), axis=1, keepdims=True)
            g_ref[j] = jnp.broadcast_to(g, (cap, 128))
        xs = _dot(jnp.concatenate(picks, axis=0), h_ref[...]).astype(BF16)
        xs_ref[...] = xs.reshape(GATHER_EXPERTS, cap, D)


def _gather_call(h2, sets):
    bsz = h2.shape[0]
    in_specs, args, out_specs, out_shape = [], [], [], []
    for slot, aff, n, cap, blk_idx in sets:
        in_specs += [pl.BlockSpec((None, N_EXPERTS, n), lambda b, e: (b, 0, 0)),
                     pl.BlockSpec((None, N_EXPERTS, n), lambda b, e: (b, 0, 0)),
                     pl.BlockSpec((None, n, D), lambda b, e, i=blk_idx: (b, i, 0))]
        args += [slot, aff, h2]
        out_specs += [pl.BlockSpec((GATHER_EXPERTS, cap, D), lambda b, e: (e, b, 0)),
                      pl.BlockSpec((GATHER_EXPERTS, cap, 128), lambda b, e: (e, b, 0))]
        out_shape += [jax.ShapeDtypeStruct((N_EXPERTS, bsz * cap, D), BF16),
                      jax.ShapeDtypeStruct((N_EXPERTS, bsz * cap, 128), F32)]
    return pl.pallas_call(
        functools.partial(_gather_kernel, n_sets=len(sets)),
        grid=(bsz, N_EXPERTS // GATHER_EXPERTS),
        in_specs=in_specs, out_specs=out_specs, out_shape=out_shape,
        compiler_params=_cp(2), name="gather",
    )(*args)


def _ffn_kernel(*refs, rl, rc, nf):
    if rc:
        xl_ref, xc_ref, gl_ref, gc_ref, wg_ref, wu_ref, wd_ref, yl_ref, yc_ref, hid = refs
        groups = ((xl_ref, gl_ref, yl_ref, 0, rl), (xc_ref, gc_ref, yc_ref, rl, rc))
    else:
        xl_ref, gl_ref, wg_ref, wu_ref, wd_ref, yl_ref, hid = refs
        groups = ((xl_ref, gl_ref, yl_ref, 0, rl),)
    s = pl.program_id(1)

    @pl.when(s < nf)
    def _():
        wg = wg_ref[...].astype(BF16)
        wu = wu_ref[...].astype(BF16)
        for x_ref, _, _, r0, nr in groups:
            xv = x_ref[...]
            a = _dot(xv, wg)
            u = _dot(xv, wu)
            hid[s, r0:r0 + nr, :] = ((a * _sigmoid(a)) * u).astype(BF16)

    @pl.when(s >= nf)
    def _():
        wd = wd_ref[...].astype(BF16)
        for _, g_ref, y_ref, r0, nr in groups:
            hm = jnp.concatenate([hid[k, r0:r0 + nr, :] for k in range(nf)], axis=1)
            y = _dot(hm, wd)
            for c in range(FF_TILE // 128):
                cs = slice(128 * c, 128 * (c + 1))
                y_ref[:, cs] = (y[:, cs] * g_ref[...]).astype(BF16)


def _ffn_call(layer, xs_l, g_l, xs_c, g_c, w_gate, w_up, w_down):
    rl = xs_l.shape[1]
    rc = 0 if xs_c is None else xs_c.shape[1]
    nf = EXPERT_FF // FF_TILE
    nd = D // FF_TILE
    up = lambda e, s: (layer, e, 0, jnp.minimum(s, nf - 1))
    down = lambda s: jnp.maximum(s - nf, 0)
    in_specs = [pl.BlockSpec((None, rl, D), lambda e, s: (e, 0, 0))]
    args = [xs_l]
    if rc:
        in_specs.append(pl.BlockSpec((None, rc, D), lambda e, s: (e, 0, 0)))
        args.append(xs_c)
    in_specs.append(pl.BlockSpec((None, rl, 128), lambda e, s: (e, 0, 0)))
    args.append(g_l)
    if rc:
        in_specs.append(pl.BlockSpec((None, rc, 128), lambda e, s: (e, 0, 0)))
        args.append(g_c)
    in_specs += [pl.BlockSpec((None, None, D, FF_TILE), up),
                 pl.BlockSpec((None, None, D, FF_TILE), up),
                 pl.BlockSpec((None, None, EXPERT_FF, FF_TILE), lambda e, s: (layer, e, 0, down(s)))]
    args += [w_gate, w_up, w_down]
    out_specs = [pl.BlockSpec((None, rl, FF_TILE), lambda e, s: (e, 0, down(s)))]
    out_shape = [jax.ShapeDtypeStruct((N_EXPERTS, rl, D), BF16)]
    if rc:
        out_specs.append(pl.BlockSpec((None, rc, FF_TILE), lambda e, s: (e, 0, down(s))))
        out_shape.append(jax.ShapeDtypeStruct((N_EXPERTS, rc, D), BF16))
    return pl.pallas_call(
        functools.partial(_ffn_kernel, rl=rl, rc=rc, nf=nf),
        grid=(N_EXPERTS, nf + nd),
        in_specs=in_specs, out_specs=out_specs, out_shape=out_shape,
        scratch_shapes=[pltpu.VMEM((nf, rl + rc, FF_TILE), BF16)],
        compiler_params=_cp(2), name="expert_ffn",
    )(*args)


def _scatter_kernel(*refs, ne, with_ctx, has_next):
    refs = list(refs)
    slot_ref, yl_ref = refs[:2]
    refs = refs[2:]
    yc_ref = refs.pop(0) if with_ctx else None
    x_ref, gpost_ref, mod_ref = refs[:3]
    refs = refs[3:]
    if has_next:
        gpre_ref, modn_ref = refs[:2]
        refs = refs[2:]
    xo_ref = refs[0]
    hn_ref = refs[1] if has_next else None
    acc = refs[-1]
    eg = pl.program_id(1)
    r = pl.program_id(2)
    ng = pl.num_programs(1)

    def add(cap, y_ref):
        kk = ne * cap
        if cap % 128 == 0:
            want = lax.broadcasted_iota(jnp.int32, (BLK, cap), 1).astype(F32)
            pt = jnp.concatenate([jnp.where(slot_ref[:, j:j + 1] == want, 1.0, 0.0).astype(BF16)
                                  for j in range(ne)], axis=1)
        else:
            er = lax.broadcasted_iota(jnp.int32, (128, kk), 0)
            ec = lax.broadcasted_iota(jnp.int32, (128, kk), 1)
            expand = jnp.where(er == ec // cap, 1.0, 0.0).astype(BF16)
            spread = _dot(slot_ref[...].astype(BF16), expand)
            want = (lax.broadcasted_iota(jnp.int32, (BLK, kk), 1) % cap).astype(F32)
            pt = jnp.where(spread == want, 1.0, 0.0).astype(BF16)
        acc[r] += _dot(pt, y_ref[...].reshape(kk, D))

    @pl.when(eg == 0)
    def _():
        acc[r] = jnp.zeros((BLK, D), F32)

    @pl.when(r < NLAT)
    def _():
        add(CAP_LAT, yl_ref)

    if with_ctx:
        @pl.when(r == NLAT)
        def _():
            add(CAP_CTX, yc_ref)

    @pl.when(eg == ng - 1)
    def _():
        x2 = x_ref[...] + mod_ref[5:6, :] * (_rms(acc[r]) * gpost_ref[...])
        xo_ref[...] = x2
        if has_next:
            hn = (_rms(x2) * gpre_ref[...]) * (1.0 + modn_ref[1:2, :]) + modn_ref[0:1, :]
            hn_ref[...] = hn.astype(BF16)


def _scatter_call(slot_t, y_l, y_c, x_mid, g_post, modtab, g_pre_next, modtab_next):
    bsz = x_mid.shape[0]
    with_ctx = y_c is not None
    has_next = g_pre_next is not None
    n_tiles = NBLK if with_ctx else NLAT
    ne = SCATTER_EXPERTS
    ng = N_EXPERTS // ne
    vec = pl.BlockSpec((1, D), lambda b, g, r: (0, 0))
    modspec = pl.BlockSpec((None, None, 8, D), lambda b, g, r: (b, r // NLAT, 0, 0))
    late = lambda b, g, r: (b, jnp.where(g == ng - 1, r, 0), 0)
    in_specs = [pl.BlockSpec((None, None, BLK, 128), lambda b, g, r: (b, g, r, 0)),
                pl.BlockSpec((ne, CAP_LAT, D), lambda b, g, r: (g, b, 0))]
    args = [slot_t, y_l]
    if with_ctx:
        in_specs.append(pl.BlockSpec((ne, CAP_CTX, D), lambda b, g, r: (g, b, 0)))
        args.append(y_c)
    in_specs += [pl.BlockSpec((None, BLK, D), late), vec, modspec]
    args += [x_mid, g_post.reshape(1, D), modtab]
    if has_next:
        in_specs += [vec, modspec]
        args += [g_pre_next.reshape(1, D), modtab_next]
    out_specs = [pl.BlockSpec((None, BLK, D), late)]
    out_shape = [jax.ShapeDtypeStruct((bsz, n_tiles * BLK, D), F32)]
    if has_next:
        out_specs.append(pl.BlockSpec((None, BLK, D), late))
        out_shape.append(jax.ShapeDtypeStruct((bsz, n_tiles * BLK, D), BF16))
    return pl.pallas_call(
        functools.partial(_scatter_kernel, ne=ne, with_ctx=with_ctx, has_next=has_next),
        grid=(bsz, ng, n_tiles),
        in_specs=in_specs, out_specs=out_specs, out_shape=out_shape,
        scratch_shapes=[pltpu.VMEM((n_tiles, BLK, D), F32)],
        compiler_params=_cp(3), name="scatter",
    )(*args)


def _pack_kernel(w_ref, o_ref):
    src = np.cumsum((0, WA, WA, WA, KB, VB, 2 * DEC_RANK, KB, VB, WC, WC))
    order = ((4, C_VB), (7, C_GB), (0, C_QA), (1, C_KA), (2, C_VA), (3, C_KB), (6, C_QB), (8, C_UC),
             (9, C_VC), (5, C_DEC))
    for seg, dst in order:
        lo, hi = int(src[seg]), int(src[seg + 1])
        o_ref[:, dst:dst + hi - lo] = w_ref[:, lo:hi].astype(BF16)
    tail = C_DEC + 2 * DEC_RANK
    o_ref[:, tail:] = jnp.zeros((o_ref.shape[0], NP - tail), BF16)


def _pack_w_in(w_in, layer):
    rows = 256
    n_in = w_in.shape[-1]
    return pl.pallas_call(
        _pack_kernel,
        grid=(D // rows,),
        in_specs=[pl.BlockSpec((None, rows, n_in), lambda i: (layer, i, 0))],
        out_specs=pl.BlockSpec((rows, NP), lambda i: (i, 0)),
        out_shape=jax.ShapeDtypeStruct((D, NP), BF16),
        compiler_params=_cp(1), name="pack_w_in",
    )(w_in)


def kernel(x, c, ctx, c_ctx, w_ada, b_ada, g_pre_mix, g_post_mix, g_pre_ffn, g_post_ffn, w_in, w_dec,
           b_dec, rpb, g_gla, ln_v_g, ln_v_b, w_sp, b_sp, w_out, w_router, w_gate, w_up, w_down):
    bsz = x.shape[0]
    c_all = jnp.zeros((16, D), F32).at[:bsz].set(c).at[8].set(c_ctx)
    mods = _ada_call(c_all, w_ada, b_ada).reshape(DEPTH, 16, 6, D)
    lat = mods[:, :bsz]
    cx = jnp.broadcast_to(mods[:, 8:9], lat.shape)
    modtab = jnp.pad(jnp.stack([lat, cx], axis=2), ((0, 0), (0, 0), (0, 0), (0, 2), (0, 0)))

    rope = _rope_tables()
    mstack = jnp.asarray(_gla_matrices(), BF16)
    lev_ids = jnp.asarray(_gla_level_ids())
    hmask = jnp.asarray(np.stack([(np.arange(128) < DH_A), (np.arange(128) >= DH_A)]) * DH_A ** -0.5, F32)
    upper = jnp.asarray(np.triu(np.ones((T, T), np.float32), 1), BF16)

    stream = (x, ctx)
    h = _prenorm_call(x, ctx, g_pre_mix[0], modtab[0])
    for l in range(DEPTH):
        last = l == DEPTH - 1
        nb = NLAT if last else NBLK
        P = _proj_call(h.reshape(bsz * TT, D), _pack_w_in(w_in, l)).reshape(bsz, TT, NP)
        bs_rep = jnp.broadcast_to(b_sp[l][:, :, None], (GC, MIX_CHUNK, 128))
        o_a, o_c = _na_gmlp_call(P, _na_bias(rpb[l]), hmask, ln_v_g[l], ln_v_b[l], w_sp[l], bs_rep, nb)
        wd_pad = jnp.zeros((2, 128, KB), F32)
        wd_pad = wd_pad.at[0, :DEC_RANK].set(w_dec[l, 0]).at[1, DEC_RANK:2 * DEC_RANK].set(w_dec[l, 1])
        o_g = _gla_call(P, wd_pad, b_dec[l].reshape(2, 1, KB), rope, mstack, lev_ids)
        wr_pad = jnp.pad(w_router[l], ((0, 0), (0, 128 - N_EXPERTS)))
        x_mid, h2, lg = _out_call(o_a, o_g, P, o_c, w_out[l].astype(BF16), stream, g_gla[l], g_post_mix[l],
                                  g_pre_ffn[l], modtab[l], wr_pad, nb)
        routed = _route_call(lg, upper, not last)
        slot_l, aff_l, slot_t = routed[:3]
        if last:
            xs_l, gt_l = _gather_call(h2, [(slot_l, aff_l, T, CAP_LAT, 0)])
            (y_l,) = _ffn_call(l, xs_l, gt_l, None, None, w_gate, w_up, w_down)
            (stream,) = _scatter_call(slot_t, y_l, None, x_mid, g_post_ffn[l], modtab[l], None, None)
        else:
            slot_c, aff_c = routed[3:]
            xs_l, gt_l, xs_c, gt_c = _gather_call(h2, [(slot_l, aff_l, T, CAP_LAT, 0),
                                                       (slot_c, aff_c, L, CAP_CTX, NLAT)])
            y_l, y_c = _ffn_call(l, xs_l, gt_l, xs_c, gt_c, w_gate, w_up, w_down)
            stream, h = _scatter_call(slot_t, y_l, y_c, x_mid, g_post_ffn[l], modtab[l], g_pre_mix[l + 1],
                                      modtab[l + 1])
    return stream
```

```python
import functools

import numpy as np
import jax
import jax.numpy as jnp
from jax import lax
from jax.experimental import pallas as pl
from jax.experimental.pallas import tpu as pltpu

F32 = jnp.float32
BF16 = jnp.bfloat16

D = 2048
T = 2048
L = 256
TT = T + L
BLK = 256
NBLK = TT // BLK
NLAT = T // BLK
DEPTH = 2
GRID_W = 64
ROWS = T // GRID_W
HA, DH_A = 8, 64
WIN_ROWS, WIN_COLS = 8, 16
HB, DK_B, DV_B = 4, 128, 256
DEC_RANK = 16
GATE_TAU = 16.0
GC, CG, MIX_CHUNK = 4, 128, 128
N_EXPERTS = 16
EXPERT_FF = 2048
CAP_FACTOR = 2
ROPE_BASE = 10000.0
EPS = 1e-6
LOG2E = 1.4426950408889634
WA, KB, VB, WC = HA * DH_A, HB * DK_B, HB * DV_B, GC * CG

NP = 3 * D
C_VB, C_GB, C_QA, C_KA, C_VA, C_KB, C_QB, C_UC, C_VC, C_DEC = (
    0, 1024, 2048, 2560, 3072, 3584, 4096, 4608, 5120, 5632)

NA_QROWS = 4
NA_KROWS = 12
NA_KEYS = NA_KROWS * GRID_W
NEG = -1e30

GLA_LEVELS = 8
GLA_FINE = (5, 6, 7)
FF_TILE = 256
SCATTER_EXPERTS = 8
SCATTER_WIN = 64
GATHER_EXPERTS = 2
CAP_LAT = CAP_FACTOR * T // N_EXPERTS
CAP_CTX = CAP_FACTOR * L // N_EXPERTS
VMEM_LIMIT = 56 * 1024 * 1024


def _cp(n_axes):
    return pltpu.CompilerParams(dimension_semantics=("arbitrary",) * n_axes,
                                vmem_limit_bytes=VMEM_LIMIT)


def _dot(a, b):
    return jnp.dot(a, b, preferred_element_type=F32)


def _dot_nt(a, b):
    return lax.dot_general(a, b, (((1,), (1,)), ((), ())), preferred_element_type=F32)


def _dot_tn(a, b):
    return lax.dot_general(a, b, (((0,), (0,)), ((), ())), preferred_element_type=F32)


def _rms(x):
    return x * lax.rsqrt(jnp.mean(x * x, axis=-1, keepdims=True) + EPS)


def _sigmoid(x):
    return 1.0 / (1.0 + jnp.exp(-x))


def _split_bf16(x):
    hi = x.astype(BF16)
    lo = (x - hi.astype(F32)).astype(BF16)
    return hi, lo


def _ada_kernel(c_ref, w_ref, b_ref, o_ref):
    cv = c_ref[...]
    hi, lo = _split_bf16(cv * _sigmoid(cv))
    r = _dot(jnp.concatenate([hi, lo], axis=0), w_ref[...].astype(BF16))
    o_ref[...] = r[:16] + r[16:] + b_ref[...]


def _ada_call(c_all, w_ada, b_ada):
    tn = 1024
    n6 = w_ada.shape[-1]
    return pl.pallas_call(
        _ada_kernel,
        grid=(DEPTH, n6 // tn),
        in_specs=[pl.BlockSpec((16, D), lambda l, n: (0, 0)),
                  pl.BlockSpec((None, D, tn), lambda l, n: (l, 0, n)),
                  pl.BlockSpec((None, 1, tn), lambda l, n: (l, 0, n))],
        out_specs=pl.BlockSpec((None, 16, tn), lambda l, n: (l, 0, n)),
        out_shape=jax.ShapeDtypeStruct((DEPTH, 16, n6), F32),
        compiler_params=_cp(2), name="ada",
    )(c_all, w_ada, b_ada.reshape(DEPTH, 1, n6))


def _stream_block(x_ref, c_ref):
    return jnp.where(pl.program_id(1) < NLAT, x_ref[...], c_ref[...])


def _stream_specs():
    return [pl.BlockSpec((None, BLK, D), lambda b, j: (b, jnp.minimum(j, NLAT - 1), 0)),
            pl.BlockSpec((None, BLK, D), lambda b, j: (b, 0, 0))]


def _prenorm_kernel(x_ref, c_ref, g_ref, mod_ref, h_ref):
    y = _rms(_stream_block(x_ref, c_ref)) * g_ref[...]
    h_ref[...] = (y * (1.0 + mod_ref[1:2, :]) + mod_ref[0:1, :]).astype(BF16)


def _prenorm_call(x, ctx, g, modtab):
    bsz = x.shape[0]
    return pl.pallas_call(
        _prenorm_kernel,
        grid=(bsz, NBLK),
        in_specs=_stream_specs() + [
            pl.BlockSpec((1, D), lambda b, j: (0, 0)),
            pl.BlockSpec((None, None, 8, D), lambda b, j: (b, j // NLAT, 0, 0))],
        out_specs=pl.BlockSpec((None, BLK, D), lambda b, j: (b, j, 0)),
        out_shape=jax.ShapeDtypeStruct((bsz, TT, D), BF16),
        compiler_params=_cp(2), name="prenorm",
    )(x, ctx, g.reshape(1, D), modtab)


def _proj_kernel(h_ref, w_ref, o_ref):
    o_ref[...] = _dot(h_ref[...], w_ref[...]).astype(BF16)


def _proj_call(h2d, w_pack):
    m = h2d.shape[0]
    tm = 1024 if m % 1024 == 0 else 768
    tn = NP // 3
    return pl.pallas_call(
        _proj_kernel,
        grid=(NP // tn, m // tm),
        in_specs=[pl.BlockSpec((tm, D), lambda n, i: (i, 0)),
                  pl.BlockSpec((D, tn), lambda n, i: (0, n))],
        out_specs=pl.BlockSpec((tm, tn), lambda n, i: (i, n)),
        out_shape=jax.ShapeDtypeStruct((m, NP), BF16),
        compiler_params=_cp(2), name="proj_in",
    )(h2d, w_pack)


def _softmax_pv(s_list, v_list):
    m = s_list[0].max(axis=-1, keepdims=True)
    for s in s_list[1:]:
        m = jnp.maximum(m, s.max(axis=-1, keepdims=True))
    acc = None
    for s, v in zip(s_list, v_list):
        o = _dot(jnp.exp(s - m).astype(BF16), v)
        acc = o if acc is None else acc + o
    return acc[:, :128] / acc[:, 128:]


def _na_kernel(q_ref, k_ref, v_ref, bias_ref, hm_ref, *mlp_refs_o_ref):
    mlp_in, o_ref, oc_ref = mlp_refs_o_ref[:-2], mlp_refs_o_ref[-2], mlp_refs_o_ref[-1]
    j = pl.program_id(1)
    lane = lax.broadcasted_iota(jnp.int32, (BLK, 128), 1)
    low = lane < DH_A

    def run(local_start):
        _gmlp_body(*mlp_in, oc_ref)
        for p in range(HA // 2):
            sl = slice(128 * p, 128 * p + 128)
            q2 = q_ref[:, sl]
            kc = k_ref[T:TT, sl]
            vc = jnp.concatenate([v_ref[T:TT, sl], jnp.ones((L, 128), BF16)], axis=1)
            if local_start is not None:
                kl = k_ref[pl.ds(local_start, NA_KEYS), sl]
                vl = jnp.concatenate([v_ref[pl.ds(local_start, NA_KEYS), sl],
                                      jnp.ones((NA_KEYS, 128), BF16)], axis=1)
            pair = []
            for hh in range(2):
                qm = (q2.astype(F32) * hm_ref[hh:hh + 1, :]).astype(BF16)
                s_ctx = _dot_nt(qm, kc)
                if local_start is not None:
                    s_loc = _dot_nt(qm, kl) + bias_ref[2 * p + hh]
                    pair.append(_softmax_pv([s_loc, s_ctx], [vl, vc]))
                else:
                    pair.append(_softmax_pv([s_ctx], [vc]))
            o_ref[:, sl] = jnp.where(low, pair[0], pair[1]).astype(BF16)

    @pl.when(j < NLAT)
    def _():
        krow = jnp.clip(j * NA_QROWS - WIN_ROWS // 2, 0, ROWS - NA_KROWS)
        run(pl.multiple_of(krow * GRID_W, GRID_W))

    @pl.when(j == NLAT)
    def _():
        run(None)


def _na_gmlp_call(P, bias, hmask, ln_g, ln_b, w_sp, bs_rep, n_blocks):
    bsz = P.shape[0]

    def bias_idx(b, j):
        return (jnp.where(j == 0, 0, jnp.where(j == NLAT - 1, 2, 1)), 0, 0, 0)

    rows = n_blocks * BLK
    return pl.pallas_call(
        _na_kernel,
        grid=(bsz, n_blocks),
        in_specs=[pl.BlockSpec((None, BLK, WA), lambda b, j: (b, j, C_QA // WA)),
                  pl.BlockSpec((None, TT, WA), lambda b, j: (b, 0, C_KA // WA)),
                  pl.BlockSpec((None, TT, WA), lambda b, j: (b, 0, C_VA // WA)),
                  pl.BlockSpec((None, HA, BLK, NA_KEYS), bias_idx),
                  pl.BlockSpec((2, 128), lambda b, j: (0, 0)),
                  pl.BlockSpec((None, BLK, WC), lambda b, j: (b, j, C_UC // WC)),
                  pl.BlockSpec((None, BLK, WC), lambda b, j: (b, j, C_VC // WC)),
                  pl.BlockSpec((1, WC), lambda b, j: (0, 0)),
                  pl.BlockSpec((1, WC), lambda b, j: (0, 0)),
                  pl.BlockSpec((GC, MIX_CHUNK, MIX_CHUNK), lambda b, j: (0, 0, 0)),
                  pl.BlockSpec((GC, MIX_CHUNK, 128), lambda b, j: (0, 0, 0))],
        out_specs=[pl.BlockSpec((None, BLK, WA), lambda b, j: (b, j, 0)),
                   pl.BlockSpec((None, BLK, WC), lambda b, j: (b, j, 0))],
        out_shape=[jax.ShapeDtypeStruct((bsz, rows, WA), BF16),
                   jax.ShapeDtypeStruct((bsz, rows, WC), BF16)],
        compiler_params=_cp(2), name="nbr_attn_gmlp",
    )(P, P, P, bias, hmask, P, P, ln_g.reshape(1, WC), ln_b.reshape(1, WC), w_sp, bs_rep)


def _na_bias(rpb_l):
    n_dr, n_dc = 2 * WIN_ROWS - 1, 2 * WIN_COLS - 1
    cq = np.arange(GRID_W)[:, None]
    ck = np.arange(GRID_W)[None, :]
    cs = np.clip(cq - WIN_COLS // 2, 0, GRID_W - WIN_COLS)
    col_ok = (ck >= cs) & (ck < cs + WIN_COLS)
    dc = np.clip(ck - cq + WIN_COLS - 1, 0, n_dc - 1)
    col_sel = (dc[:, :, None] == np.arange(n_dc)).astype(np.float32)
    colx = jnp.einsum("qke,hde->hdqk", col_sel, rpb_l, precision=lax.Precision.HIGHEST)
    colx = jnp.where(col_ok, colx, NEG).astype(F32)
    return pl.pallas_call(
        _bias_kernel,
        grid=(3, HA),
        in_specs=[pl.BlockSpec((None, n_dr, GRID_W, GRID_W), lambda t, h: (h, 0, 0, 0))],
        out_specs=pl.BlockSpec((None, None, BLK, NA_KEYS), lambda t, h: (t, h, 0, 0)),
        out_shape=jax.ShapeDtypeStruct((3, HA, BLK, NA_KEYS), F32),
        compiler_params=_cp(2), name="na_bias",
    )(colx)


def _bias_kernel(colx_ref, o_ref):
    t = pl.program_id(0)
    neg = jnp.full((GRID_W, GRID_W), NEG, F32)
    for tt, rb in enumerate((0, 1, NLAT - 1)):
        @pl.when(t == tt)
        def _(rb=rb):
            k0 = int(np.clip(rb * NA_QROWS - WIN_ROWS // 2, 0, ROWS - NA_KROWS))
            for qi in range(NA_QROWS):
                r = rb * NA_QROWS + qi
                rs = int(np.clip(r - WIN_ROWS // 2, 0, ROWS - WIN_ROWS))
                for kp in range(NA_KROWS // 2):
                    pair = []
                    for kr in (k0 + 2 * kp, k0 + 2 * kp + 1):
                        inside = rs <= kr < rs + WIN_ROWS
                        pair.append(colx_ref[kr - r + WIN_ROWS - 1] if inside else neg)
                    o_ref[qi * GRID_W:(qi + 1) * GRID_W, kp * 128:(kp + 1) * 128] = (
                        jnp.concatenate(pair, axis=1))


def _gla_matrices():
    n = BLK
    i = np.arange(n)[:, None]
    t = np.arange(n)[None, :]
    out = np.zeros((2, (len(GLA_FINE) + 1) * n, n), np.float32)
    for f, lvl in enumerate(GLA_FINE):
        s = n >> (lvl + 1)
        blk0 = (i // (2 * s)) * (2 * s)
        m = blk0 + s - 1
        fwd = np.where(i > m, (t > m) & (t <= i), (t > i) & (t <= m))
        bwd = np.where(i <= m, (t >= i) & (t <= m), (t > m) & (t < i))
        out[0, f * n:(f + 1) * n] = fwd
        out[1, f * n:(f + 1) * n] = bwd
    out[0, len(GLA_FINE) * n:] = t <= i
    out[1, len(GLA_FINE) * n:] = t >= i
    return out


def _gla_level_ids():
    h = BLK // 2
    i = np.arange(h)[:, None]
    j = np.arange(h)[None, :]
    x = np.maximum(i ^ j, 1)
    lvl = GLA_LEVELS - 1 - np.floor(np.log2(x)).astype(np.int32)
    fwd = np.where(i == j, -1, np.where(i > j, lvl, -2))
    bwd = np.where(i == j, -1, np.where(i < j, lvl, -2))
    return np.stack([fwd, bwd]).astype(np.int32)


def _gla_body(rev, kb_ref, qb_ref, vb_ref, dec_ref, wd_ref, bd_ref, rope_ref, m_ref, lev_ref, o_ref,
              st_ref):
    half = BLK // 2
    z = _dot(dec_ref[...], wd_ref[...].astype(BF16)) + bd_ref[...]
    la = (jnp.minimum(z, 0.0) - jnp.log1p(jnp.exp(-jnp.abs(z)))) * (LOG2E / GATE_TAU)
    la_b = la.astype(BF16)

    def seg_sum(f):
        return _dot(m_ref[f * BLK:(f + 1) * BLK, :], la_b)

    run = seg_sum(len(GLA_FINE))
    e_in = jnp.exp2(run)
    e_out = jnp.exp2((run[0:1] if rev else run[BLK - 1:BLK]) - run)
    e_fine = {lvl: jnp.exp2(seg_sum(f)) for f, lvl in enumerate(GLA_FINE)}

    def slab_decay(i, s, ks):
        base = (i // 2) * 2 * s
        ref = base + s if rev else base + s - 1
        d = run[i * s:(i + 1) * s, ks] - run[ref:ref + 1, ks]
        return jnp.exp2(-d if ((i % 2 == 0) != rev) else d)

    row = lax.broadcasted_iota(jnp.int32, (BLK, DK_B), 0)
    lev = lev_ref[...]
    on_diag = lev == -1
    at_level = {lvl: lev == lvl for lvl in range(1, GLA_LEVELS)}
    cosv, sav, sbv = rope_ref[0], rope_ref[1], rope_ref[2]

    def rope(x):
        return x * cosv + pltpu.roll(x, DK_B - 32, 1) * sav + pltpu.roll(x, 32, 1) * sbv

    qi, ki = (0, 1) if rev else (1, 0)
    q_rows = slice(qi * half, (qi + 1) * half)
    k_rows = slice(ki * half, (ki + 1) * half)

    outs = []
    for hh in range(HB):
        ks = slice(DK_B * hh, DK_B * (hh + 1))
        q = rope(qb_ref[:, ks].astype(F32)) * (DK_B ** -0.5)
        k = rope(kb_ref[:, ks].astype(F32))
        v = vb_ref[:, DV_B * hh:DV_B * (hh + 1)]
        self_w = jnp.sum(q * k, axis=-1, keepdims=True)
        cross = _dot_nt((q[q_rows] * slab_decay(qi, half, ks)).astype(BF16),
                        (k[k_rows] * slab_decay(ki, half, ks)).astype(BF16))
        diag = [jnp.where(on_diag, self_w[c * half:(c + 1) * half], 0.0) for c in range(2)]
        for lvl in range(1, GLA_LEVELS):
            s = BLK >> (lvl + 1)
            if lvl not in GLA_FINE:
                parts = []
                for i in range(BLK // s):
                    src = q if ((i % 2 == 1) != rev) else k
                    parts.append(src[i * s:(i + 1) * s] * slab_decay(i, s, ks))
                x = jnp.concatenate(parts, axis=0)
            else:
                second = ((row // s) & 1) == 1
                x = (jnp.where(second, k, q) if rev else jnp.where(second, q, k)) * e_fine[lvl][:, ks]
            xb = x.astype(BF16)
            for c in range(2):
                xc = xb[c * half:(c + 1) * half]
                diag[c] = jnp.where(at_level[lvl], _dot_nt(xc, xc), diag[c])
        zero = jnp.zeros((half, half), F32)
        if rev:
            att = jnp.concatenate([jnp.concatenate([diag[0], cross], axis=1),
                                   jnp.concatenate([zero, diag[1]], axis=1)], axis=0)
        else:
            att = jnp.concatenate([jnp.concatenate([diag[0], zero], axis=1),
                                   jnp.concatenate([cross, diag[1]], axis=1)], axis=0)
        e_q = e_in[:, ks]
        st = st_ref[hh]
        o = _dot_nt((q * e_q).astype(BF16), st.astype(BF16)) + _dot(att.astype(BF16), v)
        outs.append(o.astype(BF16))
        carry = e_q[0:1, :] if rev else e_q[BLK - 1:BLK, :]
        kt = (k * e_out[:, ks]).astype(BF16)
        st_ref[hh] = st * carry + _dot_tn(v, kt)
    o_ref[...] = jnp.concatenate(outs, axis=-1)


def _gla_kernel(*refs):
    st_ref = refs[-1]
    d = pl.program_id(1)

    @pl.when(pl.program_id(2) == 0)
    def _():
        st_ref[...] = jnp.zeros_like(st_ref)

    @pl.when(d == 0)
    def _():
        _gla_body(False, *refs)

    @pl.when(d == 1)
    def _():
        _gla_body(True, *refs)


def _gla_call(P, wd_pad, bd, rope, mstack, lev_ids):
    bsz = P.shape[0]

    def blk(dd, s):
        return jnp.where(s == 0, NLAT, jnp.where(dd == 0, s - 1, NLAT - s))

    return pl.pallas_call(
        _gla_kernel,
        grid=(bsz, 2, NBLK),
        in_specs=[
            pl.BlockSpec((None, BLK, KB), lambda b, dd, s: (b, blk(dd, s), C_KB // KB)),
            pl.BlockSpec((None, BLK, KB), lambda b, dd, s: (b, blk(dd, s), C_QB // KB)),
            pl.BlockSpec((None, BLK, VB), lambda b, dd, s: (b, blk(dd, s), C_VB // VB)),
            pl.BlockSpec((None, BLK, 128), lambda b, dd, s: (b, blk(dd, s), C_DEC // 128)),
            pl.BlockSpec((None, 128, KB), lambda b, dd, s: (dd, 0, 0)),
            pl.BlockSpec((None, 1, KB), lambda b, dd, s: (dd, 0, 0)),
            pl.BlockSpec((3, BLK, DK_B), lambda b, dd, s: (0, blk(dd, s), 0)),
            pl.BlockSpec((None, (len(GLA_FINE) + 1) * BLK, BLK), lambda b, dd, s: (dd, 0, 0)),
            pl.BlockSpec((None, BLK // 2, BLK // 2), lambda b, dd, s: (dd, 0, 0)),
        ],
        out_specs=pl.BlockSpec((None, None, BLK, VB), lambda b, dd, s: (dd, b, blk(dd, s), 0)),
        out_shape=jax.ShapeDtypeStruct((2, bsz, TT, VB), BF16),
        scratch_shapes=[pltpu.VMEM((HB, DV_B, DK_B), F32)],
        compiler_params=_cp(3), name="gla",
    )(P, P, P, P, wd_pad, bd, rope, mstack, lev_ids)


def _rope_tables():
    t = np.arange(T)
    half = DK_B // 2
    inv = np.float32(ROPE_BASE) ** (-np.arange(0, half, 2, dtype=np.float32) / np.float32(half))

    def tab(pos):
        ang = pos.astype(np.float32)[:, None] * inv[None, :]
        return np.concatenate([ang, ang], axis=-1)

    ang = np.concatenate([tab(t // GRID_W), tab(t % GRID_W)], axis=-1)
    cos, sin = np.cos(ang), np.sin(ang)
    first = (np.arange(DK_B) % half) < (half // 2)
    sa = np.where(first[None, :], -sin, 0.0)
    sb = np.where(first[None, :], 0.0, sin)
    ident = np.ones((L, DK_B), np.float32)
    zero = np.zeros((L, DK_B), np.float32)
    tabs = np.stack([np.concatenate([cos, ident]), np.concatenate([sa, zero]), np.concatenate([sb, zero])])
    return jnp.asarray(tabs, F32)


def _gelu(x):
    return 0.5 * x * (1.0 + jnp.tanh(0.7978845608028654 * (x + 0.044715 * (x * x * x))))


def _gmlp_body(u_ref, v_ref, g_ref, b_ref, w_ref, bs_ref, o_ref):
    for ch in range(BLK // MIX_CHUNK):
        rs = slice(ch * MIX_CHUNK, (ch + 1) * MIX_CHUNK)
        u = _gelu(u_ref[rs, :].astype(F32))
        v = _gelu(v_ref[rs, :].astype(F32))
        for g in range(GC):
            cs = slice(g * CG, (g + 1) * CG)
            vg = v[:, cs]
            mu = jnp.mean(vg, axis=-1, keepdims=True)
            var = jnp.mean(jnp.square(vg - mu), axis=-1, keepdims=True)
            vn = (vg - mu) * lax.rsqrt(var + EPS) * g_ref[:, cs] + b_ref[:, cs]
            s = _dot(w_ref[g].astype(BF16), vn.astype(BF16)) + bs_ref[g]
            o_ref[rs, cs] = (u[:, cs] * s).astype(BF16)


def _out_kernel(*refs, split):
    oa_ref, og_ref, gb_ref, oc_ref, w_ref = refs[:5]
    if split:
        x_in = _stream_block(refs[5], refs[6])
    else:
        x_in = refs[5][...]
    gg_ref, gpost_ref, gpre_ref, mod_ref, wr_ref, xo_ref, h2_ref, lg_ref = refs[6 + split:]
    o = og_ref[0].astype(F32) + og_ref[1].astype(F32)
    gb = gb_ref[...].astype(F32)
    parts = []
    for h in range(HB):
        cs = slice(h * DV_B, (h + 1) * DV_B)
        gh = gb[:, cs]
        parts.append((_rms(o[:, cs]) * gg_ref[:, cs] * (gh * _sigmoid(gh))).astype(BF16))
    mixed = jnp.concatenate([oa_ref[...]] + parts + [oc_ref[...]], axis=-1)
    y = _dot(mixed, w_ref[...])
    x1 = x_in + mod_ref[2:3, :] * (_rms(y) * gpost_ref[...])
    xo_ref[...] = x1
    h2 = ((_rms(x1) * gpre_ref[...]) * (1.0 + mod_ref[4:5, :]) + mod_ref[3:4, :]).astype(BF16)
    h2_ref[...] = h2
    wh, wl = _split_bf16(wr_ref[...])
    lg_ref[...] = _dot(h2, wh) + _dot(h2, wl)


def _out_call(o_a, o_g, P, o_c, w_out_b, stream, g_gla, g_post, g_pre, modtab, wr_pad, n_blocks):
    bsz = P.shape[0]
    rows = n_blocks * BLK
    split = isinstance(stream, tuple)
    rowspec = lambda w: pl.BlockSpec((None, BLK, w), lambda b, j: (b, j, 0))
    vec = lambda w: pl.BlockSpec((1, w), lambda b, j: (0, 0))
    return pl.pallas_call(
        functools.partial(_out_kernel, split=split),
        grid=(bsz, n_blocks),
        in_specs=[rowspec(WA),
                  pl.BlockSpec((2, None, BLK, VB), lambda b, j: (0, b, j, 0)),
                  pl.BlockSpec((None, BLK, VB), lambda b, j: (b, j, C_GB // VB)),
                  rowspec(WC),
                  pl.BlockSpec((D, D), lambda b, j: (0, 0))]
                 + (_stream_specs() if split else [rowspec(D)])
                 + [vec(VB), vec(D), vec(D),
                    pl.BlockSpec((None, None, 8, D), lambda b, j: (b, j // NLAT, 0, 0)),
                    pl.BlockSpec((D, 128), lambda b, j: (0, 0))],
        out_specs=[rowspec(D), rowspec(D), rowspec(128)],
        out_shape=[jax.ShapeDtypeStruct((bsz, rows, D), F32),
                   jax.ShapeDtypeStruct((bsz, rows, D), BF16),
                   jax.ShapeDtypeStruct((bsz, rows, 128), F32)],
        compiler_params=_cp(2), name="proj_out",
    )(o_a, o_g, P, o_c, w_out_b, *(stream if split else (stream,)), g_gla.reshape(1, VB),
      g_post.reshape(1, D), g_pre.reshape(1, D), modtab, wr_pad)


def _route_set(lg, cap, upper):
    n = lg.shape[0]
    lt = lg.T[:N_EXPERTS, :]
    ex = jnp.exp(lt - lt.max(axis=0, keepdims=True))
    aff = ex / ex.sum(axis=0, keepdims=True)
    bits = pltpu.bitcast(aff, jnp.int32)
    capf = jnp.float32(cap)

    def keep(cand, prefix):
        cnt = jnp.sum(jnp.where(bits >= cand, 1.0, 0.0), axis=1, keepdims=True)
        return jnp.where(cnt >= capf, cand, prefix)

    def two_bits(i, prefix):
        lo = jnp.left_shift(jnp.int32(1), 29 - 2 * i)
        best = keep(prefix | lo, prefix)
        best = keep(prefix | (lo + lo), best)
        return keep(prefix | (lo + lo + lo), best)

    thr = lax.fori_loop(0, 15, two_bits, jnp.zeros((N_EXPERTS, 1), jnp.int32))
    thr = keep(thr | 1, thr)
    gt = jnp.where(bits > thr, 1.0, 0.0)
    eq = jnp.where(bits == thr, 1.0, 0.0)
    need = capf - gt.sum(axis=1, keepdims=True)
    rank_eq = _dot(eq.astype(BF16), upper)
    sel = gt + eq * jnp.where(rank_eq < need, 1.0, 0.0)
    before = _dot(sel.astype(BF16), upper)
    slot = jnp.where(sel > 0.5, before, -1.0)
    pad = jnp.full((128 - SCATTER_EXPERTS, n), -1.0, F32)
    slot_t = [jnp.concatenate([slot[g * SCATTER_EXPERTS:(g + 1) * SCATTER_EXPERTS], pad], axis=0).T
              for g in range(N_EXPERTS // SCATTER_EXPERTS)]
    tile_start = jnp.concatenate([before[:, r * BLK:r * BLK + 1] for r in range(n // BLK)]
                                 + [jnp.full((N_EXPERTS, 128 - n // BLK), capf, F32)], axis=1)
    return slot, aff, slot_t, tile_start


def _route_kernel(lg_ref, u_ref, *out_refs, with_ctx):
    sl, af, st, ts = _route_set(lg_ref[0:T, :], CAP_LAT, u_ref[...])
    out_refs[0][...] = sl
    out_refs[1][...] = af
    for g, s in enumerate(st):
        out_refs[2][g, 0:T, :] = s
    out_refs[3][...] = ts
    if with_ctx:
        sl, af, st, _ = _route_set(lg_ref[T:TT, :], CAP_CTX, u_ref[0:L, 0:L])
        out_refs[4][...] = sl
        out_refs[5][...] = af
        for g, s in enumerate(st):
            out_refs[2][g, T:TT, :] = s


def _route_call(lg, upper, with_ctx):
    bsz, rows, _ = lg.shape
    ng = N_EXPERTS // SCATTER_EXPERTS
    en = lambda n: pl.BlockSpec((None, N_EXPERTS, n), lambda b: (b, 0, 0))
    out_specs = [en(T), en(T), pl.BlockSpec((None, ng, rows, 128), lambda b: (b, 0, 0, 0)), en(128)]
    out_shape = [jax.ShapeDtypeStruct((bsz, N_EXPERTS, T), F32),
                 jax.ShapeDtypeStruct((bsz, N_EXPERTS, T), F32),
                 jax.ShapeDtypeStruct((bsz, ng, rows, 128), F32),
                 jax.ShapeDtypeStruct((bsz, N_EXPERTS, 128), F32)]
    if with_ctx:
        out_specs += [en(L), en(L)]
        out_shape += [jax.ShapeDtypeStruct((bsz, N_EXPERTS, L), F32)] * 2
    return pl.pallas_call(
        functools.partial(_route_kernel, with_ctx=with_ctx),
        grid=(bsz,),
        in_specs=[pl.BlockSpec((None, rows, 128), lambda b: (b, 0, 0)),
                  pl.BlockSpec((T, T), lambda b: (0, 0))],
        out_specs=out_specs, out_shape=out_shape,
        compiler_params=_cp(1), name="route",
    )(lg, upper)


def _gather_kernel(*refs, n_sets):
    ins, outs = refs[:3 * n_sets], refs[3 * n_sets:]
    for k in range(n_sets):
        slot_ref, aff_ref, h_ref = ins[3 * k:3 * k + 3]
        xs_ref, g_ref = outs[2 * k:2 * k + 2]
        cap, n = xs_ref.shape[1], h_ref.shape[0]
        sid = lax.broadcasted_iota(jnp.int32, (cap, n), 0).astype(F32)
        picks = []
        for j in range(GATHER_EXPERTS):
            e = pl.program_id(1) * GATHER_EXPERTS + j
            hit = sid == slot_ref[pl.ds(e, 1), :]
            picks.append(jnp.where(hit, 1.0, 0.0).astype(BF16))
            g = jnp.sum(jnp.where(hit, aff_ref[pl.ds(e, 1), :], 0.0), axis=1, keepdims=True)
            g_ref[j] = jnp.broadcast_to(g, (cap, 128))
        xs = _dot(jnp.concatenate(picks, axis=0), h_ref[...]).astype(BF16)
        xs_ref[...] = xs.reshape(GATHER_EXPERTS, cap, D)


def _gather_call(h2, sets):
    bsz = h2.shape[0]
    in_specs, args, out_specs, out_shape = [], [], [], []
    for slot, aff, n, cap, blk_idx in sets:
        in_specs += [pl.BlockSpec((None, N_EXPERTS, n), lambda b, e: (b, 0, 0)),
                     pl.BlockSpec((None, N_EXPERTS, n), lambda b, e: (b, 0, 0)),
                     pl.BlockSpec((None, n, D), lambda b, e, i=blk_idx: (b, i, 0))]
        args += [slot, aff, h2]
        out_specs += [pl.BlockSpec((GATHER_EXPERTS, cap, D), lambda b, e: (e, b, 0)),
                      pl.BlockSpec((GATHER_EXPERTS, cap, 128), lambda b, e: (e, b, 0))]
        out_shape += [jax.ShapeDtypeStruct((N_EXPERTS, bsz * cap, D), BF16),
                      jax.ShapeDtypeStruct((N_EXPERTS, bsz * cap, 128), F32)]
    return pl.pallas_call(
        functools.partial(_gather_kernel, n_sets=len(sets)),
        grid=(bsz, N_EXPERTS // GATHER_EXPERTS),
        in_specs=in_specs, out_specs=out_specs, out_shape=out_shape,
        compiler_params=_cp(2), name="gather",
    )(*args)


def _ffn_kernel(*refs, rl, rc, nf):
    if rc:
        xl_ref, xc_ref, gl_ref, gc_ref, wg_ref, wu_ref, wd_ref, yl_ref, yc_ref, hid = refs
        groups = ((xl_ref, gl_ref, yl_ref, 0, rl), (xc_ref, gc_ref, yc_ref, rl, rc))
    else:
        xl_ref, gl_ref, wg_ref, wu_ref, wd_ref, yl_ref, hid = refs
        groups = ((xl_ref, gl_ref, yl_ref, 0, rl),)
    s = pl.program_id(1)

    @pl.when(s < nf)
    def _():
        wg = wg_ref[...].astype(BF16)
        wu = wu_ref[...].astype(BF16)
        for x_ref, _, _, r0, nr in groups:
            xv = x_ref[...]
            a = _dot(xv, wg)
            u = _dot(xv, wu)
            hid[s, r0:r0 + nr, :] = ((a * _sigmoid(a)) * u).astype(BF16)

    @pl.when(s >= nf)
    def _():
        wd = wd_ref[...].astype(BF16)
        for _, g_ref, y_ref, r0, nr in groups:
            hm = jnp.concatenate([hid[k, r0:r0 + nr, :] for k in range(nf)], axis=1)
            y = _dot(hm, wd)
            for c in range(FF_TILE // 128):
                cs = slice(128 * c, 128 * (c + 1))
                y_ref[:, cs] = (y[:, cs] * g_ref[...]).astype(BF16)


def _ffn_call(layer, xs_l, g_l, xs_c, g_c, w_gate, w_up, w_down):
    rl = xs_l.shape[1]
    rc = 0 if xs_c is None else xs_c.shape[1]
    nf = EXPERT_FF // FF_TILE
    nd = D // FF_TILE
    up = lambda e, s: (layer, e, 0, jnp.minimum(s, nf - 1))
    down = lambda s: jnp.maximum(s - nf, 0)
    in_specs = [pl.BlockSpec((None, rl, D), lambda e, s: (e, 0, 0))]
    args = [xs_l]
    if rc:
        in_specs.append(pl.BlockSpec((None, rc, D), lambda e, s: (e, 0, 0)))
        args.append(xs_c)
    in_specs.append(pl.BlockSpec((None, rl, 128), lambda e, s: (e, 0, 0)))
    args.append(g_l)
    if rc:
        in_specs.append(pl.BlockSpec((None, rc, 128), lambda e, s: (e, 0, 0)))
        args.append(g_c)
    in_specs += [pl.BlockSpec((None, None, D, FF_TILE), up),
                 pl.BlockSpec((None, None, D, FF_TILE), up),
                 pl.BlockSpec((None, None, EXPERT_FF, FF_TILE), lambda e, s: (layer, e, 0, down(s)))]
    args += [w_gate, w_up, w_down]
    out_specs = [pl.BlockSpec((None, rl, FF_TILE), lambda e, s: (e, 0, down(s)))]
    out_shape = [jax.ShapeDtypeStruct((N_EXPERTS, rl, D), BF16)]
    if rc:
        out_specs.append(pl.BlockSpec((None, rc, FF_TILE), lambda e, s: (e, 0, down(s))))
        out_shape.append(jax.ShapeDtypeStruct((N_EXPERTS, rc, D), BF16))
    return pl.pallas_call(
        functools.partial(_ffn_kernel, rl=rl, rc=rc, nf=nf),
        grid=(N_EXPERTS, nf + nd),
        in_specs=in_specs, out_specs=out_specs, out_shape=out_shape,
        scratch_shapes=[pltpu.VMEM((nf, rl + rc, FF_TILE), BF16)],
        compiler_params=_cp(2), name="expert_ffn",
    )(*args)


def _scatter_kernel(*refs, ne, with_ctx, has_next):
    refs = list(refs)
    start_ref, slot_ref, yl_ref = refs[:3]
    refs = refs[3:]
    yc_ref = refs.pop(0) if with_ctx else None
    x_ref, gpost_ref, mod_ref = refs[:3]
    refs = refs[3:]
    if has_next:
        gpre_ref, modn_ref = refs[:2]
        refs = refs[2:]
    xo_ref = refs[0]
    hn_ref = refs[1] if has_next else None
    acc = refs[-1]
    eg = pl.program_id(1)
    r = pl.program_id(2)
    ng = pl.num_programs(1)

    def add(cap, y_ref):
        kk = ne * cap
        if cap % 128 == 0:
            want = lax.broadcasted_iota(jnp.int32, (BLK, cap), 1).astype(F32)
            pt = jnp.concatenate([jnp.where(slot_ref[:, j:j + 1] == want, 1.0, 0.0).astype(BF16)
                                  for j in range(ne)], axis=1)
        else:
            er = lax.broadcasted_iota(jnp.int32, (128, kk), 0)
            ec = lax.broadcasted_iota(jnp.int32, (128, kk), 1)
            expand = jnp.where(er == ec // cap, 1.0, 0.0).astype(BF16)
            spread = _dot(slot_ref[...].astype(BF16), expand)
            want = (lax.broadcasted_iota(jnp.int32, (BLK, kk), 1) % cap).astype(F32)
            pt = jnp.where(spread == want, 1.0, 0.0).astype(BF16)
        acc[r] += _dot(pt, y_ref[...].reshape(kk, D))

    def add_windows(first):
        lane = lax.broadcasted_iota(jnp.int32, (BLK, 128), 1).astype(F32)
        low = lane < SCATTER_WIN
        tiles = []
        for j in range(0, ne, 2):
            rel_a = slot_ref[:, j:j + 1] - first[j].astype(F32)
            rel_b = slot_ref[:, j + 1:j + 2] - first[j + 1].astype(F32) + SCATTER_WIN
            tiles.append(jnp.where(jnp.where(low, rel_a, rel_b) == lane, 1.0, 0.0).astype(BF16))
        yw = jnp.concatenate([yl_ref[j, pl.ds(pl.multiple_of(first[j], 16), SCATTER_WIN), :]
                              for j in range(ne)], axis=0)
        acc[r] += _dot(jnp.concatenate(tiles, axis=1), yw)

    @pl.when(eg == 0)
    def _():
        acc[r] = jnp.zeros((BLK, D), F32)

    @pl.when(r < NLAT)
    def _():
        b = pl.program_id(0)
        first, fits = [], None
        for j in range(ne):
            lo = start_ref[b, eg * ne + j, r]
            hi = start_ref[b, eg * ne + j, r + 1]
            f = jnp.minimum((lo // 16) * 16, CAP_LAT - SCATTER_WIN)
            first.append(f)
            ok = hi <= f + SCATTER_WIN
            fits = ok if fits is None else jnp.logical_and(fits, ok)

        @pl.when(fits)
        def _():
            add_windows(first)

        @pl.when(jnp.logical_not(fits))
        def _():
            add(CAP_LAT, yl_ref)

    if with_ctx:
        @pl.when(r == NLAT)
        def _():
            add(CAP_CTX, yc_ref)

    @pl.when(eg == ng - 1)
    def _():
        x2 = x_ref[...] + mod_ref[5:6, :] * (_rms(acc[r]) * gpost_ref[...])
        xo_ref[...] = x2
        if has_next:
            hn = (_rms(x2) * gpre_ref[...]) * (1.0 + modn_ref[1:2, :]) + modn_ref[0:1, :]
            hn_ref[...] = hn.astype(BF16)


def _scatter_call(tile_start, slot_t, y_l, y_c, x_mid, g_post, modtab, g_pre_next, modtab_next):
    bsz = x_mid.shape[0]
    with_ctx = y_c is not None
    has_next = g_pre_next is not None
    n_tiles = NBLK if with_ctx else NLAT
    ne = SCATTER_EXPERTS
    ng = N_EXPERTS // ne
    vec = pl.BlockSpec((1, D), lambda b, g, r: (0, 0))
    modspec = pl.BlockSpec((None, None, 8, D), lambda b, g, r: (b, r // NLAT, 0, 0))
    late = lambda b, g, r: (b, jnp.where(g == ng - 1, r, 0), 0)
    in_specs = [pl.BlockSpec(memory_space=pltpu.SMEM),
                pl.BlockSpec((None, None, BLK, 128), lambda b, g, r: (b, g, r, 0)),
                pl.BlockSpec((ne, CAP_LAT, D), lambda b, g, r: (g, b, 0))]
    args = [tile_start[:, :, :NLAT + 1].astype(jnp.int32), slot_t, y_l]
    if with_ctx:
        in_specs.append(pl.BlockSpec((ne, CAP_CTX, D), lambda b, g, r: (g, b, 0)))
        args.append(y_c)
    in_specs += [pl.BlockSpec((None, BLK, D), late), vec, modspec]
    args += [x_mid, g_post.reshape(1, D), modtab]
    if has_next:
        in_specs += [vec, modspec]
        args += [g_pre_next.reshape(1, D), modtab_next]
    out_specs = [pl.BlockSpec((None, BLK, D), late)]
    out_shape = [jax.ShapeDtypeStruct((bsz, n_tiles * BLK, D), F32)]
    if has_next:
        out_specs.append(pl.BlockSpec((None, BLK, D), late))
        out_shape.append(jax.ShapeDtypeStruct((bsz, n_tiles * BLK, D), BF16))
    return pl.pallas_call(
        functools.partial(_scatter_kernel, ne=ne, with_ctx=with_ctx, has_next=has_next),
        grid=(bsz, ng, n_tiles),
        in_specs=in_specs, out_specs=out_specs, out_shape=out_shape,
        scratch_shapes=[pltpu.VMEM((n_tiles, BLK, D), F32)],
        compiler_params=_cp(3), name="scatter",
    )(*args)


def _pack_kernel(w_ref, o_ref):
    src = np.cumsum((0, WA, WA, WA, KB, VB, 2 * DEC_RANK, KB, VB, WC, WC))
    order = ((4, C_VB), (7, C_GB), (0, C_QA), (1, C_KA), (2, C_VA), (3, C_KB), (6, C_QB), (8, C_UC),
             (9, C_VC), (5, C_DEC))
    for seg, dst in order:
        lo, hi = int(src[seg]), int(src[seg + 1])
        o_ref[:, dst:dst + hi - lo] = w_ref[:, lo:hi].astype(BF16)
    tail = C_DEC + 2 * DEC_RANK
    o_ref[:, tail:] = jnp.zeros((o_ref.shape[0], NP - tail), BF16)


def _pack_w_in(w_in, layer):
    rows = 256
    n_in = w_in.shape[-1]
    return pl.pallas_call(
        _pack_kernel,
        grid=(D // rows,),
        in_specs=[pl.BlockSpec((None, rows, n_in), lambda i: (layer, i, 0))],
        out_specs=pl.BlockSpec((rows, NP), lambda i: (i, 0)),
        out_shape=jax.ShapeDtypeStruct((D, NP), BF16),
        compiler_params=_cp(1), name="pack_w_in",
    )(w_in)


def kernel(x, c, ctx, c_ctx, w_ada, b_ada, g_pre_mix, g_post_mix, g_pre_ffn, g_post_ffn, w_in, w_dec,
           b_dec, rpb, g_gla, ln_v_g, ln_v_b, w_sp, b_sp, w_out, w_router, w_gate, w_up, w_down):
    bsz = x.shape[0]
    c_all = jnp.zeros((16, D), F32).at[:bsz].set(c).at[8].set(c_ctx)
    mods = _ada_call(c_all, w_ada, b_ada).reshape(DEPTH, 16, 6, D)
    lat = mods[:, :bsz]
    cx = jnp.broadcast_to(mods[:, 8:9], lat.shape)
    modtab = jnp.pad(jnp.stack([lat, cx], axis=2), ((0, 0), (0, 0), (0, 0), (0, 2), (0, 0)))

    rope = _rope_tables()
    mstack = jnp.asarray(_gla_matrices(), BF16)
    lev_ids = jnp.asarray(_gla_level_ids())
    hmask = jnp.asarray(np.stack([(np.arange(128) < DH_A), (np.arange(128) >= DH_A)]) * DH_A ** -0.5, F32)
    upper = jnp.asarray(np.triu(np.ones((T, T), np.float32), 1), BF16)

    stream = (x, ctx)
    h = _prenorm_call(x, ctx, g_pre_mix[0], modtab[0])
    for l in range(DEPTH):
        last = l == DEPTH - 1
        nb = NLAT if last else NBLK
        P = _proj_call(h.reshape(bsz * TT, D), _pack_w_in(w_in, l)).reshape(bsz, TT, NP)
        bs_rep = jnp.broadcast_to(b_sp[l][:, :, None], (GC, MIX_CHUNK, 128))
        o_a, o_c = _na_gmlp_call(P, _na_bias(rpb[l]), hmask, ln_v_g[l], ln_v_b[l], w_sp[l], bs_rep, nb)
        wd_pad = jnp.zeros((2, 128, KB), F32)
        wd_pad = wd_pad.at[0, :DEC_RANK].set(w_dec[l, 0]).at[1, DEC_RANK:2 * DEC_RANK].set(w_dec[l, 1])
        o_g = _gla_call(P, wd_pad, b_dec[l].reshape(2, 1, KB), rope, mstack, lev_ids)
        wr_pad = jnp.pad(w_router[l], ((0, 0), (0, 128 - N_EXPERTS)))
        x_mid, h2, lg = _out_call(o_a, o_g, P, o_c, w_out[l].astype(BF16), stream, g_gla[l], g_post_mix[l],
                                  g_pre_ffn[l], modtab[l], wr_pad, nb)
        routed = _route_call(lg, upper, not last)
        slot_l, aff_l, slot_t, tile_start = routed[:4]
        if last:
            xs_l, gt_l = _gather_call(h2, [(slot_l, aff_l, T, CAP_LAT, 0)])
            (y_l,) = _ffn_call(l, xs_l, gt_l, None, None, w_gate, w_up, w_down)
            (stream,) = _scatter_call(tile_start, slot_t, y_l, None, x_mid, g_post_ffn[l], modtab[l], None,
                                      None)
        else:
            slot_c, aff_c = routed[4:]
            xs_l, gt_l, xs_c, gt_c = _gather_call(h2, [(slot_l, aff_l, T, CAP_LAT, 0),
                                                       (slot_c, aff_c, L, CAP_CTX, NLAT)])
            y_l, y_c = _ffn_call(l, xs_l, gt_l, xs_c, gt_c, w_gate, w_up, w_down)
            stream, h = _scatter_call(tile_start, slot_t, y_l, y_c, x_mid, g_post_ffn[l], modtab[l],
                                      g_pre_mix[l + 1], modtab[l + 1])
    return stream
```

```python
import functools

import numpy as np
import jax
import jax.numpy as jnp
from jax import lax
from jax.experimental import pallas as pl
from jax.experimental.pallas import tpu as pltpu

F32 = jnp.float32
BF16 = jnp.bfloat16

D = 2048
T = 2048
L = 256
TT = T + L
BLK = 256
NBLK = TT // BLK
NLAT = T // BLK
DEPTH = 2
GRID_W = 64
ROWS = T // GRID_W
HA, DH_A = 8, 64
WIN_ROWS, WIN_COLS = 8, 16
HB, DK_B, DV_B = 4, 128, 256
DEC_RANK = 16
GATE_TAU = 16.0
GC, CG, MIX_CHUNK = 4, 128, 128
N_EXPERTS = 16
EXPERT_FF = 2048
CAP_FACTOR = 2
ROPE_BASE = 10000.0
EPS = 1e-6
LOG2E = 1.4426950408889634
WA, KB, VB, WC = HA * DH_A, HB * DK_B, HB * DV_B, GC * CG

NP = 3 * D
C_VB, C_GB, C_QA, C_KA, C_VA, C_KB, C_QB, C_UC, C_VC, C_DEC = (
    0, 1024, 2048, 2560, 3072, 3584, 4096, 4608, 5120, 5632)

NA_QROWS = 4
NA_KROWS = 12
NA_KEYS = NA_KROWS * GRID_W
NEG = -1e30

GLA_LEVELS = 8
GLA_FINE = (5, 6, 7)
FF_TILE = 256
SCATTER_EXPERTS = 8
SCATTER_WIN = 64
GATHER_EXPERTS = 2
CAP_LAT = CAP_FACTOR * T // N_EXPERTS
CAP_CTX = CAP_FACTOR * L // N_EXPERTS
VMEM_LIMIT = 56 * 1024 * 1024


def _cp(n_axes):
    return pltpu.CompilerParams(dimension_semantics=("arbitrary",) * n_axes,
                                vmem_limit_bytes=VMEM_LIMIT)


def _dot(a, b):
    return jnp.dot(a, b, preferred_element_type=F32)


def _dot_nt(a, b):
    return lax.dot_general(a, b, (((1,), (1,)), ((), ())), preferred_element_type=F32)


def _dot_tn(a, b):
    return lax.dot_general(a, b, (((0,), (0,)), ((), ())), preferred_element_type=F32)


def _rms(x):
    return x * lax.rsqrt(jnp.mean(x * x, axis=-1, keepdims=True) + EPS)


def _sigmoid(x):
    return 1.0 / (1.0 + jnp.exp(-x))


def _split_bf16(x):
    hi = x.astype(BF16)
    lo = (x - hi.astype(F32)).astype(BF16)
    return hi, lo


def _ada_kernel(c_ref, w_ref, b_ref, o_ref):
    cv = c_ref[...]
    hi, lo = _split_bf16(cv * _sigmoid(cv))
    r = _dot(jnp.concatenate([hi, lo], axis=0), w_ref[...].astype(BF16))
    o_ref[...] = r[:16] + r[16:] + b_ref[...]


def _ada_call(c_all, w_ada, b_ada):
    tn = 1024
    n6 = w_ada.shape[-1]
    return pl.pallas_call(
        _ada_kernel,
        grid=(DEPTH, n6 // tn),
        in_specs=[pl.BlockSpec((16, D), lambda l, n: (0, 0)),
                  pl.BlockSpec((None, D, tn), lambda l, n: (l, 0, n)),
                  pl.BlockSpec((None, 1, tn), lambda l, n: (l, 0, n))],
        out_specs=pl.BlockSpec((None, 16, tn), lambda l, n: (l, 0, n)),
        out_shape=jax.ShapeDtypeStruct((DEPTH, 16, n6), F32),
        compiler_params=_cp(2), name="ada",
    )(c_all, w_ada, b_ada.reshape(DEPTH, 1, n6))


def _stream_block(x_ref, c_ref):
    return jnp.where(pl.program_id(1) < NLAT, x_ref[...], c_ref[...])


def _stream_specs():
    return [pl.BlockSpec((None, BLK, D), lambda b, j: (b, jnp.minimum(j, NLAT - 1), 0)),
            pl.BlockSpec((None, BLK, D), lambda b, j: (b, 0, 0))]


def _prenorm_kernel(x_ref, c_ref, g_ref, mod_ref, h_ref):
    y = _rms(_stream_block(x_ref, c_ref)) * g_ref[...]
    h_ref[...] = (y * (1.0 + mod_ref[1:2, :]) + mod_ref[0:1, :]).astype(BF16)


def _prenorm_call(x, ctx, g, modtab):
    bsz = x.shape[0]
    return pl.pallas_call(
        _prenorm_kernel,
        grid=(bsz, NBLK),
        in_specs=_stream_specs() + [
            pl.BlockSpec((1, D), lambda b, j: (0, 0)),
            pl.BlockSpec((None, None, 8, D), lambda b, j: (b, j // NLAT, 0, 0))],
        out_specs=pl.BlockSpec((None, BLK, D), lambda b, j: (b, j, 0)),
        out_shape=jax.ShapeDtypeStruct((bsz, TT, D), BF16),
        compiler_params=_cp(2), name="prenorm",
    )(x, ctx, g.reshape(1, D), modtab)


def _proj_kernel(h_ref, w_ref, o_ref):
    o_ref[...] = _dot(h_ref[...], w_ref[...]).astype(BF16)


def _proj_call(h2d, w_pack):
    m = h2d.shape[0]
    tm = 1024 if m % 1024 == 0 else 768
    tn = NP // 3
    return pl.pallas_call(
        _proj_kernel,
        grid=(NP // tn, m // tm),
        in_specs=[pl.BlockSpec((tm, D), lambda n, i: (i, 0)),
                  pl.BlockSpec((D, tn), lambda n, i: (0, n))],
        out_specs=pl.BlockSpec((tm, tn), lambda n, i: (i, n)),
        out_shape=jax.ShapeDtypeStruct((m, NP), BF16),
        compiler_params=_cp(2), name="proj_in",
    )(h2d, w_pack)


def _softmax_pv(s_list, v_list):
    m = s_list[0].max(axis=-1, keepdims=True)
    for s in s_list[1:]:
        m = jnp.maximum(m, s.max(axis=-1, keepdims=True))
    acc = None
    for s, v in zip(s_list, v_list):
        o = _dot(jnp.exp(s - m).astype(BF16), v)
        acc = o if acc is None else acc + o
    return acc[:, :128] / acc[:, 128:]


def _na_kernel(q_ref, k_ref, v_ref, bias_ref, hm_ref, *mlp_refs_o_ref):
    mlp_in, o_ref, oc_ref = mlp_refs_o_ref[:-2], mlp_refs_o_ref[-2], mlp_refs_o_ref[-1]
    j = pl.program_id(1)
    lane = lax.broadcasted_iota(jnp.int32, (BLK, 128), 1)
    low = lane < DH_A

    def run(local_start):
        _gmlp_body(*mlp_in, oc_ref)
        for p in range(HA // 2):
            sl = slice(128 * p, 128 * p + 128)
            q2 = q_ref[:, sl]
            kc = k_ref[T:TT, sl]
            vc = jnp.concatenate([v_ref[T:TT, sl], jnp.ones((L, 128), BF16)], axis=1)
            if local_start is not None:
                kl = k_ref[pl.ds(local_start, NA_KEYS), sl]
                vl = jnp.concatenate([v_ref[pl.ds(local_start, NA_KEYS), sl],
                                      jnp.ones((NA_KEYS, 128), BF16)], axis=1)
            pair = []
            for hh in range(2):
                qm = (q2.astype(F32) * hm_ref[hh:hh + 1, :]).astype(BF16)
                s_ctx = _dot_nt(qm, kc)
                if local_start is not None:
                    s_loc = _dot_nt(qm, kl) + bias_ref[2 * p + hh]
                    pair.append(_softmax_pv([s_loc, s_ctx], [vl, vc]))
                else:
                    pair.append(_softmax_pv([s_ctx], [vc]))
            o_ref[:, sl] = jnp.where(low, pair[0], pair[1]).astype(BF16)

    @pl.when(j < NLAT)
    def _():
        krow = jnp.clip(j * NA_QROWS - WIN_ROWS // 2, 0, ROWS - NA_KROWS)
        run(pl.multiple_of(krow * GRID_W, GRID_W))

    @pl.when(j == NLAT)
    def _():
        run(None)


def _na_gmlp_call(P, bias, hmask, ln_g, ln_b, w_sp, bs_rep, n_blocks):
    bsz = P.shape[0]

    def bias_idx(b, j):
        return (jnp.where(j == 0, 0, jnp.where(j == NLAT - 1, 2, 1)), 0, 0, 0)

    rows = n_blocks * BLK
    return pl.pallas_call(
        _na_kernel,
        grid=(bsz, n_blocks),
        in_specs=[pl.BlockSpec((None, BLK, WA), lambda b, j: (b, j, C_QA // WA)),
                  pl.BlockSpec((None, TT, WA), lambda b, j: (b, 0, C_KA // WA)),
                  pl.BlockSpec((None, TT, WA), lambda b, j: (b, 0, C_VA // WA)),
                  pl.BlockSpec((None, HA, BLK, NA_KEYS), bias_idx),
                  pl.BlockSpec((2, 128), lambda b, j: (0, 0)),
                  pl.BlockSpec((None, BLK, WC), lambda b, j: (b, j, C_UC // WC)),
                  pl.BlockSpec((None, BLK, WC), lambda b, j: (b, j, C_VC // WC)),
                  pl.BlockSpec((1, WC), lambda b, j: (0, 0)),
                  pl.BlockSpec((1, WC), lambda b, j: (0, 0)),
                  pl.BlockSpec((GC, MIX_CHUNK, MIX_CHUNK), lambda b, j: (0, 0, 0)),
                  pl.BlockSpec((GC, MIX_CHUNK, 128), lambda b, j: (0, 0, 0))],
        out_specs=[pl.BlockSpec((None, BLK, WA), lambda b, j: (b, j, 0)),
                   pl.BlockSpec((None, BLK, WC), lambda b, j: (b, j, 0))],
        out_shape=[jax.ShapeDtypeStruct((bsz, rows, WA), BF16),
                   jax.ShapeDtypeStruct((bsz, rows, WC), BF16)],
        compiler_params=_cp(2), name="nbr_attn_gmlp",
    )(P, P, P, bias, hmask, P, P, ln_g.reshape(1, WC), ln_b.reshape(1, WC), w_sp, bs_rep)


def _na_bias(rpb_l):
    n_dr, n_dc = 2 * WIN_ROWS - 1, 2 * WIN_COLS - 1
    cq = np.arange(GRID_W)[:, None]
    ck = np.arange(GRID_W)[None, :]
    cs = np.clip(cq - WIN_COLS // 2, 0, GRID_W - WIN_COLS)
    col_ok = (ck >= cs) & (ck < cs + WIN_COLS)
    dc = np.clip(ck - cq + WIN_COLS - 1, 0, n_dc - 1)
    col_sel = (dc[:, :, None] == np.arange(n_dc)).astype(np.float32)
    colx = jnp.einsum("qke,hde->hdqk", col_sel, rpb_l, precision=lax.Precision.HIGHEST)
    colx = jnp.where(col_ok, colx, NEG).astype(F32)
    return pl.pallas_call(
        _bias_kernel,
        grid=(3, HA),
        in_specs=[pl.BlockSpec((None, n_dr, GRID_W, GRID_W), lambda t, h: (h, 0, 0, 0))],
        out_specs=pl.BlockSpec((None, None, BLK, NA_KEYS), lambda t, h: (t, h, 0, 0)),
        out_shape=jax.ShapeDtypeStruct((3, HA, BLK, NA_KEYS), F32),
        compiler_params=_cp(2), name="na_bias",
    )(colx)


def _bias_kernel(colx_ref, o_ref):
    t = pl.program_id(0)
    neg = jnp.full((GRID_W, GRID_W), NEG, F32)
    for tt, rb in enumerate((0, 1, NLAT - 1)):
        @pl.when(t == tt)
        def _(rb=rb):
            k0 = int(np.clip(rb * NA_QROWS - WIN_ROWS // 2, 0, ROWS - NA_KROWS))
            for qi in range(NA_QROWS):
                r = rb * NA_QROWS + qi
                rs = int(np.clip(r - WIN_ROWS // 2, 0, ROWS - WIN_ROWS))
                for kp in range(NA_KROWS // 2):
                    pair = []
                    for kr in (k0 + 2 * kp, k0 + 2 * kp + 1):
                        inside = rs <= kr < rs + WIN_ROWS
                        pair.append(colx_ref[kr - r + WIN_ROWS - 1] if inside else neg)
                    o_ref[qi * GRID_W:(qi + 1) * GRID_W, kp * 128:(kp + 1) * 128] = (
                        jnp.concatenate(pair, axis=1))


def _gla_matrices():
    n = BLK
    i = np.arange(n)[:, None]
    t = np.arange(n)[None, :]
    out = np.zeros((2, (len(GLA_FINE) + 1) * n, n), np.float32)
    for f, lvl in enumerate(GLA_FINE):
        s = n >> (lvl + 1)
        blk0 = (i // (2 * s)) * (2 * s)
        m = blk0 + s - 1
        fwd = np.where(i > m, (t > m) & (t <= i), (t > i) & (t <= m))
        bwd = np.where(i <= m, (t >= i) & (t <= m), (t > m) & (t < i))
        out[0, f * n:(f + 1) * n] = fwd
        out[1, f * n:(f + 1) * n] = bwd
    out[0, len(GLA_FINE) * n:] = t <= i
    out[1, len(GLA_FINE) * n:] = t >= i
    return out


def _gla_level_ids():
    h = BLK // 2
    i = np.arange(h)[:, None]
    j = np.arange(h)[None, :]
    x = np.maximum(i ^ j, 1)
    lvl = GLA_LEVELS - 1 - np.floor(np.log2(x)).astype(np.int32)
    fwd = np.where(i == j, -1, np.where(i > j, lvl, -2))
    bwd = np.where(i == j, -1, np.where(i < j, lvl, -2))
    return np.stack([fwd, bwd]).astype(np.int32)


def _gla_body(rev, kb_ref, qb_ref, vb_ref, dec_ref, wd_ref, bd_ref, rope_ref, m_ref, lev_ref, o_ref,
              st_ref):
    half = BLK // 2
    z = _dot(dec_ref[...], wd_ref[...].astype(BF16)) + bd_ref[...]
    la = (jnp.minimum(z, 0.0) - jnp.log1p(jnp.exp(-jnp.abs(z)))) * (LOG2E / GATE_TAU)
    la_b = la.astype(BF16)

    def seg_sum(f):
        return _dot(m_ref[f * BLK:(f + 1) * BLK, :], la_b)

    run = seg_sum(len(GLA_FINE))
    e_in = jnp.exp2(run)
    e_out = jnp.exp2((run[0:1] if rev else run[BLK - 1:BLK]) - run)
    e_fine = {lvl: jnp.exp2(seg_sum(f)) for f, lvl in enumerate(GLA_FINE)}

    def slab_decay(i, s, ks):
        base = (i // 2) * 2 * s
        ref = base + s if rev else base + s - 1
        d = run[i * s:(i + 1) * s, ks] - run[ref:ref + 1, ks]
        return jnp.exp2(-d if ((i % 2 == 0) != rev) else d)

    row = lax.broadcasted_iota(jnp.int32, (BLK, DK_B), 0)
    lev = lev_ref[...]
    on_diag = lev == -1
    at_level = {lvl: lev == lvl for lvl in range(1, GLA_LEVELS)}
    cosv, sav, sbv = rope_ref[0], rope_ref[1], rope_ref[2]

    def rope(x):
        return x * cosv + pltpu.roll(x, DK_B - 32, 1) * sav + pltpu.roll(x, 32, 1) * sbv

    qi, ki = (0, 1) if rev else (1, 0)
    q_rows = slice(qi * half, (qi + 1) * half)
    k_rows = slice(ki * half, (ki + 1) * half)

    outs = []
    for hh in range(HB):
        ks = slice(DK_B * hh, DK_B * (hh + 1))
        q = rope(qb_ref[:, ks].astype(F32)) * (DK_B ** -0.5)
        k = rope(kb_ref[:, ks].astype(F32))
        v = vb_ref[:, DV_B * hh:DV_B * (hh + 1)]
        self_w = jnp.sum(q * k, axis=-1, keepdims=True)
        cross = _dot_nt((q[q_rows] * slab_decay(qi, half, ks)).astype(BF16),
                        (k[k_rows] * slab_decay(ki, half, ks)).astype(BF16))
        diag = [jnp.where(on_diag, self_w[c * half:(c + 1) * half], 0.0) for c in range(2)]
        for lvl in range(1, GLA_LEVELS):
            s = BLK >> (lvl + 1)
            if lvl not in GLA_FINE:
                parts = []
                for i in range(BLK // s):
                    src = q if ((i % 2 == 1) != rev) else k
                    parts.append(src[i * s:(i + 1) * s] * slab_decay(i, s, ks))
                x = jnp.concatenate(parts, axis=0)
            else:
                second = ((row // s) & 1) == 1
                x = (jnp.where(second, k, q) if rev else jnp.where(second, q, k)) * e_fine[lvl][:, ks]
            xb = x.astype(BF16)
            for c in range(2):
                xc = xb[c * half:(c + 1) * half]
                diag[c] = jnp.where(at_level[lvl], _dot_nt(xc, xc), diag[c])
        zero = jnp.zeros((half, half), F32)
        if rev:
            att = jnp.concatenate([jnp.concatenate([diag[0], cross], axis=1),
                                   jnp.concatenate([zero, diag[1]], axis=1)], axis=0)
        else:
            att = jnp.concatenate([jnp.concatenate([diag[0], zero], axis=1),
                                   jnp.concatenate([cross, diag[1]], axis=1)], axis=0)
        e_q = e_in[:, ks]
        st = st_ref[hh]
        o = _dot_nt((q * e_q).astype(BF16), st.astype(BF16)) + _dot(att.astype(BF16), v)
        outs.append(o.astype(BF16))
        carry = e_q[0:1, :] if rev else e_q[BLK - 1:BLK, :]
        kt = (k * e_out[:, ks]).astype(BF16)
        st_ref[hh] = st * carry + _dot_tn(v, kt)
    o_ref[...] = jnp.concatenate(outs, axis=-1)


def _gla_kernel(*refs):
    st_ref = refs[-1]
    d = pl.program_id(1)

    @pl.when(pl.program_id(2) == 0)
    def _():
        st_ref[...] = jnp.zeros_like(st_ref)

    @pl.when(d == 0)
    def _():
        _gla_body(False, *refs)

    @pl.when(d == 1)
    def _():
        _gla_body(True, *refs)


def _gla_call(P, wd_pad, bd, rope, mstack, lev_ids):
    bsz = P.shape[0]

    def blk(dd, s):
        return jnp.where(s == 0, NLAT, jnp.where(dd == 0, s - 1, NLAT - s))

    return pl.pallas_call(
        _gla_kernel,
        grid=(bsz, 2, NBLK),
        in_specs=[
            pl.BlockSpec((None, BLK, KB), lambda b, dd, s: (b, blk(dd, s), C_KB // KB)),
            pl.BlockSpec((None, BLK, KB), lambda b, dd, s: (b, blk(dd, s), C_QB // KB)),
            pl.BlockSpec((None, BLK, VB), lambda b, dd, s: (b, blk(dd, s), C_VB // VB)),
            pl.BlockSpec((None, BLK, 128), lambda b, dd, s: (b, blk(dd, s), C_DEC // 128)),
            pl.BlockSpec((None, 128, KB), lambda b, dd, s: (dd, 0, 0)),
            pl.BlockSpec((None, 1, KB), lambda b, dd, s: (dd, 0, 0)),
            pl.BlockSpec((3, BLK, DK_B), lambda b, dd, s: (0, blk(dd, s), 0)),
            pl.BlockSpec((None, (len(GLA_FINE) + 1) * BLK, BLK), lambda b, dd, s: (dd, 0, 0)),
            pl.BlockSpec((None, BLK // 2, BLK // 2), lambda b, dd, s: (dd, 0, 0)),
        ],
        out_specs=pl.BlockSpec((None, None, BLK, VB), lambda b, dd, s: (dd, b, blk(dd, s), 0)),
        out_shape=jax.ShapeDtypeStruct((2, bsz, TT, VB), BF16),
        scratch_shapes=[pltpu.VMEM((HB, DV_B, DK_B), F32)],
        compiler_params=_cp(3), name="gla",
    )(P, P, P, P, wd_pad, bd, rope, mstack, lev_ids)


def _rope_tables():
    t = np.arange(T)
    half = DK_B // 2
    inv = np.float32(ROPE_BASE) ** (-np.arange(0, half, 2, dtype=np.float32) / np.float32(half))

    def tab(pos):
        ang = pos.astype(np.float32)[:, None] * inv[None, :]
        return np.concatenate([ang, ang], axis=-1)

    ang = np.concatenate([tab(t // GRID_W), tab(t % GRID_W)], axis=-1)
    cos, sin = np.cos(ang), np.sin(ang)
    first = (np.arange(DK_B) % half) < (half // 2)
    sa = np.where(first[None, :], -sin, 0.0)
    sb = np.where(first[None, :], 0.0, sin)
    ident = np.ones((L, DK_B), np.float32)
    zero = np.zeros((L, DK_B), np.float32)
    tabs = np.stack([np.concatenate([cos, ident]), np.concatenate([sa, zero]), np.concatenate([sb, zero])])
    return jnp.asarray(tabs, F32)


def _gelu(x):
    return 0.5 * x * (1.0 + jnp.tanh(0.7978845608028654 * (x + 0.044715 * (x * x * x))))


def _gmlp_body(u_ref, v_ref, g_ref, b_ref, w_ref, bs_ref, o_ref):
    for ch in range(BLK // MIX_CHUNK):
        rs = slice(ch * MIX_CHUNK, (ch + 1) * MIX_CHUNK)
        u = _gelu(u_ref[rs, :].astype(F32))
        v = _gelu(v_ref[rs, :].astype(F32))
        for g in range(GC):
            cs = slice(g * CG, (g + 1) * CG)
            vg = v[:, cs]
            mu = jnp.mean(vg, axis=-1, keepdims=True)
            var = jnp.mean(jnp.square(vg - mu), axis=-1, keepdims=True)
            vn = (vg - mu) * lax.rsqrt(var + EPS) * g_ref[:, cs] + b_ref[:, cs]
            s = _dot(w_ref[g].astype(BF16), vn.astype(BF16)) + bs_ref[g]
            o_ref[rs, cs] = (u[:, cs] * s).astype(BF16)


def _out_kernel(*refs, split):
    oa_ref, og_ref, gb_ref, oc_ref, w_ref = refs[:5]
    if split:
        x_in = _stream_block(refs[5], refs[6])
    else:
        x_in = refs[5][...]
    gg_ref, gpost_ref, gpre_ref, mod_ref, wr_ref, xo_ref, h2_ref, lg_ref = refs[6 + split:]
    o = og_ref[0].astype(F32) + og_ref[1].astype(F32)
    gb = gb_ref[...].astype(F32)
    parts = []
    for h in range(HB):
        cs = slice(h * DV_B, (h + 1) * DV_B)
        gh = gb[:, cs]
        parts.append((_rms(o[:, cs]) * gg_ref[:, cs] * (gh * _sigmoid(gh))).astype(BF16))
    mixed = jnp.concatenate([oa_ref[...]] + parts + [oc_ref[...]], axis=-1)
    y = _dot(mixed, w_ref[...])
    x1 = x_in + mod_ref[2:3, :] * (_rms(y) * gpost_ref[...])
    xo_ref[...] = x1
    h2 = ((_rms(x1) * gpre_ref[...]) * (1.0 + mod_ref[4:5, :]) + mod_ref[3:4, :]).astype(BF16)
    h2_ref[...] = h2
    wh, wl = _split_bf16(wr_ref[...])
    lg_ref[...] = _dot(h2, wh) + _dot(h2, wl)


def _out_call(o_a, o_g, P, o_c, w_out_b, stream, g_gla, g_post, g_pre, modtab, wr_pad, n_blocks):
    bsz = P.shape[0]
    rows = n_blocks * BLK
    split = isinstance(stream, tuple)
    rowspec = lambda w: pl.BlockSpec((None, BLK, w), lambda b, j: (b, j, 0))
    vec = lambda w: pl.BlockSpec((1, w), lambda b, j: (0, 0))
    return pl.pallas_call(
        functools.partial(_out_kernel, split=split),
        grid=(bsz, n_blocks),
        in_specs=[rowspec(WA),
                  pl.BlockSpec((2, None, BLK, VB), lambda b, j: (0, b, j, 0)),
                  pl.BlockSpec((None, BLK, VB), lambda b, j: (b, j, C_GB // VB)),
                  rowspec(WC),
                  pl.BlockSpec((D, D), lambda b, j: (0, 0))]
                 + (_stream_specs() if split else [rowspec(D)])
                 + [vec(VB), vec(D), vec(D),
                    pl.BlockSpec((None, None, 8, D), lambda b, j: (b, j // NLAT, 0, 0)),
                    pl.BlockSpec((D, 128), lambda b, j: (0, 0))],
        out_specs=[rowspec(D), rowspec(D), rowspec(128)],
        out_shape=[jax.ShapeDtypeStruct((bsz, rows, D), F32),
                   jax.ShapeDtypeStruct((bsz, rows, D), BF16),
                   jax.ShapeDtypeStruct((bsz, rows, 128), F32)],
        compiler_params=_cp(2), name="proj_out",
    )(o_a, o_g, P, o_c, w_out_b, *(stream if split else (stream,)), g_gla.reshape(1, VB),
      g_post.reshape(1, D), g_pre.reshape(1, D), modtab, wr_pad)


def _route_set(lg, cap, upper):
    n = lg.shape[0]
    lt = lg.T[:N_EXPERTS, :]
    ex = jnp.exp(lt - lt.max(axis=0, keepdims=True))
    aff = ex / ex.sum(axis=0, keepdims=True)
    bits = pltpu.bitcast(aff, jnp.int32)
    capf = jnp.float32(cap)

    def keep(cand, prefix):
        cnt = jnp.sum(jnp.where(bits >= cand, 1.0, 0.0), axis=1, keepdims=True)
        return jnp.where(cnt >= capf, cand, prefix)

    def two_bits(i, prefix):
        lo = jnp.left_shift(jnp.int32(1), 29 - 2 * i)
        best = keep(prefix | lo, prefix)
        best = keep(prefix | (lo + lo), best)
        return keep(prefix | (lo + lo + lo), best)

    thr = lax.fori_loop(0, 15, two_bits, jnp.zeros((N_EXPERTS, 1), jnp.int32))
    thr = keep(thr | 1, thr)
    gt = jnp.where(bits > thr, 1.0, 0.0)
    eq = jnp.where(bits == thr, 1.0, 0.0)
    need = capf - gt.sum(axis=1, keepdims=True)
    rank_eq = _dot(eq.astype(BF16), upper)
    sel = gt + eq * jnp.where(rank_eq < need, 1.0, 0.0)
    before = _dot(sel.astype(BF16), upper)
    slot = jnp.where(sel > 0.5, before, -1.0)
    pad = jnp.full((128 - SCATTER_EXPERTS, n), -1.0, F32)
    slot_t = [jnp.concatenate([slot[g * SCATTER_EXPERTS:(g + 1) * SCATTER_EXPERTS], pad], axis=0).T
              for g in range(N_EXPERTS // SCATTER_EXPERTS)]
    tile_start = jnp.concatenate([before[:, r * BLK:r * BLK + 1] for r in range(n // BLK)]
                                 + [jnp.full((N_EXPERTS, 128 - n // BLK), capf, F32)], axis=1)
    return slot, aff, slot_t, tile_start


def _route_kernel(lg_ref, u_ref, *out_refs, with_ctx):
    sl, af, st, ts = _route_set(lg_ref[0:T, :], CAP_LAT, u_ref[...])
    out_refs[0][...] = sl
    out_refs[1][...] = af
    for g, s in enumerate(st):
        out_refs[2][g, 0:T, :] = s
    out_refs[3][...] = ts
    if with_ctx:
        sl, af, st, _ = _route_set(lg_ref[T:TT, :], CAP_CTX, u_ref[0:L, 0:L])
        out_refs[4][...] = sl
        out_refs[5][...] = af
        for g, s in enumerate(st):
            out_refs[2][g, T:TT, :] = s


def _route_call(lg, upper, with_ctx):
    bsz, rows, _ = lg.shape
    ng = N_EXPERTS // SCATTER_EXPERTS
    en = lambda n: pl.BlockSpec((None, N_EXPERTS, n), lambda b: (b, 0, 0))
    out_specs = [en(T), en(T), pl.BlockSpec((None, ng, rows, 128), lambda b: (b, 0, 0, 0)), en(128)]
    out_shape = [jax.ShapeDtypeStruct((bsz, N_EXPERTS, T), F32),
                 jax.ShapeDtypeStruct((bsz, N_EXPERTS, T), F32),
                 jax.ShapeDtypeStruct((bsz, ng, rows, 128), F32),
                 jax.ShapeDtypeStruct((bsz, N_EXPERTS, 128), F32)]
    if with_ctx:
        out_specs += [en(L), en(L)]
        out_shape += [jax.ShapeDtypeStruct((bsz, N_EXPERTS, L), F32)] * 2
    return pl.pallas_call(
        functools.partial(_route_kernel, with_ctx=with_ctx),
        grid=(bsz,),
        in_specs=[pl.BlockSpec((None, rows, 128), lambda b: (b, 0, 0)),
                  pl.BlockSpec((T, T), lambda b: (0, 0))],
        out_specs=out_specs, out_shape=out_shape,
        compiler_params=_cp(1), name="route",
    )(lg, upper)


def _gather_kernel(*refs, n_sets):
    ins, outs = refs[:3 * n_sets], refs[3 * n_sets:]
    for k in range(n_sets):
        slot_ref, aff_ref, h_ref = ins[3 * k:3 * k + 3]
        xs_ref, g_ref = outs[2 * k:2 * k + 2]
        cap, n = xs_ref.shape[1], h_ref.shape[0]
        sid = lax.broadcasted_iota(jnp.int32, (cap, n), 0).astype(F32)
        picks = []
        for j in range(GATHER_EXPERTS):
            e = pl.program_id(1) * GATHER_EXPERTS + j
            hit = sid == slot_ref[pl.ds(e, 1), :]
            picks.append(jnp.where(hit, 1.0, 0.0).astype(BF16))
            g = jnp.sum(jnp.where(hit, aff_ref[pl.ds(e, 1), :], 0.0), axis=1, keepdims=True)
            g_ref[j] = jnp.broadcast_to(g, (cap, 128))
        xs = _dot(jnp.concatenate(picks, axis=0), h_ref[...]).astype(BF16)
        xs_ref[...] = xs.reshape(GATHER_EXPERTS, cap, D)


def _gather_call(h2, sets):
    bsz = h2.shape[0]
    in_specs, args, out_specs, out_shape = [], [], [], []
    for slot, aff, n, cap, blk_idx in sets:
        in_specs += [pl.BlockSpec((None, N_EXPERTS, n), lambda b, e: (b, 0, 0)),
                     pl.BlockSpec((None, N_EXPERTS, n), lambda b, e: (b, 0, 0)),
                     pl.BlockSpec((None, n, D), lambda b, e, i=blk_idx: (b, i, 0))]
        args += [slot, aff, h2]
        out_specs += [pl.BlockSpec((GATHER_EXPERTS, cap, D), lambda b, e: (e, b, 0)),
                      pl.BlockSpec((GATHER_EXPERTS, cap, 128), lambda b, e: (e, b, 0))]
        out_shape += [jax.ShapeDtypeStruct((N_EXPERTS, bsz * cap, D), BF16),
                      jax.ShapeDtypeStruct((N_EXPERTS, bsz * cap, 128), F32)]
    return pl.pallas_call(
        functools.partial(_gather_kernel, n_sets=len(sets)),
        grid=(bsz, N_EXPERTS // GATHER_EXPERTS),
        in_specs=in_specs, out_specs=out_specs, out_shape=out_shape,
        compiler_params=_cp(2), name="gather",
    )(*args)


def _window_starts(start_ref, b, first_expert, r, ne):
    first, fits = [], None
    for j in range(ne):
        lo = start_ref[b, first_expert + j, r]
        hi = start_ref[b, first_expert + j, r + 1]
        f = jnp.minimum((lo // 16) * 16, CAP_LAT - SCATTER_WIN)
        first.append(f)
        ok = hi <= f + SCATTER_WIN
        fits = ok if fits is None else jnp.logical_and(fits, ok)
    return first, fits


def _gather_tiles_kernel(start_ref, slot_ref, aff_ref, h_ref, xs_ref, g_ref):
    ne = SCATTER_EXPERTS
    b, eg, r = pl.program_id(0), pl.program_id(1), pl.program_id(2)

    @pl.when(r == 0)
    def _():
        xs_ref[...] = jnp.zeros_like(xs_ref)
        g_ref[...] = jnp.zeros_like(g_ref)

    first, fits = _window_starts(start_ref, b, eg * ne, r, ne)

    def put(rows, row0):
        sid = lax.broadcasted_iota(jnp.int32, (rows, BLK), 0).astype(F32)
        picks, where_to = [], []
        for j in range(ne):
            e = eg * ne + j
            hit = sid == slot_ref[pl.ds(e, 1), :] - row0[j].astype(F32)
            picks.append(jnp.where(hit, 1.0, 0.0).astype(BF16))
            g = jnp.sum(jnp.where(hit, aff_ref[pl.ds(e, 1), :], 0.0), axis=1, keepdims=True)
            ws = pl.ds(pl.multiple_of(row0[j], 16), rows)
            g_ref[j, ws, :] += jnp.broadcast_to(g, (rows, 128))
            where_to.append(ws)
        res = _dot(jnp.concatenate(picks, axis=0), h_ref[...])
        for j, ws in enumerate(where_to):
            xs_ref[j, ws, :] = (xs_ref[j, ws, :].astype(F32) + res[j * rows:(j + 1) * rows]).astype(BF16)

    @pl.when(fits)
    def _():
        put(SCATTER_WIN, first)

    @pl.when(jnp.logical_not(fits))
    def _():
        put(CAP_LAT, [jnp.int32(0)] * ne)


def _gather_tiles_call(tile_start, slot, aff, h2):
    bsz = h2.shape[0]
    ne = SCATTER_EXPERTS
    return pl.pallas_call(
        _gather_tiles_kernel,
        grid=(bsz, N_EXPERTS // ne, NLAT),
        in_specs=[pl.BlockSpec(memory_space=pltpu.SMEM),
                  pl.BlockSpec((None, N_EXPERTS, BLK), lambda b, g, r: (b, 0, r)),
                  pl.BlockSpec((None, N_EXPERTS, BLK), lambda b, g, r: (b, 0, r)),
                  pl.BlockSpec((None, BLK, D), lambda b, g, r: (b, r, 0))],
        out_specs=[pl.BlockSpec((ne, CAP_LAT, D), lambda b, g, r: (g, b, 0)),
                   pl.BlockSpec((ne, CAP_LAT, 128), lambda b, g, r: (g, b, 0))],
        out_shape=[jax.ShapeDtypeStruct((N_EXPERTS, bsz * CAP_LAT, D), BF16),
                   jax.ShapeDtypeStruct((N_EXPERTS, bsz * CAP_LAT, 128), F32)],
        compiler_params=_cp(3), name="gather_tiles",
    )(tile_start[:, :, :NLAT + 1].astype(jnp.int32), slot, aff, h2)


def _ffn_kernel(*refs, rl, rc, nf):
    if rc:
        xl_ref, xc_ref, gl_ref, gc_ref, wg_ref, wu_ref, wd_ref, yl_ref, yc_ref, hid = refs
        groups = ((xl_ref, gl_ref, yl_ref, 0, rl), (xc_ref, gc_ref, yc_ref, rl, rc))
    else:
        xl_ref, gl_ref, wg_ref, wu_ref, wd_ref, yl_ref, hid = refs
        groups = ((xl_ref, gl_ref, yl_ref, 0, rl),)
    s = pl.program_id(1)

    @pl.when(s < nf)
    def _():
        wg = wg_ref[...].astype(BF16)
        wu = wu_ref[...].astype(BF16)
        for x_ref, _, _, r0, nr in groups:
            xv = x_ref[...]
            a = _dot(xv, wg)
            u = _dot(xv, wu)
            hid[s, r0:r0 + nr, :] = ((a * _sigmoid(a)) * u).astype(BF16)

    @pl.when(s >= nf)
    def _():
        wd = wd_ref[...].astype(BF16)
        for _, g_ref, y_ref, r0, nr in groups:
            hm = jnp.concatenate([hid[k, r0:r0 + nr, :] for k in range(nf)], axis=1)
            y = _dot(hm, wd)
            for c in range(FF_TILE // 128):
                cs = slice(128 * c, 128 * (c + 1))
                y_ref[:, cs] = (y[:, cs] * g_ref[...]).astype(BF16)


def _ffn_call(layer, xs_l, g_l, xs_c, g_c, w_gate, w_up, w_down):
    rl = xs_l.shape[1]
    rc = 0 if xs_c is None else xs_c.shape[1]
    nf = EXPERT_FF // FF_TILE
    nd = D // FF_TILE
    up = lambda e, s: (layer, e, 0, jnp.minimum(s, nf - 1))
    down = lambda s: jnp.maximum(s - nf, 0)
    in_specs = [pl.BlockSpec((None, rl, D), lambda e, s: (e, 0, 0))]
    args = [xs_l]
    if rc:
        in_specs.append(pl.BlockSpec((None, rc, D), lambda e, s: (e, 0, 0)))
        args.append(xs_c)
    in_specs.append(pl.BlockSpec((None, rl, 128), lambda e, s: (e, 0, 0)))
    args.append(g_l)
    if rc:
        in_specs.append(pl.BlockSpec((None, rc, 128), lambda e, s: (e, 0, 0)))
        args.append(g_c)
    in_specs += [pl.BlockSpec((None, None, D, FF_TILE), up),
                 pl.BlockSpec((None, None, D, FF_TILE), up),
                 pl.BlockSpec((None, None, EXPERT_FF, FF_TILE), lambda e, s: (layer, e, 0, down(s)))]
    args += [w_gate, w_up, w_down]
    out_specs = [pl.BlockSpec((None, rl, FF_TILE), lambda e, s: (e, 0, down(s)))]
    out_shape = [jax.ShapeDtypeStruct((N_EXPERTS, rl, D), BF16)]
    if rc:
        out_specs.append(pl.BlockSpec((None, rc, FF_TILE), lambda e, s: (e, 0, down(s))))
        out_shape.append(jax.ShapeDtypeStruct((N_EXPERTS, rc, D), BF16))
    return pl.pallas_call(
        functools.partial(_ffn_kernel, rl=rl, rc=rc, nf=nf),
        grid=(N_EXPERTS, nf + nd),
        in_specs=in_specs, out_specs=out_specs, out_shape=out_shape,
        scratch_shapes=[pltpu.VMEM((nf, rl + rc, FF_TILE), BF16)],
        compiler_params=_cp(2), name="expert_ffn",
    )(*args)


def _scatter_kernel(*refs, ne, with_ctx, has_next):
    refs = list(refs)
    start_ref, slot_ref, yl_ref = refs[:3]
    refs = refs[3:]
    yc_ref = refs.pop(0) if with_ctx else None
    x_ref, gpost_ref, mod_ref = refs[:3]
    refs = refs[3:]
    if has_next:
        gpre_ref, modn_ref = refs[:2]
        refs = refs[2:]
    xo_ref = refs[0]
    hn_ref = refs[1] if has_next else None
    acc = refs[-1]
    eg = pl.program_id(1)
    r = pl.program_id(2)
    ng = pl.num_programs(1)

    def add(cap, y_ref):
        kk = ne * cap
        if cap % 128 == 0:
            want = lax.broadcasted_iota(jnp.int32, (BLK, cap), 1).astype(F32)
            pt = jnp.concatenate([jnp.where(slot_ref[:, j:j + 1] == want, 1.0, 0.0).astype(BF16)
                                  for j in range(ne)], axis=1)
        else:
            er = lax.broadcasted_iota(jnp.int32, (128, kk), 0)
            ec = lax.broadcasted_iota(jnp.int32, (128, kk), 1)
            expand = jnp.where(er == ec // cap, 1.0, 0.0).astype(BF16)
            spread = _dot(slot_ref[...].astype(BF16), expand)
            want = (lax.broadcasted_iota(jnp.int32, (BLK, kk), 1) % cap).astype(F32)
            pt = jnp.where(spread == want, 1.0, 0.0).astype(BF16)
        acc[r] += _dot(pt, y_ref[...].reshape(kk, D))

    def add_windows(first):
        lane = lax.broadcasted_iota(jnp.int32, (BLK, 128), 1).astype(F32)
        low = lane < SCATTER_WIN
        tiles = []
        for j in range(0, ne, 2):
            rel_a = slot_ref[:, j:j + 1] - first[j].astype(F32)
            rel_b = slot_ref[:, j + 1:j + 2] - first[j + 1].astype(F32) + SCATTER_WIN
            tiles.append(jnp.where(jnp.where(low, rel_a, rel_b) == lane, 1.0, 0.0).astype(BF16))
        yw = jnp.concatenate([yl_ref[j, pl.ds(pl.multiple_of(first[j], 16), SCATTER_WIN), :]
                              for j in range(ne)], axis=0)
        acc[r] += _dot(jnp.concatenate(tiles, axis=1), yw)

    @pl.when(eg == 0)
    def _():
        acc[r] = jnp.zeros((BLK, D), F32)

    @pl.when(r < NLAT)
    def _():
        first, fits = _window_starts(start_ref, pl.program_id(0), eg * ne, r, ne)

        @pl.when(fits)
        def _():
            add_windows(first)

        @pl.when(jnp.logical_not(fits))
        def _():
            add(CAP_LAT, yl_ref)

    if with_ctx:
        @pl.when(r == NLAT)
        def _():
            add(CAP_CTX, yc_ref)

    @pl.when(eg == ng - 1)
    def _():
        x2 = x_ref[...] + mod_ref[5:6, :] * (_rms(acc[r]) * gpost_ref[...])
        xo_ref[...] = x2
        if has_next:
            hn = (_rms(x2) * gpre_ref[...]) * (1.0 + modn_ref[1:2, :]) + modn_ref[0:1, :]
            hn_ref[...] = hn.astype(BF16)


def _scatter_call(tile_start, slot_t, y_l, y_c, x_mid, g_post, modtab, g_pre_next, modtab_next):
    bsz = x_mid.shape[0]
    with_ctx = y_c is not None
    has_next = g_pre_next is not None
    n_tiles = NBLK if with_ctx else NLAT
    ne = SCATTER_EXPERTS
    ng = N_EXPERTS // ne
    vec = pl.BlockSpec((1, D), lambda b, g, r: (0, 0))
    modspec = pl.BlockSpec((None, None, 8, D), lambda b, g, r: (b, r // NLAT, 0, 0))
    late = lambda b, g, r: (b, jnp.where(g == ng - 1, r, 0), 0)
    in_specs = [pl.BlockSpec(memory_space=pltpu.SMEM),
                pl.BlockSpec((None, None, BLK, 128), lambda b, g, r: (b, g, r, 0)),
                pl.BlockSpec((ne, CAP_LAT, D), lambda b, g, r: (g, b, 0))]
    args = [tile_start[:, :, :NLAT + 1].astype(jnp.int32), slot_t, y_l]
    if with_ctx:
        in_specs.append(pl.BlockSpec((ne, CAP_CTX, D), lambda b, g, r: (g, b, 0)))
        args.append(y_c)
    in_specs += [pl.BlockSpec((None, BLK, D), late), vec, modspec]
    args += [x_mid, g_post.reshape(1, D), modtab]
    if has_next:
        in_specs += [vec, modspec]
        args += [g_pre_next.reshape(1, D), modtab_next]
    out_specs = [pl.BlockSpec((None, BLK, D), late)]
    out_shape = [jax.ShapeDtypeStruct((bsz, n_tiles * BLK, D), F32)]
    if has_next:
        out_specs.append(pl.BlockSpec((None, BLK, D), late))
        out_shape.append(jax.ShapeDtypeStruct((bsz, n_tiles * BLK, D), BF16))
    return pl.pallas_call(
        functools.partial(_scatter_kernel, ne=ne, with_ctx=with_ctx, has_next=has_next),
        grid=(bsz, ng, n_tiles),
        in_specs=in_specs, out_specs=out_specs, out_shape=out_shape,
        scratch_shapes=[pltpu.VMEM((n_tiles, BLK, D), F32)],
        compiler_params=_cp(3), name="scatter",
    )(*args)


def _pack_kernel(w_ref, o_ref):
    src = np.cumsum((0, WA, WA, WA, KB, VB, 2 * DEC_RANK, KB, VB, WC, WC))
    order = ((4, C_VB), (7, C_GB), (0, C_QA), (1, C_KA), (2, C_VA), (3, C_KB), (6, C_QB), (8, C_UC),
             (9, C_VC), (5, C_DEC))
    for seg, dst in order:
        lo, hi = int(src[seg]), int(src[seg + 1])
        o_ref[:, dst:dst + hi - lo] = w_ref[:, lo:hi].astype(BF16)
    tail = C_DEC + 2 * DEC_RANK
    o_ref[:, tail:] = jnp.zeros((o_ref.shape[0], NP - tail), BF16)


def _pack_w_in(w_in, layer):
    rows = 256
    n_in = w_in.shape[-1]
    return pl.pallas_call(
        _pack_kernel,
        grid=(D // rows,),
        in_specs=[pl.BlockSpec((None, rows, n_in), lambda i: (layer, i, 0))],
        out_specs=pl.BlockSpec((rows, NP), lambda i: (i, 0)),
        out_shape=jax.ShapeDtypeStruct((D, NP), BF16),
        compiler_params=_cp(1), name="pack_w_in",
    )(w_in)


def kernel(x, c, ctx, c_ctx, w_ada, b_ada, g_pre_mix, g_post_mix, g_pre_ffn, g_post_ffn, w_in, w_dec,
           b_dec, rpb, g_gla, ln_v_g, ln_v_b, w_sp, b_sp, w_out, w_router, w_gate, w_up, w_down):
    bsz = x.shape[0]
    c_all = jnp.zeros((16, D), F32).at[:bsz].set(c).at[8].set(c_ctx)
    mods = _ada_call(c_all, w_ada, b_ada).reshape(DEPTH, 16, 6, D)
    lat = mods[:, :bsz]
    cx = jnp.broadcast_to(mods[:, 8:9], lat.shape)
    modtab = jnp.pad(jnp.stack([lat, cx], axis=2), ((0, 0), (0, 0), (0, 0), (0, 2), (0, 0)))

    rope = _rope_tables()
    mstack = jnp.asarray(_gla_matrices(), BF16)
    lev_ids = jnp.asarray(_gla_level_ids())
    hmask = jnp.asarray(np.stack([(np.arange(128) < DH_A), (np.arange(128) >= DH_A)]) * DH_A ** -0.5, F32)
    upper = jnp.asarray(np.triu(np.ones((T, T), np.float32), 1), BF16)

    stream = (x, ctx)
    h = _prenorm_call(x, ctx, g_pre_mix[0], modtab[0])
    for l in range(DEPTH):
        last = l == DEPTH - 1
        nb = NLAT if last else NBLK
        P = _proj_call(h.reshape(bsz * TT, D), _pack_w_in(w_in, l)).reshape(bsz, TT, NP)
        bs_rep = jnp.broadcast_to(b_sp[l][:, :, None], (GC, MIX_CHUNK, 128))
        o_a, o_c = _na_gmlp_call(P, _na_bias(rpb[l]), hmask, ln_v_g[l], ln_v_b[l], w_sp[l], bs_rep, nb)
        wd_pad = jnp.zeros((2, 128, KB), F32)
        wd_pad = wd_pad.at[0, :DEC_RANK].set(w_dec[l, 0]).at[1, DEC_RANK:2 * DEC_RANK].set(w_dec[l, 1])
        o_g = _gla_call(P, wd_pad, b_dec[l].reshape(2, 1, KB), rope, mstack, lev_ids)
        wr_pad = jnp.pad(w_router[l], ((0, 0), (0, 128 - N_EXPERTS)))
        x_mid, h2, lg = _out_call(o_a, o_g, P, o_c, w_out[l].astype(BF16), stream, g_gla[l], g_post_mix[l],
                                  g_pre_ffn[l], modtab[l], wr_pad, nb)
        routed = _route_call(lg, upper, not last)
        slot_l, aff_l, slot_t, tile_start = routed[:4]
        xs_l, gt_l = _gather_tiles_call(tile_start, slot_l, aff_l, h2)
        if last:
            (y_l,) = _ffn_call(l, xs_l, gt_l, None, None, w_gate, w_up, w_down)
            (stream,) = _scatter_call(tile_start, slot_t, y_l, None, x_mid, g_post_ffn[l], modtab[l], None,
                                      None)
        else:
            slot_c, aff_c = routed[4:]
            xs_c, gt_c = _gather_call(h2, [(slot_c, aff_c, L, CAP_CTX, NLAT)])
            y_l, y_c = _ffn_call(l, xs_l, gt_l, xs_c, gt_c, w_gate, w_up, w_down)
            stream, h = _scatter_call(tile_start, slot_t, y_l, y_c, x_mid, g_post_ffn[l], modtab[l],
                                      g_pre_mix[l + 1], modtab[l + 1])
    return stream
```

```python
import functools

import numpy as np
import jax
import jax.numpy as jnp
from jax import lax
from jax.experimental import pallas as pl
from jax.experimental.pallas import tpu as pltpu

F32 = jnp.float32
BF16 = jnp.bfloat16

D = 2048
T = 2048
L = 256
TT = T + L
BLK = 256
NBLK = TT // BLK
NLAT = T // BLK
DEPTH = 2
GRID_W = 64
ROWS = T // GRID_W
HA, DH_A = 8, 64
WIN_ROWS, WIN_COLS = 8, 16
HB, DK_B, DV_B = 4, 128, 256
DEC_RANK = 16
GATE_TAU = 16.0
GC, CG, MIX_CHUNK = 4, 128, 128
N_EXPERTS = 16
EXPERT_FF = 2048
CAP_FACTOR = 2
ROPE_BASE = 10000.0
EPS = 1e-6
LOG2E = 1.4426950408889634
WA, KB, VB, WC = HA * DH_A, HB * DK_B, HB * DV_B, GC * CG

NP = 3 * D
C_VB, C_GB, C_QA, C_KA, C_VA, C_KB, C_QB, C_UC, C_VC, C_DEC = (
    0, 1024, 2048, 2560, 3072, 3584, 4096, 4608, 5120, 5632)

NA_QROWS = 4
NA_KROWS = 12
NA_KEYS = NA_KROWS * GRID_W
NEG = -1e30

GLA_LEVELS = 8
GLA_FINE = (5, 6, 7)
FF_TILE = 256
SCATTER_EXPERTS = 8
SCATTER_WIN = 64
GATHER_EXPERTS = 2
CAP_LAT = CAP_FACTOR * T // N_EXPERTS
CAP_CTX = CAP_FACTOR * L // N_EXPERTS
VMEM_LIMIT = 56 * 1024 * 1024


def _cp(n_axes):
    return pltpu.CompilerParams(dimension_semantics=("arbitrary",) * n_axes,
                                vmem_limit_bytes=VMEM_LIMIT)


def _dot(a, b):
    return jnp.dot(a, b, preferred_element_type=F32)


def _dot_nt(a, b):
    return lax.dot_general(a, b, (((1,), (1,)), ((), ())), preferred_element_type=F32)


def _dot_tn(a, b):
    return lax.dot_general(a, b, (((0,), (0,)), ((), ())), preferred_element_type=F32)


def _rms(x):
    return x * lax.rsqrt(jnp.mean(x * x, axis=-1, keepdims=True) + EPS)


def _sigmoid(x):
    return 1.0 / (1.0 + jnp.exp(-x))


def _split_bf16(x):
    hi = x.astype(BF16)
    lo = (x - hi.astype(F32)).astype(BF16)
    return hi, lo


def _ada_kernel(c_ref, w_ref, b_ref, o_ref):
    cv = c_ref[...]
    hi, lo = _split_bf16(cv * _sigmoid(cv))
    r = _dot(jnp.concatenate([hi, lo], axis=0), w_ref[...].astype(BF16))
    o_ref[...] = r[:16] + r[16:] + b_ref[...]


def _ada_call(c_all, w_ada, b_ada):
    tn = 1024
    n6 = w_ada.shape[-1]
    return pl.pallas_call(
        _ada_kernel,
        grid=(DEPTH, n6 // tn),
        in_specs=[pl.BlockSpec((16, D), lambda l, n: (0, 0)),
                  pl.BlockSpec((None, D, tn), lambda l, n: (l, 0, n)),
                  pl.BlockSpec((None, 1, tn), lambda l, n: (l, 0, n))],
        out_specs=pl.BlockSpec((None, 16, tn), lambda l, n: (l, 0, n)),
        out_shape=jax.ShapeDtypeStruct((DEPTH, 16, n6), F32),
        compiler_params=_cp(2), name="ada",
    )(c_all, w_ada, b_ada.reshape(DEPTH, 1, n6))


def _stream_block(x_ref, c_ref):
    return jnp.where(pl.program_id(1) < NLAT, x_ref[...], c_ref[...])


def _stream_specs():
    return [pl.BlockSpec((None, BLK, D), lambda b, j: (b, jnp.minimum(j, NLAT - 1), 0)),
            pl.BlockSpec((None, BLK, D), lambda b, j: (b, 0, 0))]


def _prenorm_kernel(x_ref, c_ref, g_ref, mod_ref, h_ref):
    y = _rms(_stream_block(x_ref, c_ref)) * g_ref[...]
    h_ref[...] = (y * (1.0 + mod_ref[1:2, :]) + mod_ref[0:1, :]).astype(BF16)


def _prenorm_call(x, ctx, g, modtab):
    bsz = x.shape[0]
    return pl.pallas_call(
        _prenorm_kernel,
        grid=(bsz, NBLK),
        in_specs=_stream_specs() + [
            pl.BlockSpec((1, D), lambda b, j: (0, 0)),
            pl.BlockSpec((None, None, 8, D), lambda b, j: (b, j // NLAT, 0, 0))],
        out_specs=pl.BlockSpec((None, BLK, D), lambda b, j: (b, j, 0)),
        out_shape=jax.ShapeDtypeStruct((bsz, TT, D), BF16),
        compiler_params=_cp(2), name="prenorm",
    )(x, ctx, g.reshape(1, D), modtab)


def _proj_kernel(h_ref, w_ref, o_ref):
    o_ref[...] = _dot(h_ref[...], w_ref[...]).astype(BF16)


def _proj_call(h2d, w_pack):
    m = h2d.shape[0]
    tm = 1024 if m % 1024 == 0 else 768
    tn = NP // 3
    return pl.pallas_call(
        _proj_kernel,
        grid=(NP // tn, m // tm),
        in_specs=[pl.BlockSpec((tm, D), lambda n, i: (i, 0)),
                  pl.BlockSpec((D, tn), lambda n, i: (0, n))],
        out_specs=pl.BlockSpec((tm, tn), lambda n, i: (i, n)),
        out_shape=jax.ShapeDtypeStruct((m, NP), BF16),
        compiler_params=_cp(2), name="proj_in",
    )(h2d, w_pack)


def _softmax_pv(s_list, v_list):
    m = s_list[0].max(axis=-1, keepdims=True)
    for s in s_list[1:]:
        m = jnp.maximum(m, s.max(axis=-1, keepdims=True))
    acc = None
    for s, v in zip(s_list, v_list):
        o = _dot(jnp.exp(s - m).astype(BF16), v)
        acc = o if acc is None else acc + o
    return acc[:, :128] / acc[:, 128:]


def _na_kernel(q_ref, k_ref, v_ref, bias_ref, hm_ref, *mlp_refs_o_ref):
    mlp_in, o_ref, oc_ref = mlp_refs_o_ref[:-2], mlp_refs_o_ref[-2], mlp_refs_o_ref[-1]
    j = pl.program_id(1)
    lane = lax.broadcasted_iota(jnp.int32, (BLK, 128), 1)
    low = lane < DH_A

    def run(local_start):
        _gmlp_body(*mlp_in, oc_ref)
        for p in range(HA // 2):
            sl = slice(128 * p, 128 * p + 128)
            q2 = q_ref[:, sl]
            kc = k_ref[T:TT, sl]
            vc = jnp.concatenate([v_ref[T:TT, sl], jnp.ones((L, 128), BF16)], axis=1)
            if local_start is not None:
                kl = k_ref[pl.ds(local_start, NA_KEYS), sl]
                vl = jnp.concatenate([v_ref[pl.ds(local_start, NA_KEYS), sl],
                                      jnp.ones((NA_KEYS, 128), BF16)], axis=1)
            pair = []
            for hh in range(2):
                qm = (q2.astype(F32) * hm_ref[hh:hh + 1, :]).astype(BF16)
                s_ctx = _dot_nt(qm, kc)
                if local_start is not None:
                    s_loc = _dot_nt(qm, kl) + bias_ref[2 * p + hh]
                    pair.append(_softmax_pv([s_loc, s_ctx], [vl, vc]))
                else:
                    pair.append(_softmax_pv([s_ctx], [vc]))
            o_ref[:, sl] = jnp.where(low, pair[0], pair[1]).astype(BF16)

    @pl.when(j < NLAT)
    def _():
        krow = jnp.clip(j * NA_QROWS - WIN_ROWS // 2, 0, ROWS - NA_KROWS)
        run(pl.multiple_of(krow * GRID_W, GRID_W))

    @pl.when(j == NLAT)
    def _():
        run(None)


def _na_gmlp_call(P, bias, hmask, ln_g, ln_b, w_sp, bs_rep, n_blocks):
    bsz = P.shape[0]

    def bias_idx(b, j):
        return (jnp.where(j == 0, 0, jnp.where(j == NLAT - 1, 2, 1)), 0, 0, 0)

    rows = n_blocks * BLK
    return pl.pallas_call(
        _na_kernel,
        grid=(bsz, n_blocks),
        in_specs=[pl.BlockSpec((None, BLK, WA), lambda b, j: (b, j, C_QA // WA)),
                  pl.BlockSpec((None, TT, WA), lambda b, j: (b, 0, C_KA // WA)),
                  pl.BlockSpec((None, TT, WA), lambda b, j: (b, 0, C_VA // WA)),
                  pl.BlockSpec((None, HA, BLK, NA_KEYS), bias_idx),
                  pl.BlockSpec((2, 128), lambda b, j: (0, 0)),
                  pl.BlockSpec((None, BLK, WC), lambda b, j: (b, j, C_UC // WC)),
                  pl.BlockSpec((None, BLK, WC), lambda b, j: (b, j, C_VC // WC)),
                  pl.BlockSpec((1, WC), lambda b, j: (0, 0)),
                  pl.BlockSpec((1, WC), lambda b, j: (0, 0)),
                  pl.BlockSpec((GC, MIX_CHUNK, MIX_CHUNK), lambda b, j: (0, 0, 0)),
                  pl.BlockSpec((GC, MIX_CHUNK, 128), lambda b, j: (0, 0, 0))],
        out_specs=[pl.BlockSpec((None, BLK, WA), lambda b, j: (b, j, 0)),
                   pl.BlockSpec((None, BLK, WC), lambda b, j: (b, j, 0))],
        out_shape=[jax.ShapeDtypeStruct((bsz, rows, WA), BF16),
                   jax.ShapeDtypeStruct((bsz, rows, WC), BF16)],
        compiler_params=_cp(2), name="nbr_attn_gmlp",
    )(P, P, P, bias, hmask, P, P, ln_g.reshape(1, WC), ln_b.reshape(1, WC), w_sp, bs_rep)


def _na_bias(rpb_l):
    n_dr, n_dc = 2 * WIN_ROWS - 1, 2 * WIN_COLS - 1
    cq = np.arange(GRID_W)[:, None]
    ck = np.arange(GRID_W)[None, :]
    cs = np.clip(cq - WIN_COLS // 2, 0, GRID_W - WIN_COLS)
    col_ok = (ck >= cs) & (ck < cs + WIN_COLS)
    dc = np.clip(ck - cq + WIN_COLS - 1, 0, n_dc - 1)
    col_sel = (dc[:, :, None] == np.arange(n_dc)).astype(np.float32)
    colx = jnp.einsum("qke,hde->hdqk", col_sel, rpb_l, precision=lax.Precision.HIGHEST)
    colx = jnp.where(col_ok, colx, NEG).astype(F32)
    return pl.pallas_call(
        _bias_kernel,
        grid=(3, HA),
        in_specs=[pl.BlockSpec((None, n_dr, GRID_W, GRID_W), lambda t, h: (h, 0, 0, 0))],
        out_specs=pl.BlockSpec((None, None, BLK, NA_KEYS), lambda t, h: (t, h, 0, 0)),
        out_shape=jax.ShapeDtypeStruct((3, HA, BLK, NA_KEYS), F32),
        compiler_params=_cp(2), name="na_bias",
    )(colx)


def _bias_kernel(colx_ref, o_ref):
    t = pl.program_id(0)
    neg = jnp.full((GRID_W, GRID_W), NEG, F32)
    for tt, rb in enumerate((0, 1, NLAT - 1)):
        @pl.when(t == tt)
        def _(rb=rb):
            k0 = int(np.clip(rb * NA_QROWS - WIN_ROWS // 2, 0, ROWS - NA_KROWS))
            for qi in range(NA_QROWS):
                r = rb * NA_QROWS + qi
                rs = int(np.clip(r - WIN_ROWS // 2, 0, ROWS - WIN_ROWS))
                for kp in range(NA_KROWS // 2):
                    pair = []
                    for kr in (k0 + 2 * kp, k0 + 2 * kp + 1):
                        inside = rs <= kr < rs + WIN_ROWS
                        pair.append(colx_ref[kr - r + WIN_ROWS - 1] if inside else neg)
                    o_ref[qi * GRID_W:(qi + 1) * GRID_W, kp * 128:(kp + 1) * 128] = (
                        jnp.concatenate(pair, axis=1))


def _gla_matrices():
    n = BLK
    i = np.arange(n)[:, None]
    t = np.arange(n)[None, :]
    out = np.zeros((2, (len(GLA_FINE) + 1) * n, n), np.float32)
    for f, lvl in enumerate(GLA_FINE):
        s = n >> (lvl + 1)
        blk0 = (i // (2 * s)) * (2 * s)
        m = blk0 + s - 1
        fwd = np.where(i > m, (t > m) & (t <= i), (t > i) & (t <= m))
        bwd = np.where(i <= m, (t >= i) & (t <= m), (t > m) & (t < i))
        out[0, f * n:(f + 1) * n] = fwd
        out[1, f * n:(f + 1) * n] = bwd
    out[0, len(GLA_FINE) * n:] = t <= i
    out[1, len(GLA_FINE) * n:] = t >= i
    return out


def _gla_level_ids():
    h = BLK // 2
    i = np.arange(h)[:, None]
    j = np.arange(h)[None, :]
    x = np.maximum(i ^ j, 1)
    lvl = GLA_LEVELS - 1 - np.floor(np.log2(x)).astype(np.int32)
    fwd = np.where(i == j, -1, np.where(i > j, lvl, -2))
    bwd = np.where(i == j, -1, np.where(i < j, lvl, -2))
    return np.stack([fwd, bwd]).astype(np.int32)


def _gla_body(rev, kb_ref, qb_ref, vb_ref, dec_ref, wd_ref, bd_ref, rope_ref, m_ref, lev_ref, o_ref,
              st_ref):
    half = BLK // 2
    z = _dot(dec_ref[...], wd_ref[...].astype(BF16)) + bd_ref[...]
    la = (jnp.minimum(z, 0.0) - jnp.log1p(jnp.exp(-jnp.abs(z)))) * (LOG2E / GATE_TAU)
    la_b = la.astype(BF16)

    def seg_sum(f):
        return _dot(m_ref[f * BLK:(f + 1) * BLK, :], la_b)

    run = seg_sum(len(GLA_FINE))
    e_in = jnp.exp2(run)
    e_out = jnp.exp2((run[0:1] if rev else run[BLK - 1:BLK]) - run)
    e_fine = {lvl: jnp.exp2(seg_sum(f)) for f, lvl in enumerate(GLA_FINE)}

    def slab_decay(i, s, ks):
        base = (i // 2) * 2 * s
        ref = base + s if rev else base + s - 1
        d = run[i * s:(i + 1) * s, ks] - run[ref:ref + 1, ks]
        return jnp.exp2(-d if ((i % 2 == 0) != rev) else d)

    row = lax.broadcasted_iota(jnp.int32, (BLK, DK_B), 0)
    lev = lev_ref[...]
    on_diag = lev == -1
    at_level = {lvl: lev == lvl for lvl in range(1, GLA_LEVELS)}
    cosv, sav, sbv = rope_ref[0], rope_ref[1], rope_ref[2]

    def rope(x):
        return x * cosv + pltpu.roll(x, DK_B - 32, 1) * sav + pltpu.roll(x, 32, 1) * sbv

    qi, ki = (0, 1) if rev else (1, 0)
    q_rows = slice(qi * half, (qi + 1) * half)
    k_rows = slice(ki * half, (ki + 1) * half)

    outs = []
    for hh in range(HB):
        ks = slice(DK_B * hh, DK_B * (hh + 1))
        q = rope(qb_ref[:, ks].astype(F32)) * (DK_B ** -0.5)
        k = rope(kb_ref[:, ks].astype(F32))
        v = vb_ref[:, DV_B * hh:DV_B * (hh + 1)]
        self_w = jnp.sum(q * k, axis=-1, keepdims=True)
        cross = _dot_nt((q[q_rows] * slab_decay(qi, half, ks)).astype(BF16),
                        (k[k_rows] * slab_decay(ki, half, ks)).astype(BF16))
        diag = [jnp.where(on_diag, self_w[c * half:(c + 1) * half], 0.0) for c in range(2)]
        for lvl in range(1, GLA_LEVELS):
            s = BLK >> (lvl + 1)
            if lvl not in GLA_FINE:
                parts = []
                for i in range(BLK // s):
                    src = q if ((i % 2 == 1) != rev) else k
                    parts.append(src[i * s:(i + 1) * s] * slab_decay(i, s, ks))
                x = jnp.concatenate(parts, axis=0)
            else:
                second = ((row // s) & 1) == 1
                x = (jnp.where(second, k, q) if rev else jnp.where(second, q, k)) * e_fine[lvl][:, ks]
            xb = x.astype(BF16)
            for c in range(2):
                xc = xb[c * half:(c + 1) * half]
                diag[c] = jnp.where(at_level[lvl], _dot_nt(xc, xc), diag[c])
        zero = jnp.zeros((half, half), F32)
        if rev:
            att = jnp.concatenate([jnp.concatenate([diag[0], cross], axis=1),
                                   jnp.concatenate([zero, diag[1]], axis=1)], axis=0)
        else:
            att = jnp.concatenate([jnp.concatenate([diag[0], zero], axis=1),
                                   jnp.concatenate([cross, diag[1]], axis=1)], axis=0)
        e_q = e_in[:, ks]
        st = st_ref[hh]
        o = _dot_nt((q * e_q).astype(BF16), st.astype(BF16)) + _dot(att.astype(BF16), v)
        outs.append(o.astype(BF16))
        carry = e_q[0:1, :] if rev else e_q[BLK - 1:BLK, :]
        kt = (k * e_out[:, ks]).astype(BF16)
        st_ref[hh] = st * carry + _dot_tn(v, kt)
    o_ref[...] = jnp.concatenate(outs, axis=-1)


def _gla_kernel(*refs):
    st_ref = refs[-1]
    d = pl.program_id(1)

    @pl.when(pl.program_id(2) == 0)
    def _():
        st_ref[...] = jnp.zeros_like(st_ref)

    @pl.when(d == 0)
    def _():
        _gla_body(False, *refs)

    @pl.when(d == 1)
    def _():
        _gla_body(True, *refs)


def _gla_call(P, wd_pad, bd, rope, mstack, lev_ids):
    bsz = P.shape[0]

    def blk(dd, s):
        return jnp.where(s == 0, NLAT, jnp.where(dd == 0, s - 1, NLAT - s))

    return pl.pallas_call(
        _gla_kernel,
        grid=(bsz, 2, NBLK),
        in_specs=[
            pl.BlockSpec((None, BLK, KB), lambda b, dd, s: (b, blk(dd, s), C_KB // KB)),
            pl.BlockSpec((None, BLK, KB), lambda b, dd, s: (b, blk(dd, s), C_QB // KB)),
            pl.BlockSpec((None, BLK, VB), lambda b, dd, s: (b, blk(dd, s), C_VB // VB)),
            pl.BlockSpec((None, BLK, 128), lambda b, dd, s: (b, blk(dd, s), C_DEC // 128)),
            pl.BlockSpec((None, 128, KB), lambda b, dd, s: (dd, 0, 0)),
            pl.BlockSpec((None, 1, KB), lambda b, dd, s: (dd, 0, 0)),
            pl.BlockSpec((3, BLK, DK_B), lambda b, dd, s: (0, blk(dd, s), 0)),
            pl.BlockSpec((None, (len(GLA_FINE) + 1) * BLK, BLK), lambda b, dd, s: (dd, 0, 0)),
            pl.BlockSpec((None, BLK // 2, BLK // 2), lambda b, dd, s: (dd, 0, 0)),
        ],
        out_specs=pl.BlockSpec((None, None, BLK, VB), lambda b, dd, s: (dd, b, blk(dd, s), 0)),
        out_shape=jax.ShapeDtypeStruct((2, bsz, TT, VB), BF16),
        scratch_shapes=[pltpu.VMEM((HB, DV_B, DK_B), F32)],
        compiler_params=_cp(3), name="gla",
    )(P, P, P, P, wd_pad, bd, rope, mstack, lev_ids)


def _rope_tables():
    t = np.arange(T)
    half = DK_B // 2
    inv = np.float32(ROPE_BASE) ** (-np.arange(0, half, 2, dtype=np.float32) / np.float32(half))

    def tab(pos):
        ang = pos.astype(np.float32)[:, None] * inv[None, :]
        return np.concatenate([ang, ang], axis=-1)

    ang = np.concatenate([tab(t // GRID_W), tab(t % GRID_W)], axis=-1)
    cos, sin = np.cos(ang), np.sin(ang)
    first = (np.arange(DK_B) % half) < (half // 2)
    sa = np.where(first[None, :], -sin, 0.0)
    sb = np.where(first[None, :], 0.0, sin)
    ident = np.ones((L, DK_B), np.float32)
    zero = np.zeros((L, DK_B), np.float32)
    tabs = np.stack([np.concatenate([cos, ident]), np.concatenate([sa, zero]), np.concatenate([sb, zero])])
    return jnp.asarray(tabs, F32)


def _gelu(x):
    return 0.5 * x * (1.0 + jnp.tanh(0.7978845608028654 * (x + 0.044715 * (x * x * x))))


def _gmlp_body(u_ref, v_ref, g_ref, b_ref, w_ref, bs_ref, o_ref):
    for ch in range(BLK // MIX_CHUNK):
        rs = slice(ch * MIX_CHUNK, (ch + 1) * MIX_CHUNK)
        u = _gelu(u_ref[rs, :].astype(F32))
        v = _gelu(v_ref[rs, :].astype(F32))
        for g in range(GC):
            cs = slice(g * CG, (g + 1) * CG)
            vg = v[:, cs]
            mu = jnp.mean(vg, axis=-1, keepdims=True)
            var = jnp.mean(jnp.square(vg - mu), axis=-1, keepdims=True)
            vn = (vg - mu) * lax.rsqrt(var + EPS) * g_ref[:, cs] + b_ref[:, cs]
            s = _dot(w_ref[g].astype(BF16), vn.astype(BF16)) + bs_ref[g]
            o_ref[rs, cs] = (u[:, cs] * s).astype(BF16)


def _out_kernel(*refs, split):
    oa_ref, og_ref, gb_ref, oc_ref, w_ref = refs[:5]
    if split:
        x_in = _stream_block(refs[5], refs[6])
    else:
        x_in = refs[5][...]
    gg_ref, gpost_ref, gpre_ref, mod_ref, wr_ref, xo_ref, h2_ref, lg_ref = refs[6 + split:]
    o = og_ref[0].astype(F32) + og_ref[1].astype(F32)
    gb = gb_ref[...].astype(F32)
    parts = []
    for h in range(HB):
        cs = slice(h * DV_B, (h + 1) * DV_B)
        gh = gb[:, cs]
        parts.append((_rms(o[:, cs]) * gg_ref[:, cs] * (gh * _sigmoid(gh))).astype(BF16))
    mixed = jnp.concatenate([oa_ref[...]] + parts + [oc_ref[...]], axis=-1)
    y = _dot(mixed, w_ref[...])
    x1 = x_in + mod_ref[2:3, :] * (_rms(y) * gpost_ref[...])
    xo_ref[...] = x1
    h2 = ((_rms(x1) * gpre_ref[...]) * (1.0 + mod_ref[4:5, :]) + mod_ref[3:4, :]).astype(BF16)
    h2_ref[...] = h2
    wh, wl = _split_bf16(wr_ref[...])
    lg_ref[...] = _dot(h2, wh) + _dot(h2, wl)


def _out_call(o_a, o_g, P, o_c, w_out_b, stream, g_gla, g_post, g_pre, modtab, wr_pad, n_blocks):
    bsz = P.shape[0]
    rows = n_blocks * BLK
    split = isinstance(stream, tuple)
    rowspec = lambda w: pl.BlockSpec((None, BLK, w), lambda b, j: (b, j, 0))
    vec = lambda w: pl.BlockSpec((1, w), lambda b, j: (0, 0))
    return pl.pallas_call(
        functools.partial(_out_kernel, split=split),
        grid=(bsz, n_blocks),
        in_specs=[rowspec(WA),
                  pl.BlockSpec((2, None, BLK, VB), lambda b, j: (0, b, j, 0)),
                  pl.BlockSpec((None, BLK, VB), lambda b, j: (b, j, C_GB // VB)),
                  rowspec(WC),
                  pl.BlockSpec((D, D), lambda b, j: (0, 0))]
                 + (_stream_specs() if split else [rowspec(D)])
                 + [vec(VB), vec(D), vec(D),
                    pl.BlockSpec((None, None, 8, D), lambda b, j: (b, j // NLAT, 0, 0)),
                    pl.BlockSpec((D, 128), lambda b, j: (0, 0))],
        out_specs=[rowspec(D), rowspec(D), rowspec(128)],
        out_shape=[jax.ShapeDtypeStruct((bsz, rows, D), F32),
                   jax.ShapeDtypeStruct((bsz, rows, D), BF16),
                   jax.ShapeDtypeStruct((bsz, rows, 128), F32)],
        compiler_params=_cp(2), name="proj_out",
    )(o_a, o_g, P, o_c, w_out_b, *(stream if split else (stream,)), g_gla.reshape(1, VB),
      g_post.reshape(1, D), g_pre.reshape(1, D), modtab, wr_pad)


def _route_set(lg, cap, upper):
    n = lg.shape[0]
    lt = lg.T[:N_EXPERTS, :]
    ex = jnp.exp(lt - lt.max(axis=0, keepdims=True))
    aff = ex / ex.sum(axis=0, keepdims=True)
    bits = pltpu.bitcast(aff, jnp.int32)
    capf = jnp.float32(cap)

    def keep(cand, prefix):
        cnt = jnp.sum(jnp.where(bits >= cand, 1.0, 0.0), axis=1, keepdims=True)
        return jnp.where(cnt >= capf, cand, prefix)

    def two_bits(i, prefix):
        lo = jnp.left_shift(jnp.int32(1), 29 - 2 * i)
        best = keep(prefix | lo, prefix)
        best = keep(prefix | (lo + lo), best)
        return keep(prefix | (lo + lo + lo), best)

    thr = lax.fori_loop(0, 15, two_bits, jnp.zeros((N_EXPERTS, 1), jnp.int32))
    thr = keep(thr | 1, thr)
    gt = jnp.where(bits > thr, 1.0, 0.0)
    eq = jnp.where(bits == thr, 1.0, 0.0)
    need = capf - gt.sum(axis=1, keepdims=True)
    rank_eq = _dot(eq.astype(BF16), upper)
    sel = gt + eq * jnp.where(rank_eq < need, 1.0, 0.0)
    before = _dot(sel.astype(BF16), upper)
    slot = jnp.where(sel > 0.5, before, -1.0)
    pad = jnp.full((128 - SCATTER_EXPERTS, n), -1.0, F32)
    slot_t = [jnp.concatenate([slot[g * SCATTER_EXPERTS:(g + 1) * SCATTER_EXPERTS], pad], axis=0).T
              for g in range(N_EXPERTS // SCATTER_EXPERTS)]
    tile_start = jnp.concatenate([before[:, r * BLK:r * BLK + 1] for r in range(n // BLK)]
                                 + [jnp.full((N_EXPERTS, 128 - n // BLK), capf, F32)], axis=1)
    return slot, aff, slot_t, tile_start


def _route_kernel(lg_ref, u_ref, *out_refs, with_ctx):
    sl, af, st, ts = _route_set(lg_ref[0:T, :], CAP_LAT, u_ref[...])
    out_refs[0][...] = sl
    out_refs[1][...] = af
    for g, s in enumerate(st):
        out_refs[2][g, 0:T, :] = s
    out_refs[3][...] = ts
    if with_ctx:
        sl, af, st, _ = _route_set(lg_ref[T:TT, :], CAP_CTX, u_ref[0:L, 0:L])
        out_refs[4][...] = sl
        out_refs[5][...] = af
        for g, s in enumerate(st):
            out_refs[2][g, T:TT, :] = s


def _route_call(lg, upper, with_ctx):
    bsz, rows, _ = lg.shape
    ng = N_EXPERTS // SCATTER_EXPERTS
    en = lambda n: pl.BlockSpec((None, N_EXPERTS, n), lambda b: (b, 0, 0))
    out_specs = [en(T), en(T), pl.BlockSpec((None, ng, rows, 128), lambda b: (b, 0, 0, 0)), en(128)]
    out_shape = [jax.ShapeDtypeStruct((bsz, N_EXPERTS, T), F32),
                 jax.ShapeDtypeStruct((bsz, N_EXPERTS, T), F32),
                 jax.ShapeDtypeStruct((bsz, ng, rows, 128), F32),
                 jax.ShapeDtypeStruct((bsz, N_EXPERTS, 128), F32)]
    if with_ctx:
        out_specs += [en(L), en(L)]
        out_shape += [jax.ShapeDtypeStruct((bsz, N_EXPERTS, L), F32)] * 2
    return pl.pallas_call(
        functools.partial(_route_kernel, with_ctx=with_ctx),
        grid=(bsz,),
        in_specs=[pl.BlockSpec((None, rows, 128), lambda b: (b, 0, 0)),
                  pl.BlockSpec((T, T), lambda b: (0, 0))],
        out_specs=out_specs, out_shape=out_shape,
        compiler_params=_cp(1), name="route",
    )(lg, upper)


def _gather_kernel(*refs, n_sets):
    ins, outs = refs[:3 * n_sets], refs[3 * n_sets:]
    for k in range(n_sets):
        slot_ref, aff_ref, h_ref = ins[3 * k:3 * k + 3]
        xs_ref, g_ref = outs[2 * k:2 * k + 2]
        cap, n = xs_ref.shape[1], h_ref.shape[0]
        sid = lax.broadcasted_iota(jnp.int32, (cap, n), 0).astype(F32)
        picks = []
        for j in range(GATHER_EXPERTS):
            e = pl.program_id(1) * GATHER_EXPERTS + j
            hit = sid == slot_ref[pl.ds(e, 1), :]
            picks.append(jnp.where(hit, 1.0, 0.0).astype(BF16))
            g = jnp.sum(jnp.where(hit, aff_ref[pl.ds(e, 1), :], 0.0), axis=1, keepdims=True)
            g_ref[j] = jnp.broadcast_to(g, (cap, 128))
        xs = _dot(jnp.concatenate(picks, axis=0), h_ref[...]).astype(BF16)
        xs_ref[...] = xs.reshape(GATHER_EXPERTS, cap, D)


def _gather_call(h2, sets):
    bsz = h2.shape[0]
    in_specs, args, out_specs, out_shape = [], [], [], []
    for slot, aff, n, cap, blk_idx in sets:
        in_specs += [pl.BlockSpec((None, N_EXPERTS, n), lambda b, e: (b, 0, 0)),
                     pl.BlockSpec((None, N_EXPERTS, n), lambda b, e: (b, 0, 0)),
                     pl.BlockSpec((None, n, D), lambda b, e, i=blk_idx: (b, i, 0))]
        args += [slot, aff, h2]
        out_specs += [pl.BlockSpec((GATHER_EXPERTS, cap, D), lambda b, e: (e, b, 0)),
                      pl.BlockSpec((GATHER_EXPERTS, cap, 128), lambda b, e: (e, b, 0))]
        out_shape += [jax.ShapeDtypeStruct((N_EXPERTS, bsz * cap, D), BF16),
                      jax.ShapeDtypeStruct((N_EXPERTS, bsz * cap, 128), F32)]
    return pl.pallas_call(
        functools.partial(_gather_kernel, n_sets=len(sets)),
        grid=(bsz, N_EXPERTS // GATHER_EXPERTS),
        in_specs=in_specs, out_specs=out_specs, out_shape=out_shape,
        compiler_params=_cp(2), name="gather",
    )(*args)


def _window_starts(start_ref, b, first_expert, r, ne):
    first, fits = [], None
    for j in range(ne):
        lo = start_ref[b, first_expert + j, r]
        hi = start_ref[b, first_expert + j, r + 1]
        f = jnp.minimum((lo // 16) * 16, CAP_LAT - SCATTER_WIN)
        first.append(f)
        ok = hi <= f + SCATTER_WIN
        fits = ok if fits is None else jnp.logical_and(fits, ok)
    return first, fits


def _gather_tiles_kernel(start_ref, slot_ref, aff_ref, h_ref, xs_ref, g_ref):
    ne = SCATTER_EXPERTS
    b, eg, r = pl.program_id(0), pl.program_id(1), pl.program_id(2)

    @pl.when(r == 0)
    def _():
        xs_ref[...] = jnp.zeros_like(xs_ref)
        g_ref[...] = jnp.zeros_like(g_ref)

    first, fits = _window_starts(start_ref, b, eg * ne, r, ne)

    def put(rows, row0):
        sid = lax.broadcasted_iota(jnp.int32, (rows, BLK), 0).astype(F32)
        picks, where_to = [], []
        for j in range(ne):
            e = eg * ne + j
            hit = sid == slot_ref[pl.ds(e, 1), :] - row0[j].astype(F32)
            picks.append(jnp.where(hit, 1.0, 0.0).astype(BF16))
            g = jnp.sum(jnp.where(hit, aff_ref[pl.ds(e, 1), :], 0.0), axis=1, keepdims=True)
            ws = pl.ds(pl.multiple_of(row0[j], 16), rows)
            g_ref[j, ws, :] += jnp.broadcast_to(g, (rows, 128))
            where_to.append(ws)
        res = _dot(jnp.concatenate(picks, axis=0), h_ref[...])
        for j, ws in enumerate(where_to):
            xs_ref[j, ws, :] = (xs_ref[j, ws, :].astype(F32) + res[j * rows:(j + 1) * rows]).astype(BF16)

    @pl.when(fits)
    def _():
        put(SCATTER_WIN, first)

    @pl.when(jnp.logical_not(fits))
    def _():
        put(CAP_LAT, [jnp.int32(0)] * ne)


def _gather_tiles_call(tile_start, slot, aff, h2):
    bsz = h2.shape[0]
    ne = SCATTER_EXPERTS
    return pl.pallas_call(
        _gather_tiles_kernel,
        grid=(bsz, N_EXPERTS // ne, NLAT),
        in_specs=[pl.BlockSpec(memory_space=pltpu.SMEM),
                  pl.BlockSpec((None, N_EXPERTS, BLK), lambda b, g, r: (b, 0, r)),
                  pl.BlockSpec((None, N_EXPERTS, BLK), lambda b, g, r: (b, 0, r)),
                  pl.BlockSpec((None, BLK, D), lambda b, g, r: (b, r, 0))],
        out_specs=[pl.BlockSpec((ne, CAP_LAT, D), lambda b, g, r: (g, b, 0)),
                   pl.BlockSpec((ne, CAP_LAT, 128), lambda b, g, r: (g, b, 0))],
        out_shape=[jax.ShapeDtypeStruct((N_EXPERTS, bsz * CAP_LAT, D), BF16),
                   jax.ShapeDtypeStruct((N_EXPERTS, bsz * CAP_LAT, 128), F32)],
        compiler_params=_cp(3), name="gather_tiles",
    )(tile_start[:, :, :NLAT + 1].astype(jnp.int32), slot, aff, h2)


def _ffn_kernel(*refs, rl, rc, nf):
    if rc:
        xl_ref, xc_ref, gl_ref, gc_ref, wg_ref, wu_ref, wd_ref, yl_ref, yc_ref, hid = refs
        groups = ((xl_ref, gl_ref, yl_ref, 0, rl), (xc_ref, gc_ref, yc_ref, rl, rc))
    else:
        xl_ref, gl_ref, wg_ref, wu_ref, wd_ref, yl_ref, hid = refs
        groups = ((xl_ref, gl_ref, yl_ref, 0, rl),)
    s = pl.program_id(1)

    @pl.when(s < nf)
    def _():
        wg = wg_ref[...].astype(BF16)
        wu = wu_ref[...].astype(BF16)
        for x_ref, _, _, r0, nr in groups:
            xv = x_ref[...]
            a = _dot(xv, wg)
            u = _dot(xv, wu)
            hid[s, r0:r0 + nr, :] = ((a * _sigmoid(a)) * u).astype(BF16)

    @pl.when(s >= nf)
    def _():
        wd = wd_ref[...].astype(BF16)
        for _, g_ref, y_ref, r0, nr in groups:
            hm = jnp.concatenate([hid[k, r0:r0 + nr, :] for k in range(nf)], axis=1)
            y = _dot(hm, wd)
            for c in range(FF_TILE // 128):
                cs = slice(128 * c, 128 * (c + 1))
                y_ref[:, cs] = (y[:, cs] * g_ref[...]).astype(BF16)


def _ffn_call(layer, xs_l, g_l, xs_c, g_c, w_gate, w_up, w_down):
    rl = xs_l.shape[1]
    rc = 0 if xs_c is None else xs_c.shape[1]
    nf = EXPERT_FF // FF_TILE
    nd = D // FF_TILE
    up = lambda e, s: (layer, e, 0, jnp.minimum(s, nf - 1))
    down = lambda s: jnp.maximum(s - nf, 0)
    in_specs = [pl.BlockSpec((None, rl, D), lambda e, s: (e, 0, 0))]
    args = [xs_l]
    if rc:
        in_specs.append(pl.BlockSpec((None, rc, D), lambda e, s: (e, 0, 0)))
        args.append(xs_c)
    in_specs.append(pl.BlockSpec((None, rl, 128), lambda e, s: (e, 0, 0)))
    args.append(g_l)
    if rc:
        in_specs.append(pl.BlockSpec((None, rc, 128), lambda e, s: (e, 0, 0)))
        args.append(g_c)
    in_specs += [pl.BlockSpec((None, None, D, FF_TILE), up),
                 pl.BlockSpec((None, None, D, FF_TILE), up),
                 pl.BlockSpec((None, None, EXPERT_FF, FF_TILE), lambda e, s: (layer, e, 0, down(s)))]
    args += [w_gate, w_up, w_down]
    out_specs = [pl.BlockSpec((None, rl, FF_TILE), lambda e, s: (e, 0, down(s)))]
    out_shape = [jax.ShapeDtypeStruct((N_EXPERTS, rl, D), BF16)]
    if rc:
        out_specs.append(pl.BlockSpec((None, rc, FF_TILE), lambda e, s: (e, 0, down(s))))
        out_shape.append(jax.ShapeDtypeStruct((N_EXPERTS, rc, D), BF16))
    return pl.pallas_call(
        functools.partial(_ffn_kernel, rl=rl, rc=rc, nf=nf),
        grid=(N_EXPERTS, nf + nd),
        in_specs=in_specs, out_specs=out_specs, out_shape=out_shape,
        scratch_shapes=[pltpu.VMEM((nf, rl + rc, FF_TILE), BF16)],
        compiler_params=_cp(2), name="expert_ffn",
    )(*args)


def _scatter_kernel(*refs, ne, ng, with_ctx, has_next):
    refs = list(refs)
    start_ref, slot_ref, yl_ref = refs[:3]
    refs = refs[3:]
    yc_ref = refs.pop(0) if with_ctx else None
    x_ref, gpost_ref, mod_ref = refs[:3]
    refs = refs[3:]
    if has_next:
        gpre_ref, modn_ref = refs[:2]
        refs = refs[2:]
    xo_ref = refs[0]
    hn_ref = refs[1] if has_next else None
    acc = refs[-1]
    eg = pl.program_id(1)
    r = pl.program_id(2)

    def full(cap, y_ref):
        kk = ne * cap
        if cap % 128 == 0:
            want = lax.broadcasted_iota(jnp.int32, (BLK, cap), 1).astype(F32)
            pt = jnp.concatenate([jnp.where(slot_ref[:, j:j + 1] == want, 1.0, 0.0).astype(BF16)
                                  for j in range(ne)], axis=1)
        else:
            er = lax.broadcasted_iota(jnp.int32, (128, kk), 0)
            ec = lax.broadcasted_iota(jnp.int32, (128, kk), 1)
            expand = jnp.where(er == ec // cap, 1.0, 0.0).astype(BF16)
            spread = _dot(slot_ref[...].astype(BF16), expand)
            want = (lax.broadcasted_iota(jnp.int32, (BLK, kk), 1) % cap).astype(F32)
            pt = jnp.where(spread == want, 1.0, 0.0).astype(BF16)
        return _dot(pt, y_ref[...].reshape(kk, D))

    def windows(first):
        lane = lax.broadcasted_iota(jnp.int32, (BLK, 128), 1).astype(F32)
        low = lane < SCATTER_WIN
        tiles = []
        for j in range(0, ne, 2):
            rel_a = slot_ref[:, j:j + 1] - first[j].astype(F32)
            rel_b = slot_ref[:, j + 1:j + 2] - first[j + 1].astype(F32) + SCATTER_WIN
            tiles.append(jnp.where(jnp.where(low, rel_a, rel_b) == lane, 1.0, 0.0).astype(BF16))
        yw = jnp.concatenate([yl_ref[j, pl.ds(pl.multiple_of(first[j], 16), SCATTER_WIN), :]
                              for j in range(ne)], axis=0)
        return _dot(jnp.concatenate(tiles, axis=1), yw)

    def emit(moe):
        x2 = x_ref[...] + mod_ref[5:6, :] * (_rms(moe) * gpost_ref[...])
        xo_ref[...] = x2
        if has_next:
            hn = (_rms(x2) * gpre_ref[...]) * (1.0 + modn_ref[1:2, :]) + modn_ref[0:1, :]
            hn_ref[...] = hn.astype(BF16)

    def settle(share):
        if ng > 1:
            @pl.when(eg == 0)
            def _():
                acc[r] = share()
        if ng > 2:
            @pl.when(jnp.logical_and(eg > 0, eg < ng - 1))
            def _():
                acc[r] += share()

        @pl.when(eg == ng - 1)
        def _():
            emit(share() if ng == 1 else acc[r] + share())

    @pl.when(r < NLAT)
    def _():
        first, fits = _window_starts(start_ref, pl.program_id(0), eg * ne, r, ne)

        @pl.when(fits)
        def _():
            settle(lambda: windows(first))

        @pl.when(jnp.logical_not(fits))
        def _():
            settle(lambda: full(CAP_LAT, yl_ref))

    if with_ctx:
        @pl.when(r == NLAT)
        def _():
            settle(lambda: full(CAP_CTX, yc_ref))


def _scatter_call(tile_start, slot_t, y_l, y_c, x_mid, g_post, modtab, g_pre_next, modtab_next):
    bsz = x_mid.shape[0]
    with_ctx = y_c is not None
    has_next = g_pre_next is not None
    n_tiles = NBLK if with_ctx else NLAT
    ne = SCATTER_EXPERTS
    ng = N_EXPERTS // ne
    vec = pl.BlockSpec((1, D), lambda b, g, r: (0, 0))
    modspec = pl.BlockSpec((None, None, 8, D), lambda b, g, r: (b, r // NLAT, 0, 0))
    late = lambda b, g, r: (b, jnp.where(g == ng - 1, r, 0), 0)
    in_specs = [pl.BlockSpec(memory_space=pltpu.SMEM),
                pl.BlockSpec((None, None, BLK, 128), lambda b, g, r: (b, g, r, 0)),
                pl.BlockSpec((ne, CAP_LAT, D), lambda b, g, r: (g, b, 0))]
    args = [tile_start[:, :, :NLAT + 1].astype(jnp.int32), slot_t, y_l]
    if with_ctx:
        in_specs.append(pl.BlockSpec((ne, CAP_CTX, D), lambda b, g, r: (g, b, 0)))
        args.append(y_c)
    in_specs += [pl.BlockSpec((None, BLK, D), late), vec, modspec]
    args += [x_mid, g_post.reshape(1, D), modtab]
    if has_next:
        in_specs += [vec, modspec]
        args += [g_pre_next.reshape(1, D), modtab_next]
    out_specs = [pl.BlockSpec((None, BLK, D), late)]
    out_shape = [jax.ShapeDtypeStruct((bsz, n_tiles * BLK, D), F32)]
    if has_next:
        out_specs.append(pl.BlockSpec((None, BLK, D), late))
        out_shape.append(jax.ShapeDtypeStruct((bsz, n_tiles * BLK, D), BF16))
    return pl.pallas_call(
        functools.partial(_scatter_kernel, ne=ne, ng=ng, with_ctx=with_ctx, has_next=has_next),
        grid=(bsz, ng, n_tiles),
        in_specs=in_specs, out_specs=out_specs, out_shape=out_shape,
        scratch_shapes=[pltpu.VMEM((n_tiles, BLK, D), F32)],
        compiler_params=_cp(3), name="scatter",
    )(*args)


def _pack_kernel(w_ref, o_ref):
    src = np.cumsum((0, WA, WA, WA, KB, VB, 2 * DEC_RANK, KB, VB, WC, WC))
    order = ((4, C_VB), (7, C_GB), (0, C_QA), (1, C_KA), (2, C_VA), (3, C_KB), (6, C_QB), (8, C_UC),
             (9, C_VC), (5, C_DEC))
    for seg, dst in order:
        lo, hi = int(src[seg]), int(src[seg + 1])
        o_ref[:, dst:dst + hi - lo] = w_ref[:, lo:hi].astype(BF16)
    tail = C_DEC + 2 * DEC_RANK
    o_ref[:, tail:] = jnp.zeros((o_ref.shape[0], NP - tail), BF16)


def _pack_w_in(w_in, layer):
    rows = 256
    n_in = w_in.shape[-1]
    return pl.pallas_call(
        _pack_kernel,
        grid=(D // rows,),
        in_specs=[pl.BlockSpec((None, rows, n_in), lambda i: (layer, i, 0))],
        out_specs=pl.BlockSpec((rows, NP), lambda i: (i, 0)),
        out_shape=jax.ShapeDtypeStruct((D, NP), BF16),
        compiler_params=_cp(1), name="pack_w_in",
    )(w_in)


def kernel(x, c, ctx, c_ctx, w_ada, b_ada, g_pre_mix, g_post_mix, g_pre_ffn, g_post_ffn, w_in, w_dec,
           b_dec, rpb, g_gla, ln_v_g, ln_v_b, w_sp, b_sp, w_out, w_router, w_gate, w_up, w_down):
    bsz = x.shape[0]
    c_all = jnp.zeros((16, D), F32).at[:bsz].set(c).at[8].set(c_ctx)
    mods = _ada_call(c_all, w_ada, b_ada).reshape(DEPTH, 16, 6, D)
    lat = mods[:, :bsz]
    cx = jnp.broadcast_to(mods[:, 8:9], lat.shape)
    modtab = jnp.pad(jnp.stack([lat, cx], axis=2), ((0, 0), (0, 0), (0, 0), (0, 2), (0, 0)))

    rope = _rope_tables()
    mstack = jnp.asarray(_gla_matrices(), BF16)
    lev_ids = jnp.asarray(_gla_level_ids())
    hmask = jnp.asarray(np.stack([(np.arange(128) < DH_A), (np.arange(128) >= DH_A)]) * DH_A ** -0.5, F32)
    upper = jnp.asarray(np.triu(np.ones((T, T), np.float32), 1), BF16)

    stream = (x, ctx)
    h = _prenorm_call(x, ctx, g_pre_mix[0], modtab[0])
    for l in range(DEPTH):
        last = l == DEPTH - 1
        nb = NLAT if last else NBLK
        P = _proj_call(h.reshape(bsz * TT, D), _pack_w_in(w_in, l)).reshape(bsz, TT, NP)
        bs_rep = jnp.broadcast_to(b_sp[l][:, :, None], (GC, MIX_CHUNK, 128))
        o_a, o_c = _na_gmlp_call(P, _na_bias(rpb[l]), hmask, ln_v_g[l], ln_v_b[l], w_sp[l], bs_rep, nb)
        wd_pad = jnp.zeros((2, 128, KB), F32)
        wd_pad = wd_pad.at[0, :DEC_RANK].set(w_dec[l, 0]).at[1, DEC_RANK:2 * DEC_RANK].set(w_dec[l, 1])
        o_g = _gla_call(P, wd_pad, b_dec[l].reshape(2, 1, KB), rope, mstack, lev_ids)
        wr_pad = jnp.pad(w_router[l], ((0, 0), (0, 128 - N_EXPERTS)))
        x_mid, h2, lg = _out_call(o_a, o_g, P, o_c, w_out[l].astype(BF16), stream, g_gla[l], g_post_mix[l],
                                  g_pre_ffn[l], modtab[l], wr_pad, nb)
        routed = _route_call(lg, upper, not last)
        slot_l, aff_l, slot_t, tile_start = routed[:4]
        xs_l, gt_l = _gather_tiles_call(tile_start, slot_l, aff_l, h2)
        if last:
            (y_l,) = _ffn_call(l, xs_l, gt_l, None, None, w_gate, w_up, w_down)
            (stream,) = _scatter_call(tile_start, slot_t, y_l, None, x_mid, g_post_ffn[l], modtab[l], None,
                                      None)
        else:
            slot_c, aff_c = routed[4:]
            xs_c, gt_c = _gather_call(h2, [(slot_c, aff_c, L, CAP_CTX, NLAT)])
            y_l, y_c = _ffn_call(l, xs_l, gt_l, xs_c, gt_c, w_gate, w_up, w_down)
            stream, h = _scatter_call(tile_start, slot_t, y_l, y_c, x_mid, g_post_ffn[l], modtab[l],
                                      g_pre_mix[l + 1], modtab[l + 1])
    return stream
```

```python
import functools

import numpy as np
import jax
import jax.numpy as jnp
from jax import lax
from jax.experimental import pallas as pl
from jax.experimental.pallas import tpu as pltpu

F32 = jnp.float32
BF16 = jnp.bfloat16

D = 2048
T = 2048
L = 256
TT = T + L
BLK = 256
NBLK = TT // BLK
NLAT = T // BLK
DEPTH = 2
GRID_W = 64
ROWS = T // GRID_W
HA, DH_A = 8, 64
WIN_ROWS, WIN_COLS = 8, 16
HB, DK_B, DV_B = 4, 128, 256
DEC_RANK = 16
GATE_TAU = 16.0
GC, CG, MIX_CHUNK = 4, 128, 128
N_EXPERTS = 16
EXPERT_FF = 2048
CAP_FACTOR = 2
ROPE_BASE = 10000.0
EPS = 1e-6
LOG2E = 1.4426950408889634
WA, KB, VB, WC = HA * DH_A, HB * DK_B, HB * DV_B, GC * CG

NP = 3 * D
C_VB, C_GB, C_QA, C_KA, C_VA, C_KB, C_QB, C_UC, C_VC, C_DEC = (
    0, 1024, 2048, 2560, 3072, 3584, 4096, 4608, 5120, 5632)

NA_QROWS = 4
NA_KROWS = 12
NA_KEYS = NA_KROWS * GRID_W
NEG = -1e30

GLA_LEVELS = 8
GLA_FINE = (5, 6, 7)
FF_TILE = 256
SCATTER_EXPERTS = 8
SCATTER_WIN = 64
GATHER_EXPERTS = 2
CAP_LAT = CAP_FACTOR * T // N_EXPERTS
CAP_CTX = CAP_FACTOR * L // N_EXPERTS
VMEM_LIMIT = 56 * 1024 * 1024


def _cp(n_axes):
    return pltpu.CompilerParams(dimension_semantics=("arbitrary",) * n_axes,
                                vmem_limit_bytes=VMEM_LIMIT)


def _dot(a, b):
    return jnp.dot(a, b, preferred_element_type=F32)


def _dot_nt(a, b):
    return lax.dot_general(a, b, (((1,), (1,)), ((), ())), preferred_element_type=F32)


def _dot_tn(a, b):
    return lax.dot_general(a, b, (((0,), (0,)), ((), ())), preferred_element_type=F32)


def _rms(x):
    return x * lax.rsqrt(jnp.mean(x * x, axis=-1, keepdims=True) + EPS)


def _sigmoid(x):
    return 1.0 / (1.0 + jnp.exp(-x))


def _split_bf16(x):
    hi = x.astype(BF16)
    lo = (x - hi.astype(F32)).astype(BF16)
    return hi, lo


def _ada_kernel(c_ref, w_ref, b_ref, o_ref):
    cv = c_ref[...]
    hi, lo = _split_bf16(cv * _sigmoid(cv))
    r = _dot(jnp.concatenate([hi, lo], axis=0), w_ref[...].astype(BF16))
    o_ref[...] = r[:16] + r[16:] + b_ref[...]


def _ada_call(c_all, w_ada, b_ada):
    tn = 1024
    n6 = w_ada.shape[-1]
    return pl.pallas_call(
        _ada_kernel,
        grid=(DEPTH, n6 // tn),
        in_specs=[pl.BlockSpec((16, D), lambda l, n: (0, 0)),
                  pl.BlockSpec((None, D, tn), lambda l, n: (l, 0, n)),
                  pl.BlockSpec((None, 1, tn), lambda l, n: (l, 0, n))],
        out_specs=pl.BlockSpec((None, 16, tn), lambda l, n: (l, 0, n)),
        out_shape=jax.ShapeDtypeStruct((DEPTH, 16, n6), F32),
        compiler_params=_cp(2), name="ada",
    )(c_all, w_ada, b_ada.reshape(DEPTH, 1, n6))


def _stream_block(x_ref, c_ref):
    return jnp.where(pl.program_id(1) < NLAT, x_ref[...], c_ref[...])


def _stream_specs():
    return [pl.BlockSpec((None, BLK, D), lambda b, j: (b, jnp.minimum(j, NLAT - 1), 0)),
            pl.BlockSpec((None, BLK, D), lambda b, j: (b, 0, 0))]


def _prenorm_kernel(x_ref, c_ref, g_ref, mod_ref, h_ref):
    y = _rms(_stream_block(x_ref, c_ref)) * g_ref[...]
    h_ref[...] = (y * (1.0 + mod_ref[1:2, :]) + mod_ref[0:1, :]).astype(BF16)


def _prenorm_call(x, ctx, g, modtab):
    bsz = x.shape[0]
    return pl.pallas_call(
        _prenorm_kernel,
        grid=(bsz, NBLK),
        in_specs=_stream_specs() + [
            pl.BlockSpec((1, D), lambda b, j: (0, 0)),
            pl.BlockSpec((None, None, 8, D), lambda b, j: (b, j // NLAT, 0, 0))],
        out_specs=pl.BlockSpec((None, BLK, D), lambda b, j: (b, j, 0)),
        out_shape=jax.ShapeDtypeStruct((bsz, TT, D), BF16),
        compiler_params=_cp(2), name="prenorm",
    )(x, ctx, g.reshape(1, D), modtab)


def _proj_kernel(h_ref, w_ref, o_ref):
    o_ref[...] = _dot(h_ref[...], w_ref[...]).astype(BF16)


def _proj_call(h2d, w_pack):
    m = h2d.shape[0]
    tm = 1024 if m % 1024 == 0 else 768
    tn = NP // 3
    return pl.pallas_call(
        _proj_kernel,
        grid=(NP // tn, m // tm),
        in_specs=[pl.BlockSpec((tm, D), lambda n, i: (i, 0)),
                  pl.BlockSpec((D, tn), lambda n, i: (0, n))],
        out_specs=pl.BlockSpec((tm, tn), lambda n, i: (i, n)),
        out_shape=jax.ShapeDtypeStruct((m, NP), BF16),
        compiler_params=_cp(2), name="proj_in",
    )(h2d, w_pack)


def _softmax_pv(s_list, v_list):
    m = s_list[0].max(axis=-1, keepdims=True)
    for s in s_list[1:]:
        m = jnp.maximum(m, s.max(axis=-1, keepdims=True))
    acc = None
    for s, v in zip(s_list, v_list):
        o = _dot(jnp.exp(s - m).astype(BF16), v)
        acc = o if acc is None else acc + o
    return acc[:, :128] / acc[:, 128:]


def _na_kernel(q_ref, k_ref, v_ref, bias_ref, hm_ref, *mlp_refs_o_ref):
    mlp_in, o_ref, oc_ref = mlp_refs_o_ref[:-2], mlp_refs_o_ref[-2], mlp_refs_o_ref[-1]
    j = pl.program_id(1)
    lane = lax.broadcasted_iota(jnp.int32, (BLK, 128), 1)
    low = lane < DH_A

    def run(local_start):
        _gmlp_body(*mlp_in, oc_ref)
        for p in range(HA // 2):
            sl = slice(128 * p, 128 * p + 128)
            q2 = q_ref[:, sl]
            kc = k_ref[T:TT, sl]
            vc = jnp.concatenate([v_ref[T:TT, sl], jnp.ones((L, 128), BF16)], axis=1)
            if local_start is not None:
                kl = k_ref[pl.ds(local_start, NA_KEYS), sl]
                vl = jnp.concatenate([v_ref[pl.ds(local_start, NA_KEYS), sl],
                                      jnp.ones((NA_KEYS, 128), BF16)], axis=1)
            pair = []
            for hh in range(2):
                qm = (q2.astype(F32) * hm_ref[hh:hh + 1, :]).astype(BF16)
                s_ctx = _dot_nt(qm, kc)
                if local_start is not None:
                    s_loc = _dot_nt(qm, kl) + bias_ref[2 * p + hh]
                    pair.append(_softmax_pv([s_loc, s_ctx], [vl, vc]))
                else:
                    pair.append(_softmax_pv([s_ctx], [vc]))
            o_ref[:, sl] = jnp.where(low, pair[0], pair[1]).astype(BF16)

    @pl.when(j < NLAT)
    def _():
        krow = jnp.clip(j * NA_QROWS - WIN_ROWS // 2, 0, ROWS - NA_KROWS)
        run(pl.multiple_of(krow * GRID_W, GRID_W))

    @pl.when(j == NLAT)
    def _():
        run(None)


def _na_gmlp_call(P, bias, hmask, ln_g, ln_b, w_sp, bs_rep, n_blocks):
    bsz = P.shape[0]

    def bias_idx(b, j):
        return (jnp.where(j == 0, 0, jnp.where(j == NLAT - 1, 2, 1)), 0, 0, 0)

    rows = n_blocks * BLK
    return pl.pallas_call(
        _na_kernel,
        grid=(bsz, n_blocks),
        in_specs=[pl.BlockSpec((None, BLK, WA), lambda b, j: (b, j, C_QA // WA)),
                  pl.BlockSpec((None, TT, WA), lambda b, j: (b, 0, C_KA // WA)),
                  pl.BlockSpec((None, TT, WA), lambda b, j: (b, 0, C_VA // WA)),
                  pl.BlockSpec((None, HA, BLK, NA_KEYS), bias_idx),
                  pl.BlockSpec((2, 128), lambda b, j: (0, 0)),
                  pl.BlockSpec((None, BLK, WC), lambda b, j: (b, j, C_UC // WC)),
                  pl.BlockSpec((None, BLK, WC), lambda b, j: (b, j, C_VC // WC)),
                  pl.BlockSpec((1, WC), lambda b, j: (0, 0)),
                  pl.BlockSpec((1, WC), lambda b, j: (0, 0)),
                  pl.BlockSpec((GC, MIX_CHUNK, MIX_CHUNK), lambda b, j: (0, 0, 0)),
                  pl.BlockSpec((GC, MIX_CHUNK, 128), lambda b, j: (0, 0, 0))],
        out_specs=[pl.BlockSpec((None, BLK, WA), lambda b, j: (b, j, 0)),
                   pl.BlockSpec((None, BLK, WC), lambda b, j: (b, j, 0))],
        out_shape=[jax.ShapeDtypeStruct((bsz, rows, WA), BF16),
                   jax.ShapeDtypeStruct((bsz, rows, WC), BF16)],
        compiler_params=_cp(2), name="nbr_attn_gmlp",
    )(P, P, P, bias, hmask, P, P, ln_g.reshape(1, WC), ln_b.reshape(1, WC), w_sp, bs_rep)


def _na_bias(rpb_l):
    n_dr, n_dc = 2 * WIN_ROWS - 1, 2 * WIN_COLS - 1
    cq = np.arange(GRID_W)[:, None]
    ck = np.arange(GRID_W)[None, :]
    cs = np.clip(cq - WIN_COLS // 2, 0, GRID_W - WIN_COLS)
    col_ok = (ck >= cs) & (ck < cs + WIN_COLS)
    dc = np.clip(ck - cq + WIN_COLS - 1, 0, n_dc - 1)
    col_sel = (dc[:, :, None] == np.arange(n_dc)).astype(np.float32)
    colx = jnp.einsum("qke,hde->hdqk", col_sel, rpb_l, precision=lax.Precision.HIGHEST)
    colx = jnp.where(col_ok, colx, NEG).astype(F32)
    return pl.pallas_call(
        _bias_kernel,
        grid=(3, HA),
        in_specs=[pl.BlockSpec((None, n_dr, GRID_W, GRID_W), lambda t, h: (h, 0, 0, 0))],
        out_specs=pl.BlockSpec((None, None, BLK, NA_KEYS), lambda t, h: (t, h, 0, 0)),
        out_shape=jax.ShapeDtypeStruct((3, HA, BLK, NA_KEYS), F32),
        compiler_params=_cp(2), name="na_bias",
    )(colx)


def _bias_kernel(colx_ref, o_ref):
    t = pl.program_id(0)
    neg = jnp.full((GRID_W, GRID_W), NEG, F32)
    for tt, rb in enumerate((0, 1, NLAT - 1)):
        @pl.when(t == tt)
        def _(rb=rb):
            k0 = int(np.clip(rb * NA_QROWS - WIN_ROWS // 2, 0, ROWS - NA_KROWS))
            for qi in range(NA_QROWS):
                r = rb * NA_QROWS + qi
                rs = int(np.clip(r - WIN_ROWS // 2, 0, ROWS - WIN_ROWS))
                for kp in range(NA_KROWS // 2):
                    pair = []
                    for kr in (k0 + 2 * kp, k0 + 2 * kp + 1):
                        inside = rs <= kr < rs + WIN_ROWS
                        pair.append(colx_ref[kr - r + WIN_ROWS - 1] if inside else neg)
                    o_ref[qi * GRID_W:(qi + 1) * GRID_W, kp * 128:(kp + 1) * 128] = (
                        jnp.concatenate(pair, axis=1))


def _gla_matrices():
    n = BLK
    i = np.arange(n)[:, None]
    t = np.arange(n)[None, :]
    out = np.zeros((2, (len(GLA_FINE) + 1) * n, n), np.float32)
    for f, lvl in enumerate(GLA_FINE):
        s = n >> (lvl + 1)
        blk0 = (i // (2 * s)) * (2 * s)
        m = blk0 + s - 1
        fwd = np.where(i > m, (t > m) & (t <= i), (t > i) & (t <= m))
        bwd = np.where(i <= m, (t >= i) & (t <= m), (t > m) & (t < i))
        out[0, f * n:(f + 1) * n] = fwd
        out[1, f * n:(f + 1) * n] = bwd
    out[0, len(GLA_FINE) * n:] = t <= i
    out[1, len(GLA_FINE) * n:] = t >= i
    return out


def _gla_level_ids():
    h = BLK // 2
    i = np.arange(h)[:, None]
    j = np.arange(h)[None, :]
    x = np.maximum(i ^ j, 1)
    lvl = GLA_LEVELS - 1 - np.floor(np.log2(x)).astype(np.int32)
    fwd = np.where(i == j, -1, np.where(i > j, lvl, -2))
    bwd = np.where(i == j, -1, np.where(i < j, lvl, -2))
    return np.stack([fwd, bwd]).astype(np.int32)


def _gla_body(rev, kb_ref, qb_ref, vb_ref, dec_ref, wd_ref, bd_ref, rope_ref, m_ref, lev_ref, o_ref,
              st_ref):
    half = BLK // 2
    z = _dot(dec_ref[...], wd_ref[...].astype(BF16)) + bd_ref[...]
    la = (jnp.minimum(z, 0.0) - jnp.log1p(jnp.exp(-jnp.abs(z)))) * (LOG2E / GATE_TAU)
    la_b = la.astype(BF16)

    def seg_sum(f):
        return _dot(m_ref[f * BLK:(f + 1) * BLK, :], la_b)

    run = seg_sum(len(GLA_FINE))
    e_in = jnp.exp2(run)
    e_out = jnp.exp2((run[0:1] if rev else run[BLK - 1:BLK]) - run)
    e_fine = {lvl: jnp.exp2(seg_sum(f)) for f, lvl in enumerate(GLA_FINE)}

    def slab_decay(i, s, ks):
        base = (i // 2) * 2 * s
        ref = base + s if rev else base + s - 1
        d = run[i * s:(i + 1) * s, ks] - run[ref:ref + 1, ks]
        return jnp.exp2(-d if ((i % 2 == 0) != rev) else d)

    row = lax.broadcasted_iota(jnp.int32, (BLK, DK_B), 0)
    lev = lev_ref[...]
    on_diag = lev == -1
    at_level = {lvl: lev == lvl for lvl in range(1, GLA_LEVELS)}
    cosv, sav, sbv = rope_ref[0], rope_ref[1], rope_ref[2]

    def rope(x):
        return x * cosv + pltpu.roll(x, DK_B - 32, 1) * sav + pltpu.roll(x, 32, 1) * sbv

    qi, ki = (0, 1) if rev else (1, 0)
    q_rows = slice(qi * half, (qi + 1) * half)
    k_rows = slice(ki * half, (ki + 1) * half)

    outs = []
    for hh in range(HB):
        ks = slice(DK_B * hh, DK_B * (hh + 1))
        q = rope(qb_ref[:, ks].astype(F32)) * (DK_B ** -0.5)
        k = rope(kb_ref[:, ks].astype(F32))
        v = vb_ref[:, DV_B * hh:DV_B * (hh + 1)]
        self_w = jnp.sum(q * k, axis=-1, keepdims=True)
        cross = _dot_nt((q[q_rows] * slab_decay(qi, half, ks)).astype(BF16),
                        (k[k_rows] * slab_decay(ki, half, ks)).astype(BF16))
        diag = [jnp.where(on_diag, self_w[c * half:(c + 1) * half], 0.0) for c in range(2)]
        for lvl in range(1, GLA_LEVELS):
            s = BLK >> (lvl + 1)
            if lvl not in GLA_FINE:
                parts = []
                for i in range(BLK // s):
                    src = q if ((i % 2 == 1) != rev) else k
                    parts.append(src[i * s:(i + 1) * s] * slab_decay(i, s, ks))
                x = jnp.concatenate(parts, axis=0)
            else:
                second = ((row // s) & 1) == 1
                x = (jnp.where(second, k, q) if rev else jnp.where(second, q, k)) * e_fine[lvl][:, ks]
            xb = x.astype(BF16)
            for c in range(2):
                xc = xb[c * half:(c + 1) * half]
                diag[c] = jnp.where(at_level[lvl], _dot_nt(xc, xc), diag[c])
        zero = jnp.zeros((half, half), F32)
        if rev:
            att = jnp.concatenate([jnp.concatenate([diag[0], cross], axis=1),
                                   jnp.concatenate([zero, diag[1]], axis=1)], axis=0)
        else:
            att = jnp.concatenate([jnp.concatenate([diag[0], zero], axis=1),
                                   jnp.concatenate([cross, diag[1]], axis=1)], axis=0)
        e_q = e_in[:, ks]
        st = st_ref[hh]
        o = _dot_nt((q * e_q).astype(BF16), st.astype(BF16)) + _dot(att.astype(BF16), v)
        outs.append(o.astype(BF16))
        carry = e_q[0:1, :] if rev else e_q[BLK - 1:BLK, :]
        kt = (k * e_out[:, ks]).astype(BF16)
        st_ref[hh] = st * carry + _dot_tn(v, kt)
    o_ref[...] = jnp.concatenate(outs, axis=-1)


def _gla_kernel(*refs):
    st_ref = refs[-1]
    d = pl.program_id(1)

    @pl.when(pl.program_id(2) == 0)
    def _():
        st_ref[...] = jnp.zeros_like(st_ref)

    @pl.when(d == 0)
    def _():
        _gla_body(False, *refs)

    @pl.when(d == 1)
    def _():
        _gla_body(True, *refs)


def _gla_call(P, wd_pad, bd, rope, mstack, lev_ids):
    bsz = P.shape[0]

    def blk(dd, s):
        return jnp.where(s == 0, NLAT, jnp.where(dd == 0, s - 1, NLAT - s))

    return pl.pallas_call(
        _gla_kernel,
        grid=(bsz, 2, NBLK),
        in_specs=[
            pl.BlockSpec((None, BLK, KB), lambda b, dd, s: (b, blk(dd, s), C_KB // KB)),
            pl.BlockSpec((None, BLK, KB), lambda b, dd, s: (b, blk(dd, s), C_QB // KB)),
            pl.BlockSpec((None, BLK, VB), lambda b, dd, s: (b, blk(dd, s), C_VB // VB)),
            pl.BlockSpec((None, BLK, 128), lambda b, dd, s: (b, blk(dd, s), C_DEC // 128)),
            pl.BlockSpec((None, 128, KB), lambda b, dd, s: (dd, 0, 0)),
            pl.BlockSpec((None, 1, KB), lambda b, dd, s: (dd, 0, 0)),
            pl.BlockSpec((3, BLK, DK_B), lambda b, dd, s: (0, blk(dd, s), 0)),
            pl.BlockSpec((None, (len(GLA_FINE) + 1) * BLK, BLK), lambda b, dd, s: (dd, 0, 0)),
            pl.BlockSpec((None, BLK // 2, BLK // 2), lambda b, dd, s: (dd, 0, 0)),
        ],
        out_specs=pl.BlockSpec((None, None, BLK, VB), lambda b, dd, s: (dd, b, blk(dd, s), 0)),
        out_shape=jax.ShapeDtypeStruct((2, bsz, TT, VB), BF16),
        scratch_shapes=[pltpu.VMEM((HB, DV_B, DK_B), F32)],
        compiler_params=_cp(3), name="gla",
    )(P, P, P, P, wd_pad, bd, rope, mstack, lev_ids)


def _rope_tables():
    t = np.arange(T)
    half = DK_B // 2
    inv = np.float32(ROPE_BASE) ** (-np.arange(0, half, 2, dtype=np.float32) / np.float32(half))

    def tab(pos):
        ang = pos.astype(np.float32)[:, None] * inv[None, :]
        return np.concatenate([ang, ang], axis=-1)

    ang = np.concatenate([tab(t // GRID_W), tab(t % GRID_W)], axis=-1)
    cos, sin = np.cos(ang), np.sin(ang)
    first = (np.arange(DK_B) % half) < (half // 2)
    sa = np.where(first[None, :], -sin, 0.0)
    sb = np.where(first[None, :], 0.0, sin)
    ident = np.ones((L, DK_B), np.float32)
    zero = np.zeros((L, DK_B), np.float32)
    tabs = np.stack([np.concatenate([cos, ident]), np.concatenate([sa, zero]), np.concatenate([sb, zero])])
    return jnp.asarray(tabs, F32)


def _gelu(x):
    return 0.5 * x * (1.0 + jnp.tanh(0.7978845608028654 * (x + 0.044715 * (x * x * x))))


def _gmlp_body(u_ref, v_ref, g_ref, b_ref, w_ref, bs_ref, o_ref):
    for ch in range(BLK // MIX_CHUNK):
        rs = slice(ch * MIX_CHUNK, (ch + 1) * MIX_CHUNK)
        u = _gelu(u_ref[rs, :].astype(F32))
        v = _gelu(v_ref[rs, :].astype(F32))
        for g in range(GC):
            cs = slice(g * CG, (g + 1) * CG)
            vg = v[:, cs]
            mu = jnp.mean(vg, axis=-1, keepdims=True)
            var = jnp.mean(jnp.square(vg - mu), axis=-1, keepdims=True)
            vn = (vg - mu) * lax.rsqrt(var + EPS) * g_ref[:, cs] + b_ref[:, cs]
            s = _dot(w_ref[g].astype(BF16), vn.astype(BF16)) + bs_ref[g]
            o_ref[rs, cs] = (u[:, cs] * s).astype(BF16)


def _out_kernel(*refs, split):
    oa_ref, og_ref, gb_ref, oc_ref, w_ref = refs[:5]
    if split:
        x_in = _stream_block(refs[5], refs[6])
    else:
        x_in = refs[5][...]
    gg_ref, gpost_ref, gpre_ref, mod_ref, wr_ref, xo_ref, h2_ref, lg_ref = refs[6 + split:]
    o = og_ref[0].astype(F32) + og_ref[1].astype(F32)
    gb = gb_ref[...].astype(F32)
    parts = []
    for h in range(HB):
        cs = slice(h * DV_B, (h + 1) * DV_B)
        gh = gb[:, cs]
        parts.append((_rms(o[:, cs]) * gg_ref[:, cs] * (gh * _sigmoid(gh))).astype(BF16))
    mixed = jnp.concatenate([oa_ref[...]] + parts + [oc_ref[...]], axis=-1)
    y = _dot(mixed, w_ref[...])
    x1 = x_in + mod_ref[2:3, :] * (_rms(y) * gpost_ref[...])
    xo_ref[...] = x1
    h2 = ((_rms(x1) * gpre_ref[...]) * (1.0 + mod_ref[4:5, :]) + mod_ref[3:4, :]).astype(BF16)
    h2_ref[...] = h2
    wh, wl = _split_bf16(wr_ref[...])
    lg_ref[...] = _dot(h2, wh) + _dot(h2, wl)


def _out_call(o_a, o_g, P, o_c, w_out_b, stream, g_gla, g_post, g_pre, modtab, wr_pad, n_blocks):
    bsz = P.shape[0]
    rows = n_blocks * BLK
    split = isinstance(stream, tuple)
    rowspec = lambda w: pl.BlockSpec((None, BLK, w), lambda b, j: (b, j, 0))
    vec = lambda w: pl.BlockSpec((1, w), lambda b, j: (0, 0))
    return pl.pallas_call(
        functools.partial(_out_kernel, split=split),
        grid=(bsz, n_blocks),
        in_specs=[rowspec(WA),
                  pl.BlockSpec((2, None, BLK, VB), lambda b, j: (0, b, j, 0)),
                  pl.BlockSpec((None, BLK, VB), lambda b, j: (b, j, C_GB // VB)),
                  rowspec(WC),
                  pl.BlockSpec((D, D), lambda b, j: (0, 0))]
                 + (_stream_specs() if split else [rowspec(D)])
                 + [vec(VB), vec(D), vec(D),
                    pl.BlockSpec((None, None, 8, D), lambda b, j: (b, j // NLAT, 0, 0)),
                    pl.BlockSpec((D, 128), lambda b, j: (0, 0))],
        out_specs=[rowspec(D), rowspec(D), rowspec(128)],
        out_shape=[jax.ShapeDtypeStruct((bsz, rows, D), F32),
                   jax.ShapeDtypeStruct((bsz, rows, D), BF16),
                   jax.ShapeDtypeStruct((bsz, rows, 128), F32)],
        compiler_params=_cp(2), name="proj_out",
    )(o_a, o_g, P, o_c, w_out_b, *(stream if split else (stream,)), g_gla.reshape(1, VB),
      g_post.reshape(1, D), g_pre.reshape(1, D), modtab, wr_pad)


def _route_set(lg, cap, upper):
    n = lg.shape[0]
    lt = lg.T[:N_EXPERTS, :]
    ex = jnp.exp(lt - lt.max(axis=0, keepdims=True))
    aff = ex / ex.sum(axis=0, keepdims=True)
    bits = pltpu.bitcast(aff, jnp.int32)
    capf = jnp.float32(cap)

    def keep(cand, prefix):
        cnt = jnp.sum(jnp.where(bits >= cand, 1.0, 0.0), axis=1, keepdims=True)
        return jnp.where(cnt >= capf, cand, prefix)

    def two_bits(i, prefix):
        lo = jnp.left_shift(jnp.int32(1), 29 - 2 * i)
        best = keep(prefix | lo, prefix)
        best = keep(prefix | (lo + lo), best)
        return keep(prefix | (lo + lo + lo), best)

    thr = lax.fori_loop(0, 15, two_bits, jnp.zeros((N_EXPERTS, 1), jnp.int32))
    thr = keep(thr | 1, thr)
    gt = jnp.where(bits > thr, 1.0, 0.0)
    eq = jnp.where(bits == thr, 1.0, 0.0)
    need = capf - gt.sum(axis=1, keepdims=True)
    rank_eq = _dot(eq.astype(BF16), upper)
    sel = gt + eq * jnp.where(rank_eq < need, 1.0, 0.0)
    before = _dot(sel.astype(BF16), upper)
    slot = jnp.where(sel > 0.5, before, -1.0)
    pad = jnp.full((128 - SCATTER_EXPERTS, n), -1.0, F32)
    slot_t = [jnp.concatenate([slot[g * SCATTER_EXPERTS:(g + 1) * SCATTER_EXPERTS], pad], axis=0).T
              for g in range(N_EXPERTS // SCATTER_EXPERTS)]
    tile_start = jnp.concatenate([before[:, r * BLK:r * BLK + 1] for r in range(n // BLK)]
                                 + [jnp.full((N_EXPERTS, 128 - n // BLK), capf, F32)], axis=1)
    return slot, aff, slot_t, tile_start


def _route_kernel(lg_ref, u_ref, *out_refs, with_ctx):
    sl, af, st, ts = _route_set(lg_ref[0:T, :], CAP_LAT, u_ref[...])
    out_refs[0][...] = sl
    out_refs[1][...] = af
    for g, s in enumerate(st):
        out_refs[2][g, 0:T, :] = s
    out_refs[3][...] = ts
    if with_ctx:
        sl, af, st, _ = _route_set(lg_ref[T:TT, :], CAP_CTX, u_ref[0:L, 0:L])
        out_refs[4][...] = sl
        out_refs[5][...] = af
        for g, s in enumerate(st):
            out_refs[2][g, T:TT, :] = s


def _route_call(lg, upper, with_ctx):
    bsz, rows, _ = lg.shape
    ng = N_EXPERTS // SCATTER_EXPERTS
    en = lambda n: pl.BlockSpec((None, N_EXPERTS, n), lambda b: (b, 0, 0))
    out_specs = [en(T), en(T), pl.BlockSpec((None, ng, rows, 128), lambda b: (b, 0, 0, 0)), en(128)]
    out_shape = [jax.ShapeDtypeStruct((bsz, N_EXPERTS, T), F32),
                 jax.ShapeDtypeStruct((bsz, N_EXPERTS, T), F32),
                 jax.ShapeDtypeStruct((bsz, ng, rows, 128), F32),
                 jax.ShapeDtypeStruct((bsz, N_EXPERTS, 128), F32)]
    if with_ctx:
        out_specs += [en(L), en(L)]
        out_shape += [jax.ShapeDtypeStruct((bsz, N_EXPERTS, L), F32)] * 2
    return pl.pallas_call(
        functools.partial(_route_kernel, with_ctx=with_ctx),
        grid=(bsz,),
        in_specs=[pl.BlockSpec((None, rows, 128), lambda b: (b, 0, 0)),
                  pl.BlockSpec((T, T), lambda b: (0, 0))],
        out_specs=out_specs, out_shape=out_shape,
        compiler_params=_cp(1), name="route",
    )(lg, upper)


def _gather_kernel(*refs, n_sets):
    ins, outs = refs[:3 * n_sets], refs[3 * n_sets:]
    for k in range(n_sets):
        slot_ref, aff_ref, h_ref = ins[3 * k:3 * k + 3]
        xs_ref, g_ref = outs[2 * k:2 * k + 2]
        cap, n = xs_ref.shape[1], h_ref.shape[0]
        sid = lax.broadcasted_iota(jnp.int32, (cap, n), 0).astype(F32)
        picks = []
        for j in range(GATHER_EXPERTS):
            e = pl.program_id(1) * GATHER_EXPERTS + j
            hit = sid == slot_ref[pl.ds(e, 1), :]
            picks.append(jnp.where(hit, 1.0, 0.0).astype(BF16))
            g = jnp.sum(jnp.where(hit, aff_ref[pl.ds(e, 1), :], 0.0), axis=1, keepdims=True)
            g_ref[j] = jnp.broadcast_to(g, (cap, 128))
        xs = _dot(jnp.concatenate(picks, axis=0), h_ref[...]).astype(BF16)
        xs_ref[...] = xs.reshape(GATHER_EXPERTS, cap, D)


def _gather_call(h2, sets):
    bsz = h2.shape[0]
    in_specs, args, out_specs, out_shape = [], [], [], []
    for slot, aff, n, cap, blk_idx in sets:
        in_specs += [pl.BlockSpec((None, N_EXPERTS, n), lambda b, e: (b, 0, 0)),
                     pl.BlockSpec((None, N_EXPERTS, n), lambda b, e: (b, 0, 0)),
                     pl.BlockSpec((None, n, D), lambda b, e, i=blk_idx: (b, i, 0))]
        args += [slot, aff, h2]
        out_specs += [pl.BlockSpec((GATHER_EXPERTS, cap, D), lambda b, e: (e, b, 0)),
                      pl.BlockSpec((GATHER_EXPERTS, cap, 128), lambda b, e: (e, b, 0))]
        out_shape += [jax.ShapeDtypeStruct((N_EXPERTS, bsz * cap, D), BF16),
                      jax.ShapeDtypeStruct((N_EXPERTS, bsz * cap, 128), F32)]
    return pl.pallas_call(
        functools.partial(_gather_kernel, n_sets=len(sets)),
        grid=(bsz, N_EXPERTS // GATHER_EXPERTS),
        in_specs=in_specs, out_specs=out_specs, out_shape=out_shape,
        compiler_params=_cp(2), name="gather",
    )(*args)


def _window_starts(start_ref, b, first_expert, r, ne):
    first, fits = [], None
    for j in range(ne):
        lo = start_ref[b, first_expert + j, r]
        hi = start_ref[b, first_expert + j, r + 1]
        f = jnp.minimum((lo // 16) * 16, CAP_LAT - SCATTER_WIN)
        first.append(f)
        ok = hi <= f + SCATTER_WIN
        fits = ok if fits is None else jnp.logical_and(fits, ok)
    return first, fits


def _gather_tiles_kernel(start_ref, slot_ref, aff_ref, h_ref, xs_ref, g_ref):
    ne = SCATTER_EXPERTS
    b, eg, r = pl.program_id(0), pl.program_id(1), pl.program_id(2)

    @pl.when(r == 0)
    def _():
        xs_ref[...] = jnp.zeros_like(xs_ref)
        g_ref[...] = jnp.zeros_like(g_ref)

    first, fits = _window_starts(start_ref, b, eg * ne, r, ne)

    def put(rows, row0):
        sid = lax.broadcasted_iota(jnp.int32, (rows, BLK), 0).astype(F32)
        picks, where_to = [], []
        for j in range(ne):
            e = eg * ne + j
            hit = sid == slot_ref[pl.ds(e, 1), :] - row0[j].astype(F32)
            picks.append(jnp.where(hit, 1.0, 0.0).astype(BF16))
            g = jnp.sum(jnp.where(hit, aff_ref[pl.ds(e, 1), :], 0.0), axis=1, keepdims=True)
            ws = pl.ds(pl.multiple_of(row0[j], 16), rows)
            g_ref[j, ws, :] += jnp.broadcast_to(g, (rows, 128))
            where_to.append(ws)
        res = _dot(jnp.concatenate(picks, axis=0), h_ref[...])
        for j, ws in enumerate(where_to):
            xs_ref[j, ws, :] = (xs_ref[j, ws, :].astype(F32) + res[j * rows:(j + 1) * rows]).astype(BF16)

    @pl.when(fits)
    def _():
        put(SCATTER_WIN, first)

    @pl.when(jnp.logical_not(fits))
    def _():
        put(CAP_LAT, [jnp.int32(0)] * ne)


def _gather_tiles_call(tile_start, slot, aff, h2):
    bsz = h2.shape[0]
    ne = SCATTER_EXPERTS
    return pl.pallas_call(
        _gather_tiles_kernel,
        grid=(bsz, N_EXPERTS // ne, NLAT),
        in_specs=[pl.BlockSpec(memory_space=pltpu.SMEM),
                  pl.BlockSpec((None, N_EXPERTS, BLK), lambda b, g, r: (b, 0, r)),
                  pl.BlockSpec((None, N_EXPERTS, BLK), lambda b, g, r: (b, 0, r)),
                  pl.BlockSpec((None, BLK, D), lambda b, g, r: (b, r, 0))],
        out_specs=[pl.BlockSpec((ne, CAP_LAT, D), lambda b, g, r: (g, b, 0)),
                   pl.BlockSpec((ne, CAP_LAT, 128), lambda b, g, r: (g, b, 0))],
        out_shape=[jax.ShapeDtypeStruct((N_EXPERTS, bsz * CAP_LAT, D), BF16),
                   jax.ShapeDtypeStruct((N_EXPERTS, bsz * CAP_LAT, 128), F32)],
        compiler_params=_cp(3), name="gather_tiles",
    )(tile_start[:, :, :NLAT + 1].astype(jnp.int32), slot, aff, h2)


def _ffn_kernel(*refs, rl, rc, nf):
    if rc:
        xl_ref, xc_ref, gl_ref, gc_ref, wg_ref, wu_ref, wd_ref, yl_ref, yc_ref, hid = refs
        groups = ((xl_ref, gl_ref, yl_ref, 0, rl), (xc_ref, gc_ref, yc_ref, rl, rc))
    else:
        xl_ref, gl_ref, wg_ref, wu_ref, wd_ref, yl_ref, hid = refs
        groups = ((xl_ref, gl_ref, yl_ref, 0, rl),)
    s = pl.program_id(1)

    @pl.when(s < nf)
    def _():
        wg = wg_ref[...].astype(BF16)
        wu = wu_ref[...].astype(BF16)
        for x_ref, _, _, r0, nr in groups:
            xv = x_ref[...]
            a = _dot(xv, wg)
            u = _dot(xv, wu)
            hid[s, r0:r0 + nr, :] = ((a * _sigmoid(a)) * u).astype(BF16)

    @pl.when(s >= nf)
    def _():
        wd = wd_ref[...].astype(BF16)
        for _, g_ref, y_ref, r0, nr in groups:
            hm = jnp.concatenate([hid[k, r0:r0 + nr, :] for k in range(nf)], axis=1)
            y = _dot(hm, wd)
            for c in range(FF_TILE // 128):
                cs = slice(128 * c, 128 * (c + 1))
                y_ref[:, cs] = (y[:, cs] * g_ref[...]).astype(BF16)


def _ffn_call(layer, xs_l, g_l, xs_c, g_c, w_gate, w_up, w_down):
    rl = xs_l.shape[1]
    rc = 0 if xs_c is None else xs_c.shape[1]
    nf = EXPERT_FF // FF_TILE
    nd = D // FF_TILE
    up = lambda e, s: (layer, e, 0, jnp.minimum(s, nf - 1))
    down = lambda s: jnp.maximum(s - nf, 0)
    in_specs = [pl.BlockSpec((None, rl, D), lambda e, s: (e, 0, 0))]
    args = [xs_l]
    if rc:
        in_specs.append(pl.BlockSpec((None, rc, D), lambda e, s: (e, 0, 0)))
        args.append(xs_c)
    in_specs.append(pl.BlockSpec((None, rl, 128), lambda e, s: (e, 0, 0)))
    args.append(g_l)
    if rc:
        in_specs.append(pl.BlockSpec((None, rc, 128), lambda e, s: (e, 0, 0)))
        args.append(g_c)
    in_specs += [pl.BlockSpec((None, None, D, FF_TILE), up),
                 pl.BlockSpec((None, None, D, FF_TILE), up),
                 pl.BlockSpec((None, None, EXPERT_FF, FF_TILE), lambda e, s: (layer, e, 0, down(s)))]
    args += [w_gate, w_up, w_down]
    out_specs = [pl.BlockSpec((None, rl, FF_TILE), lambda e, s: (e, 0, down(s)))]
    out_shape = [jax.ShapeDtypeStruct((N_EXPERTS, rl, D), BF16)]
    if rc:
        out_specs.append(pl.BlockSpec((None, rc, FF_TILE), lambda e, s: (e, 0, down(s))))
        out_shape.append(jax.ShapeDtypeStruct((N_EXPERTS, rc, D), BF16))
    return pl.pallas_call(
        functools.partial(_ffn_kernel, rl=rl, rc=rc, nf=nf),
        grid=(N_EXPERTS, nf + nd),
        in_specs=in_specs, out_specs=out_specs, out_shape=out_shape,
        scratch_shapes=[pltpu.VMEM((nf, rl + rc, FF_TILE), BF16)],
        compiler_params=_cp(2), name="expert_ffn",
    )(*args)


def _scatter_kernel(*refs, ne, ng, with_ctx, has_next):
    refs = list(refs)
    start_ref, slot_ref, yl_ref = refs[:3]
    refs = refs[3:]
    yc_ref = refs.pop(0) if with_ctx else None
    x_ref, gpost_ref, mod_ref = refs[:3]
    refs = refs[3:]
    if has_next:
        gpre_ref, modn_ref = refs[:2]
        refs = refs[2:]
    xo_ref = refs[0]
    hn_ref = refs[1] if has_next else None
    acc = refs[-1]
    eg = pl.program_id(1)
    r = pl.program_id(2)

    def full(cap, y_ref):
        kk = ne * cap
        if cap % 128 == 0:
            want = lax.broadcasted_iota(jnp.int32, (BLK, cap), 1).astype(F32)
            pt = jnp.concatenate([jnp.where(slot_ref[:, j:j + 1] == want, 1.0, 0.0).astype(BF16)
                                  for j in range(ne)], axis=1)
        else:
            er = lax.broadcasted_iota(jnp.int32, (128, kk), 0)
            ec = lax.broadcasted_iota(jnp.int32, (128, kk), 1)
            expand = jnp.where(er == ec // cap, 1.0, 0.0).astype(BF16)
            spread = _dot(slot_ref[...].astype(BF16), expand)
            want = (lax.broadcasted_iota(jnp.int32, (BLK, kk), 1) % cap).astype(F32)
            pt = jnp.where(spread == want, 1.0, 0.0).astype(BF16)
        return _dot(pt, y_ref[...].reshape(kk, D))

    def windows(first):
        lane = lax.broadcasted_iota(jnp.int32, (BLK, 128), 1).astype(F32)
        low = lane < SCATTER_WIN
        tiles = []
        for j in range(0, ne, 2):
            rel_a = slot_ref[:, j:j + 1] - first[j].astype(F32)
            rel_b = slot_ref[:, j + 1:j + 2] - first[j + 1].astype(F32) + SCATTER_WIN
            tiles.append(jnp.where(jnp.where(low, rel_a, rel_b) == lane, 1.0, 0.0).astype(BF16))
        yw = jnp.concatenate([yl_ref[j, pl.ds(pl.multiple_of(first[j], 16), SCATTER_WIN), :]
                              for j in range(ne)], axis=0)
        return _dot(jnp.concatenate(tiles, axis=1), yw)

    def emit(moe):
        x2 = x_ref[...] + mod_ref[5:6, :] * (_rms(moe) * gpost_ref[...])
        xo_ref[...] = x2
        if has_next:
            hn = (_rms(x2) * gpre_ref[...]) * (1.0 + modn_ref[1:2, :]) + modn_ref[0:1, :]
            hn_ref[...] = hn.astype(BF16)

    def settle(share):
        if ng > 1:
            @pl.when(eg == 0)
            def _():
                acc[r] = share()
        if ng > 2:
            @pl.when(jnp.logical_and(eg > 0, eg < ng - 1))
            def _():
                acc[r] += share()

        @pl.when(eg == ng - 1)
        def _():
            emit(share() if ng == 1 else acc[r] + share())

    @pl.when(r < NLAT)
    def _():
        first, fits = _window_starts(start_ref, pl.program_id(0), eg * ne, r, ne)

        @pl.when(fits)
        def _():
            settle(lambda: windows(first))

        @pl.when(jnp.logical_not(fits))
        def _():
            settle(lambda: full(CAP_LAT, yl_ref))

    if with_ctx:
        @pl.when(r == NLAT)
        def _():
            settle(lambda: full(CAP_CTX, yc_ref))


def _scatter_call(tile_start, slot_t, y_l, y_c, x_mid, g_post, modtab, g_pre_next, modtab_next):
    bsz = x_mid.shape[0]
    with_ctx = y_c is not None
    has_next = g_pre_next is not None
    n_tiles = NBLK if with_ctx else NLAT
    ne = SCATTER_EXPERTS
    ng = N_EXPERTS // ne
    vec = pl.BlockSpec((1, D), lambda b, g, r: (0, 0))
    modspec = pl.BlockSpec((None, None, 8, D), lambda b, g, r: (b, r // NLAT, 0, 0))
    late = lambda b, g, r: (b, jnp.where(g == ng - 1, r, 0), 0)
    in_specs = [pl.BlockSpec(memory_space=pltpu.SMEM),
                pl.BlockSpec((None, None, BLK, 128), lambda b, g, r: (b, g, r, 0)),
                pl.BlockSpec((ne, CAP_LAT, D), lambda b, g, r: (g, b, 0))]
    args = [tile_start[:, :, :NLAT + 1].astype(jnp.int32), slot_t, y_l]
    if with_ctx:
        in_specs.append(pl.BlockSpec((ne, CAP_CTX, D), lambda b, g, r: (g, b, 0)))
        args.append(y_c)
    in_specs += [pl.BlockSpec((None, BLK, D), late), vec, modspec]
    args += [x_mid, g_post.reshape(1, D), modtab]
    if has_next:
        in_specs += [vec, modspec]
        args += [g_pre_next.reshape(1, D), modtab_next]
    out_specs = [pl.BlockSpec((None, BLK, D), late)]
    out_shape = [jax.ShapeDtypeStruct((bsz, n_tiles * BLK, D), F32)]
    if has_next:
        out_specs.append(pl.BlockSpec((None, BLK, D), late))
        out_shape.append(jax.ShapeDtypeStruct((bsz, n_tiles * BLK, D), BF16))
    return pl.pallas_call(
        functools.partial(_scatter_kernel, ne=ne, ng=ng, with_ctx=with_ctx, has_next=has_next),
        grid=(bsz, ng, n_tiles),
        in_specs=in_specs, out_specs=out_specs, out_shape=out_shape,
        scratch_shapes=[pltpu.VMEM((n_tiles, BLK, D), F32)],
        compiler_params=_cp(3), name="scatter",
    )(*args)


def _pack_kernel(w_ref, o_ref):
    src = np.cumsum((0, WA, WA, WA, KB, VB, 2 * DEC_RANK, KB, VB, WC, WC))
    order = ((4, C_VB), (7, C_GB), (0, C_QA), (1, C_KA), (2, C_VA), (3, C_KB), (6, C_QB), (8, C_UC),
             (9, C_VC), (5, C_DEC))
    for seg, dst in order:
        lo, hi = int(src[seg]), int(src[seg + 1])
        o_ref[:, dst:dst + hi - lo] = w_ref[:, lo:hi].astype(BF16)
    tail = C_DEC + 2 * DEC_RANK
    o_ref[:, tail:] = jnp.zeros((o_ref.shape[0], NP - tail), BF16)


def _pack_w_in(w_in, layer):
    rows = 256
    n_in = w_in.shape[-1]
    return pl.pallas_call(
        _pack_kernel,
        grid=(D // rows,),
        in_specs=[pl.BlockSpec((rows, n_in), lambda i: (layer * (D // rows) + i, 0))],
        out_specs=pl.BlockSpec((rows, NP), lambda i: (i, 0)),
        out_shape=jax.ShapeDtypeStruct((D, NP), BF16),
        compiler_params=_cp(1), name="pack_w_in",
    )(w_in.reshape(DEPTH * D, n_in))


def kernel(x, c, ctx, c_ctx, w_ada, b_ada, g_pre_mix, g_post_mix, g_pre_ffn, g_post_ffn, w_in, w_dec,
           b_dec, rpb, g_gla, ln_v_g, ln_v_b, w_sp, b_sp, w_out, w_router, w_gate, w_up, w_down):
    bsz = x.shape[0]
    c_all = jnp.zeros((16, D), F32).at[:bsz].set(c).at[8].set(c_ctx)
    mods = _ada_call(c_all, w_ada, b_ada).reshape(DEPTH, 16, 6, D)
    lat = mods[:, :bsz]
    cx = jnp.broadcast_to(mods[:, 8:9], lat.shape)
    modtab = jnp.pad(jnp.stack([lat, cx], axis=2), ((0, 0), (0, 0), (0, 0), (0, 2), (0, 0)))

    rope = _rope_tables()
    mstack = jnp.asarray(_gla_matrices(), BF16)
    lev_ids = jnp.asarray(_gla_level_ids())
    hmask = jnp.asarray(np.stack([(np.arange(128) < DH_A), (np.arange(128) >= DH_A)]) * DH_A ** -0.5, F32)
    upper = jnp.asarray(np.triu(np.ones((T, T), np.float32), 1), BF16)

    stream = (x, ctx)
    h = _prenorm_call(x, ctx, g_pre_mix[0], modtab[0])
    for l in range(DEPTH):
        last = l == DEPTH - 1
        nb = NLAT if last else NBLK
        P = _proj_call(h.reshape(bsz * TT, D), _pack_w_in(w_in, l)).reshape(bsz, TT, NP)
        bs_rep = jnp.broadcast_to(b_sp[l][:, :, None], (GC, MIX_CHUNK, 128))
        o_a, o_c = _na_gmlp_call(P, _na_bias(rpb[l]), hmask, ln_v_g[l], ln_v_b[l], w_sp[l], bs_rep, nb)
        wd_pad = jnp.zeros((2, 128, KB), F32)
        wd_pad = wd_pad.at[0, :DEC_RANK].set(w_dec[l, 0]).at[1, DEC_RANK:2 * DEC_RANK].set(w_dec[l, 1])
        o_g = _gla_call(P, wd_pad, b_dec[l].reshape(2, 1, KB), rope, mstack, lev_ids)
        wr_pad = jnp.pad(w_router[l], ((0, 0), (0, 128 - N_EXPERTS)))
        x_mid, h2, lg = _out_call(o_a, o_g, P, o_c, w_out[l].astype(BF16), stream, g_gla[l], g_post_mix[l],
                                  g_pre_ffn[l], modtab[l], wr_pad, nb)
        routed = _route_call(lg, upper, not last)
        slot_l, aff_l, slot_t, tile_start = routed[:4]
        xs_l, gt_l = _gather_tiles_call(tile_start, slot_l, aff_l, h2)
        if last:
            (y_l,) = _ffn_call(l, xs_l, gt_l, None, None, w_gate, w_up, w_down)
            (stream,) = _scatter_call(tile_start, slot_t, y_l, None, x_mid, g_post_ffn[l], modtab[l], None,
                                      None)
        else:
            slot_c, aff_c = routed[4:]
            xs_c, gt_c = _gather_call(h2, [(slot_c, aff_c, L, CAP_CTX, NLAT)])
            y_l, y_c = _ffn_call(l, xs_l, gt_l, xs_c, gt_c, w_gate, w_up, w_down)
            stream, h = _scatter_call(tile_start, slot_t, y_l, y_c, x_mid, g_post_ffn[l], modtab[l],
                                      g_pre_mix[l + 1], modtab[l + 1])
    return stream
```
